```python
import math
import jax, jax.numpy as jnp
from jax import lax
import numpy as np

D_MODEL = 1024
BATCH = 8
SEQ = 2048
DEPTH = 1
DEC_BATCH = 128
DEC_SEQ = 4
PAST_LEN = 16384
PAGE_SIZE = 128

N_META = 16
RW_WIDTH = D_MODEL // 2
RW_HEAD_DIM = 64
RW_HEADS = RW_WIDTH // RW_HEAD_DIM
RW_DECAY_LORA = 64
RW_AAA_LORA = 64
RW_GATE_LORA = 128
RW_GN_EPS = RW_HEAD_DIM * 1e-5
GDN_WIDTH = D_MODEL // 2
GDN_HEAD_DIM = 128
GDN_HEADS = GDN_WIDTH // GDN_HEAD_DIM
GDN_CONV = 4
GDN_CHUNK = 64
GDN_CONV_CH = 3 * GDN_WIDTH
N_GROUPS = 4
EXPERTS_PER_GROUP = 8
N_EXPERTS = N_GROUPS * EXPERTS_PER_GROUP
EXPERT_FF = 512
TOP_K_IN_GROUP = 2
RMS_EPS = 1e-6

RW_SIZES = (RW_WIDTH, RW_WIDTH, RW_WIDTH, RW_DECAY_LORA, RW_AAA_LORA, RW_GATE_LORA)
RW_COLS = sum(RW_SIZES)
GDN_COLS = 4 * GDN_WIDTH + 2 * GDN_HEADS
IN_COLS = RW_COLS + GDN_COLS + 2 * D_MODEL

kernel_name = 'hybrid_rwkv7_gdn_hmoe_step'


def _split(x, sizes):
    return jnp.split(x, np.cumsum(sizes)[:-1].tolist(), axis=-1)


def rmsnorm(x, w):
    xf = x.astype(jnp.float32)
    y = xf * lax.rsqrt(jnp.mean(xf * xf, axis=-1, keepdims=True) + RMS_EPS)
    return (y * w.astype(jnp.float32)).astype(x.dtype)


def l2norm(x):
    return x * lax.rsqrt(jnp.sum(x * x, axis=-1, keepdims=True) + 1e-6)


def rwkv7_scan(S0, r, w, k, v, a_vec, b_vec):
    def step(S, inp):
        r_t, w_t, k_t, v_t, a_t, b_t = inp
        sa = jnp.einsum('bhij,bhj->bhi', S, a_t)
        S = S * w_t[:, :, None, :] + sa[..., :, None] * b_t[..., None, :] + v_t[..., :, None] * k_t[..., None, :]
        return S, jnp.einsum('bhij,bhj->bhi', S, r_t)
    xs = tuple(jnp.moveaxis(t, 1, 0) for t in (r, w, k, v, a_vec, b_vec))
    S, y = lax.scan(step, S0, xs)
    return jnp.moveaxis(y, 0, 1), S


def rwkv7_branch(cur, prev, S0, p):
    B, T, _ = cur.shape
    H, N = RW_HEADS, RW_HEAD_DIM
    f32 = jnp.float32
    c = lambda name: p[name].astype(f32)
    cf = cur.astype(f32)
    zc = cf + (prev.astype(f32) - cf) * c('mu_shift')
    r, k, v, xw, xa, xg = _split(zc, RW_SIZES)
    w_raw = -jax.nn.softplus(-(c('rw_w0') + jnp.tanh(xw) @ c('rw_w2'))) - 0.5
    decay = jnp.exp(-jnp.exp(w_raw))
    a = jax.nn.sigmoid(c('rw_a0') + xa @ c('rw_a2'))
    g = jax.nn.sigmoid(xg) @ c('rw_g2')
    hd = lambda t: t.reshape(B, T, H, N)
    kk = l2norm(hd(k * c('rw_k_k')))
    k = k * (1.0 + (a - 1.0) * c('rw_k_a'))
    a_h = hd(a)
    y, S = rwkv7_scan(S0.astype(f32), hd(r), hd(decay), hd(k), hd(v), -kk, kk * a_h)
    mu = jnp.mean(y, axis=-1, keepdims=True)
    var = jnp.mean(jnp.square(y - mu), axis=-1, keepdims=True)
    y = ((y - mu) * lax.rsqrt(var + RW_GN_EPS)).reshape(B, T, RW_WIDTH) * c('rw_lnx_w') + c('rw_lnx_b')
    bonus = jnp.sum(hd(r) * hd(k) * c('rw_r_k'), axis=-1, keepdims=True) * hd(v)
    y = (y + bonus.reshape(B, T, RW_WIDTH)) * g
    return y.astype(cur.dtype), S.astype(S0.dtype)


def gdn_chunked(S0, q, k, v, g, beta, chunk):
    B, L, H, Dk = q.shape
    Dv = v.shape[-1]
    n = L // chunk

    def blk(t):
        t = jnp.swapaxes(t, 1, 2)
        return t.reshape(B, H, n, chunk, *t.shape[3:])

    q, k, v, g, beta = blk(q), blk(k), blk(v), blk(g), blk(beta)
    g = jnp.cumsum(g, axis=-1)
    pos = jnp.arange(chunk)
    causal = pos[:, None] >= pos[None, :]
    strict = pos[:, None] > pos[None, :]
    decay = jnp.exp(jnp.where(causal, g[..., :, None] - g[..., None, :], -jnp.inf))
    kb = k * beta[..., None]
    A = jnp.where(strict, jnp.einsum('bhnid,bhnjd->bhnij', kb, k) * decay, 0.0)
    M = A + jnp.eye(chunk, dtype=A.dtype)
    rhs = jnp.concatenate([v * beta[..., None], kb * jnp.exp(g)[..., None]], axis=-1)
    sol = lax.linalg.triangular_solve(M, rhs, left_side=True, lower=True, unit_diagonal=True)
    u, w = sol[..., :Dv], sol[..., Dv:]
    qk = jnp.einsum('bhnid,bhnjd->bhnij', q, k) * decay
    q_dec = q * jnp.exp(g)[..., None]
    g_last = g[..., -1]
    k_dec = k * jnp.exp(g_last[..., None] - g)[..., None]

    def step(S, inp):
        u_n, w_n, qk_n, qd_n, kd_n, gl_n = inp
        v_new = u_n - jnp.einsum('bhcd,bhde->bhce', w_n, S)
        o = jnp.einsum('bhcd,bhde->bhce', qd_n, S) + jnp.einsum('bhij,bhje->bhie', qk_n, v_new)
        S = S * jnp.exp(gl_n)[..., None, None] + jnp.einsum('bhcd,bhce->bhde', kd_n, v_new)
        return S, o

    xs = tuple(jnp.moveaxis(t, 2, 0) for t in (u, w, qk, q_dec, k_dec, g_last))
    S, o = lax.scan(step, S0, xs)
    o = jnp.moveaxis(o, 0, 2).reshape(B, H, L, Dv)
    return jnp.swapaxes(o, 1, 2), S


def gdn_branch(qkv_raw, z, a_raw, b_raw, conv_prev, S0, segments, p):
    B, T, _ = qkv_raw.shape
    H, Dh = GDN_HEADS, GDN_HEAD_DIM
    f32 = jnp.float32
    xpad = jnp.concatenate([conv_prev.astype(qkv_raw.dtype), qkv_raw], axis=1)
    conv = lax.conv_general_dilated(xpad, p['gdn_conv_w'][:, None, :].astype(xpad.dtype), (1,), 'VALID',
                                    dimension_numbers=('NWC', 'WIO', 'NWC'), feature_group_count=GDN_CONV_CH)
    qkv = jax.nn.silu(conv.astype(f32))
    q, k, v = _split(qkv, (GDN_WIDTH, GDN_WIDTH, GDN_WIDTH))
    q = l2norm(q.reshape(B, T, H, Dh)) * (Dh ** -0.5)
    k = l2norm(k.reshape(B, T, H, Dh))
    v = v.reshape(B, T, H, Dh)
    g = -jnp.exp(p['gdn_A_log'].astype(f32)) * jax.nn.softplus(a_raw.astype(f32) + p['gdn_dt_bias'].astype(f32))
    beta = jax.nn.sigmoid(b_raw.astype(f32))
    S = S0.astype(f32)
    outs = []
    start = 0
    for length, chunk in segments:
        sl = slice(start, start + length)
        o, S = gdn_chunked(S, q[:, sl], k[:, sl], v[:, sl], g[:, sl], beta[:, sl], chunk)
        outs.append(o)
        start += length
    o = jnp.concatenate(outs, axis=1) if len(outs) > 1 else outs[0]
    o = rmsnorm(o, p['gdn_norm_w']) * jax.nn.silu(z.astype(f32).reshape(B, T, H, Dh))
    return o.reshape(B, T, GDN_WIDTH).astype(qkv_raw.dtype), xpad[:, T:], S.astype(S0.dtype)


def token_mixer(xn, shift_prev, wkv0, conv0, gdn0, segments, p):
    P = xn @ p['w_in']
    rw_cur, gdn_cols, gate_cols = _split(P, (RW_COLS, GDN_COLS, 2 * D_MODEL))
    prev_first = (shift_prev.astype(xn.dtype) @ p['w_in'][:, :RW_COLS])[:, None]
    rw_prev = jnp.concatenate([prev_first, rw_cur[:, :-1]], axis=1)
    yA, wkv = rwkv7_branch(rw_cur, rw_prev, wkv0, p)
    qkv_raw, z, a_raw, b_raw = _split(gdn_cols, (3 * GDN_WIDTH, GDN_WIDTH, GDN_HEADS, GDN_HEADS))
    yB, conv, gdn = gdn_branch(qkv_raw, z, a_raw, b_raw, conv0, gdn0, segments, p)
    gA, gB = _split(gate_cols, (D_MODEL, D_MODEL))
    merged = jax.nn.sigmoid(gA) * (yA @ p['w_oA']) + jax.nn.sigmoid(gB) * (yB @ p['w_oB'])
    return merged @ p['w_o'], xn[:, -1].astype(shift_prev.dtype), wkv, conv.astype(conv0.dtype), gdn


def hier_moe(x, p):
    B, T, D = x.shape
    f32 = jnp.float32
    xt = x.reshape(B * T, D)
    gprob = jax.nn.softmax((xt @ p['router_g'] + p['router_g_b']).astype(f32), axis=-1)
    gp, gidx = lax.top_k(gprob, 1)
    elog = (xt @ p['router_e'] + p['router_e_b']).astype(f32).reshape(-1, N_GROUPS, EXPERTS_PER_GROUP)
    elog = jnp.take_along_axis(elog, gidx[:, :, None], axis=1)[:, 0]
    ev, eidx = lax.top_k(elog, TOP_K_IN_GROUP)
    wts = gp * jax.nn.softmax(ev, axis=-1)
    eid = gidx * EXPERTS_PER_GROUP + eidx
    gate = jnp.sum(jax.nn.one_hot(eid, N_EXPERTS, dtype=f32) * wts[..., None], axis=1)

    def expert(acc, inp):
        wg, wu, wd, ge = inp
        h = jax.nn.silu(xt @ wg) * (xt @ wu)
        return acc + ge[:, None] * (h @ wd).astype(f32), None

    y, _ = lax.scan(expert, jnp.zeros(xt.shape, f32), (p['moe_w_gate'], p['moe_w_up'], p['moe_w_down'], gate.T))
    return y.astype(x.dtype).reshape(B, T, D)


def decoder_layer(h, n_meta, segments, shift_prev, wkv0, conv0, gdn0, p):
    xn = rmsnorm(h, p['norm1_w'])
    mix, shift, wkv, conv, gdn = token_mixer(xn, shift_prev, wkv0, conv0, gdn0, segments, p)
    h = (h + mix)[:, n_meta:]
    h = h + hier_moe(rmsnorm(h, p['norm2_w']), p)
    return h, (shift, wkv, conv, gdn)


def setup_inputs(seed: int = 0) -> dict:
    key = jax.random.key(seed)
    ks = iter(jax.random.split(key, 48))
    f32 = jnp.float32
    L = DEPTH
    nrm = lambda shape, scale=1.0: jax.random.normal(next(ks), shape, f32) * scale
    gain = lambda shape: 1.0 + nrm(shape, 0.01)
    dt = jnp.exp(jax.random.uniform(next(ks), (L, GDN_HEADS), f32, math.log(1e-3), math.log(1e-1)))
    return {
        'x_prompt': nrm((BATCH, SEQ, D_MODEL)),
        'x_sample': nrm((DEC_BATCH, DEC_SEQ, D_MODEL)),
        'state_shift': nrm((L, DEC_BATCH, D_MODEL)),
        'state_wkv': nrm((L, DEC_BATCH, RW_HEADS, RW_HEAD_DIM, RW_HEAD_DIM), 0.3),
        'state_conv': nrm((L, DEC_BATCH, GDN_CONV - 1, GDN_CONV_CH)),
        'state_gdn': nrm((L, DEC_BATCH, GDN_HEADS, GDN_HEAD_DIM, GDN_HEAD_DIM), 0.1),
        'meta_tokens': nrm((N_META, D_MODEL)),
        'norm1_w': gain((L, D_MODEL)),
        'w_in': nrm((L, D_MODEL, IN_COLS), D_MODEL ** -0.5),
        'mu_shift': jax.random.uniform(next(ks), (L, RW_COLS), f32),
        'rw_w0': -2.0 + nrm((L, RW_WIDTH), 0.5),
        'rw_w2': nrm((L, RW_DECAY_LORA, RW_WIDTH), 0.1),
        'rw_a0': nrm((L, RW_WIDTH), 0.1),
        'rw_a2': nrm((L, RW_AAA_LORA, RW_WIDTH), 0.1),
        'rw_g2': nrm((L, RW_GATE_LORA, RW_WIDTH), RW_GATE_LORA ** -0.5),
        'rw_k_k': 0.85 + nrm((L, RW_WIDTH), 0.05),
        'rw_k_a': 1.0 + nrm((L, RW_WIDTH), 0.05),
        'rw_r_k': nrm((L, RW_HEADS, RW_HEAD_DIM), 0.1),
        'rw_lnx_w': gain((L, RW_WIDTH)),
        'rw_lnx_b': nrm((L, RW_WIDTH), 0.01),
        'gdn_conv_w': nrm((L, GDN_CONV, GDN_CONV_CH), GDN_CONV ** -0.5),
        'gdn_A_log': jnp.log(jax.random.uniform(next(ks), (L, GDN_HEADS), f32, 1.0, 16.0)),
        'gdn_dt_bias': dt + jnp.log(-jnp.expm1(-dt)),
        'gdn_norm_w': gain((L, GDN_HEAD_DIM)),
        'w_oA': nrm((L, RW_WIDTH, D_MODEL), RW_WIDTH ** -0.5),
        'w_oB': nrm((L, GDN_WIDTH, D_MODEL), GDN_WIDTH ** -0.5),
        'w_o': nrm((L, D_MODEL, D_MODEL), D_MODEL ** -0.5),
        'norm2_w': gain((L, D_MODEL)),
        'router_g': nrm((L, D_MODEL, N_GROUPS), D_MODEL ** -0.5),
        'router_g_b': nrm((L, N_GROUPS), 0.01),
        'router_e': nrm((L, D_MODEL, N_EXPERTS), D_MODEL ** -0.5),
        'router_e_b': nrm((L, N_EXPERTS), 0.01),
        'moe_w_gate': nrm((L, N_EXPERTS, D_MODEL, EXPERT_FF), D_MODEL ** -0.5),
        'moe_w_up': nrm((L, N_EXPERTS, D_MODEL, EXPERT_FF), D_MODEL ** -0.5),
        'moe_w_down': nrm((L, N_EXPERTS, EXPERT_FF, D_MODEL), EXPERT_FF ** -0.5),
        'norm_f_w': gain((D_MODEL,)),
    }


def reference(x_prompt, x_sample, state_shift, state_wkv, state_conv, state_gdn, meta_tokens, norm1_w, w_in,
              mu_shift, rw_w0, rw_w2, rw_a0, rw_a2, rw_g2, rw_k_k, rw_k_a, rw_r_k, rw_lnx_w, rw_lnx_b, gdn_conv_w,
              gdn_A_log, gdn_dt_bias, gdn_norm_w, w_oA, w_oB, w_o, norm2_w, router_g, router_g_b, router_e,
              router_e_b, moe_w_gate, moe_w_up, moe_w_down, norm_f_w):
    stacked = dict(norm1_w=norm1_w, w_in=w_in, mu_shift=mu_shift, rw_w0=rw_w0, rw_w2=rw_w2, rw_a0=rw_a0,
                   rw_a2=rw_a2, rw_g2=rw_g2, rw_k_k=rw_k_k, rw_k_a=rw_k_a, rw_r_k=rw_r_k, rw_lnx_w=rw_lnx_w,
                   rw_lnx_b=rw_lnx_b, gdn_conv_w=gdn_conv_w, gdn_A_log=gdn_A_log, gdn_dt_bias=gdn_dt_bias,
                   gdn_norm_w=gdn_norm_w, w_oA=w_oA, w_oB=w_oB, w_o=w_o, norm2_w=norm2_w, router_g=router_g,
                   router_g_b=router_g_b, router_e=router_e, router_e_b=router_e_b, moe_w_gate=moe_w_gate,
                   moe_w_up=moe_w_up, moe_w_down=moe_w_down)
    B, S_len, _ = x_prompt.shape
    dt = x_prompt.dtype
    meta = jnp.broadcast_to(meta_tokens.astype(dt)[None], (B, N_META, D_MODEL))
    h_p = jnp.concatenate([meta, x_prompt], axis=1)
    h_s = x_sample
    seg_p = ((N_META, N_META), (S_len, GDN_CHUNK))
    seg_s = ((x_sample.shape[1], x_sample.shape[1]),)
    new_p, new_s = [], []
    for layer in range(DEPTH):
        p = {name: arr[layer] for name, arr in stacked.items()}
        n_drop = N_META if layer == DEPTH - 1 else 0
        h_p, st_p = decoder_layer(
            h_p, n_drop, seg_p,
            jnp.zeros((B, D_MODEL), dt),
            jnp.zeros((B, RW_HEADS, RW_HEAD_DIM, RW_HEAD_DIM), dt),
            jnp.zeros((B, GDN_CONV - 1, GDN_CONV_CH), dt),
            jnp.zeros((B, GDN_HEADS, GDN_HEAD_DIM, GDN_HEAD_DIM), dt), p)
        h_s, st_s = decoder_layer(h_s, 0, seg_s, state_shift[layer], state_wkv[layer], state_conv[layer],
                                  state_gdn[layer], p)
        new_p.append(st_p)
        new_s.append(st_s)
    y_prompt = rmsnorm(h_p, norm_f_w)
    y_sample = rmsnorm(h_s, norm_f_w)
    shift_p = jnp.stack([s[0] for s in new_p])
    wkv_p = jnp.stack([s[1] for s in new_p])
    conv_p = jnp.stack([s[2] for s in new_p])
    gdn_p = jnp.stack([s[3] for s in new_p])
    shift_s = jnp.stack([s[0] for s in new_s])
    wkv_s = jnp.stack([s[1] for s in new_s])
    conv_s = jnp.stack([s[2] for s in new_s])
    gdn_s = jnp.stack([s[3] for s in new_s])
    return (y_prompt, y_sample, shift_p, wkv_p, conv_p, gdn_p, shift_s, wkv_s, conv_s, gdn_s)
```

```python
import functools
import math

import jax
import jax.numpy as jnp
from jax import lax
from jax.experimental import pallas as pl
from jax.experimental.pallas import tpu as pltpu

F32 = jnp.float32
BF16 = jnp.bfloat16
HIGHEST = lax.Precision.HIGHEST

D_MODEL = 1024
N_META = 16
RW_WIDTH = 512
RW_HEAD_DIM = 64
RW_HEADS = 8
RW_DECAY_LORA = 64
RW_AAA_LORA = 64
RW_GATE_LORA = 128
RW_COLS = 3 * RW_WIDTH + RW_DECAY_LORA + RW_AAA_LORA + RW_GATE_LORA
RW_GN_EPS = RW_HEAD_DIM * 1e-5
GDN_WIDTH = 512
GDN_HEAD_DIM = 128
GDN_HEADS = 4
GDN_CONV = 4
GDN_CONV_CH = 3 * GDN_WIDTH
GDN_CHUNK = 64
N_GROUPS = 4
EXPERTS_PER_GROUP = 8
N_EXPERTS = 32
EXPERT_FF = 512
RMS_EPS = 1e-6

LANES = 128
SUBLANES = 8
VMEM_LIMIT_BYTES = 56 * 1024 * 1024

COL_RW = 0
COL_QKV = COL_RW + RW_COLS
COL_Z = COL_QKV + GDN_CONV_CH
COL_GATE = COL_Z + GDN_WIDTH
COL_AB = COL_GATE + 2 * D_MODEL
IN_COLS_PACKED = COL_AB + LANES

SCAN_SEQS = 8
SCAN_ROWS = RW_HEAD_DIM // 2


def _cparams(*sem):
    return pltpu.CompilerParams(dimension_semantics=sem, vmem_limit_bytes=VMEM_LIMIT_BYTES)


def _const_spec(shape):
    zeros = (0,) * len(shape)
    return pl.BlockSpec(shape, lambda *_: zeros)


def _dot(a, b):
    return jnp.dot(a.astype(BF16), b.astype(BF16), preferred_element_type=F32)


def _dot_hi(a, b):
    return jnp.dot(a, b, preferred_element_type=F32, precision=HIGHEST)


def _dot_nt(a, b, hi=False):
    dims = (((1,), (1,)), ((), ()))
    if hi:
        return lax.dot_general(a, b, dims, precision=HIGHEST, preferred_element_type=F32)
    return lax.dot_general(a.astype(BF16), b.astype(BF16), dims, preferred_element_type=F32)


def _sigmoid(x):
    return 1.0 / (1.0 + jnp.exp(-x))


def _softplus(x):
    return jnp.maximum(x, 0.0) + jnp.log1p(jnp.exp(-jnp.abs(x)))


def _rmsnorm(x, w):
    ms = jnp.mean(x * x, axis=-1, keepdims=True)
    return x * lax.rsqrt(ms + RMS_EPS) * w


def _block_ones(width, seg):
    r = jnp.arange(width) // seg
    return (r[:, None] == r[None, :]).astype(F32)


def _in_proj_kernel(x_ref, nw_ref, w_ref, xn_ref, rw_ref, qkv_ref, z_ref, gate_ref, ab_ref, *, norm):
    x = x_ref[...]
    xn = _rmsnorm(x, nw_ref[...]) if norm else x
    xn_ref[...] = xn
    xb = xn.astype(BF16)
    rw_ref[...] = jnp.dot(xb, w_ref[:, COL_RW:COL_QKV], preferred_element_type=F32)
    qkv_ref[...] = jnp.dot(xb, w_ref[:, COL_QKV:COL_Z], preferred_element_type=F32)
    z_ref[...] = jnp.dot(xb, w_ref[:, COL_Z:COL_GATE], preferred_element_type=F32)
    gate_ref[...] = jnp.dot(xb, w_ref[:, COL_GATE:COL_AB], preferred_element_type=F32)
    ab_ref[...] = jnp.dot(xb, w_ref[:, COL_AB:IN_COLS_PACKED], preferred_element_type=F32)


def _in_proj(x, norm_w, w_packed, tm, norm=True):
    n = x.shape[0]
    widths = (D_MODEL, RW_COLS, GDN_CONV_CH, GDN_WIDTH, 2 * D_MODEL, LANES)
    row = lambda w: pl.BlockSpec((tm, w), lambda i: (i, 0))
    return pl.pallas_call(
        functools.partial(_in_proj_kernel, norm=norm),
        grid=(n // tm,),
        in_specs=[row(D_MODEL), _const_spec((1, D_MODEL)), _const_spec(w_packed.shape)],
        out_specs=[row(w) for w in widths],
        out_shape=[jax.ShapeDtypeStruct((n, w), F32) for w in widths],
        compiler_params=_cparams("arbitrary"),
        name="in_proj",
    )(x, norm_w, w_packed)


def _stage_tile(ext_ref, init_ref, cur, halo, tm):
    t = pl.program_id(1)

    @pl.when(t == 0)
    def _():
        ext_ref[0:halo, :] = init_ref[0]

    @pl.when(t > 0)
    def _():
        ext_ref[0:halo, :] = ext_ref[tm:tm + halo, :]

    ext_ref[halo:halo + tm, :] = cur


def _rwkv_prep_kernel(cur_ref, init_ref, mu_ref, w0_ref, w2_ref, a0_ref, a2_ref, g2_ref, kk_ref, ka_ref,
                      ones_ref, r_ref, w_ref, k_ref, v_ref, an_ref, bb_ref, g_ref, ext_ref, *, halo, bs, tm):
    cur = cur_ref[...]
    _stage_tile(ext_ref, init_ref, cur, halo, tm)
    prev = ext_ref[halo - bs:halo - bs + tm, :]
    zc = cur + (prev - cur) * mu_ref[...]
    c0, c1, c2 = RW_WIDTH, 2 * RW_WIDTH, 3 * RW_WIDTH
    r = zc[:, 0:c0]
    k = zc[:, c0:c1]
    v = zc[:, c1:c2]
    xw = zc[:, c2:c2 + RW_DECAY_LORA]
    xa = zc[:, c2 + RW_DECAY_LORA:c2 + RW_DECAY_LORA + RW_AAA_LORA]
    xg = zc[:, c2 + RW_DECAY_LORA + RW_AAA_LORA:RW_COLS]
    w_raw = -_softplus(-(w0_ref[...] + _dot_hi(jnp.tanh(xw), w2_ref[...]))) - 0.5
    decay = jnp.exp(-jnp.exp(w_raw))
    a = _sigmoid(a0_ref[...] + _dot_hi(xa, a2_ref[...]))
    g = _dot_hi(_sigmoid(xg), g2_ref[...])
    kkr = k * kk_ref[...]
    kk = kkr * lax.rsqrt(_dot_hi(kkr * kkr, ones_ref[...]) + 1e-6)
    r_ref[...] = r
    w_ref[...] = decay
    k_ref[...] = k * (1.0 + (a - 1.0) * ka_ref[...])
    v_ref[...] = v
    an_ref[...] = -kk
    bb_ref[...] = kk * a
    g_ref[...] = g


def _rwkv_prep(rw_cur, init, p, nb, tm, bs):
    n = rw_cur.shape[0]
    nt = n // (nb * tm)
    halo = init.shape[1]
    row = lambda w: pl.BlockSpec((tm, w), lambda b, t: (b * nt + t, 0))
    consts = [p["mu_shift"], p["rw_w0"], p["rw_w2"], p["rw_a0"], p["rw_a2"], p["rw_g2"], p["rw_k_k"], p["rw_k_a"],
              p["ones_rw"]]
    return pl.pallas_call(
        functools.partial(_rwkv_prep_kernel, halo=halo, bs=bs, tm=tm),
        grid=(nb, nt),
        in_specs=[row(RW_COLS), pl.BlockSpec((1, halo, RW_COLS), lambda b, t: (b, 0, 0))]
        + [_const_spec(c.shape) for c in consts],
        out_specs=[row(RW_WIDTH)] * 7,
        out_shape=[jax.ShapeDtypeStruct((n, RW_WIDTH), F32)] * 7,
        scratch_shapes=[pltpu.VMEM((halo + tm, RW_COLS), F32)],
        compiler_params=_cparams("arbitrary", "arbitrary"),
        name="rwkv_prep",
    )(rw_cur, init, *consts)


def _sublane_allsum(x):
    x = x + pltpu.roll(x, 4, 0)
    x = x + pltpu.roll(x, 2, 0)
    return x + pltpu.roll(x, 1, 0)


def _rwkv_scan_kernel(j_ref, v_ref, s0_ref, y_ref, sout_ref, s_ref, *, tc):
    c = pl.program_id(1)
    nj = RW_HEAD_DIM // SUBLANES

    @pl.when(c == 0)
    def _():
        s_ref[...] = s0_ref[0]

    def step(t, carry):
        def vec(which, jb):
            return j_ref[0, t, pl.ds(which * RW_HEAD_DIM + jb * SUBLANES, SUBLANES), :]

        for i in range(SCAN_ROWS):
            s = [s_ref[i, jb * SUBLANES:(jb + 1) * SUBLANES, :] for jb in range(nj)]
            acc = s[0] * vec(1, 0)
            for jb in range(1, nj):
                acc = acc + s[jb] * vec(1, jb)
            sa = _sublane_allsum(acc)
            vb = jnp.broadcast_to(v_ref[0, t, i:i + 1, :], (SUBLANES, LANES))
            accy = None
            for jb in range(nj):
                sn = s[jb] * vec(0, jb) + sa * vec(2, jb) + vb * vec(3, jb)
                s_ref[i, jb * SUBLANES:(jb + 1) * SUBLANES, :] = sn
                yr = sn * vec(4, jb)
                accy = yr if accy is None else accy + yr
            y_ref[0, t, i:i + 1, :] = _sublane_allsum(accy)[0:1, :]
        return carry

    lax.fori_loop(0, tc, step, 0)

    @pl.when(c == pl.num_programs(1) - 1)
    def _():
        sout_ref[0] = s_ref[...]


def _rwkv_scan(jvec, vvec, s0, tc):
    g, t = jvec.shape[0], jvec.shape[1]
    return pl.pallas_call(
        functools.partial(_rwkv_scan_kernel, tc=tc),
        grid=(g, t // tc),
        in_specs=[pl.BlockSpec((1, tc, 5 * RW_HEAD_DIM, LANES), lambda gi, c: (gi, c, 0, 0)),
                  pl.BlockSpec((1, tc, SCAN_ROWS, LANES), lambda gi, c: (gi, c, 0, 0)),
                  pl.BlockSpec((1, SCAN_ROWS, RW_HEAD_DIM, LANES), lambda gi, c: (gi, 0, 0, 0))],
        out_specs=[pl.BlockSpec((1, tc, SCAN_ROWS, LANES), lambda gi, c: (gi, c, 0, 0)),
                   pl.BlockSpec((1, SCAN_ROWS, RW_HEAD_DIM, LANES), lambda gi, c: (gi, 0, 0, 0))],
        out_shape=[jax.ShapeDtypeStruct((g, t, SCAN_ROWS, LANES), F32),
                   jax.ShapeDtypeStruct((g, SCAN_ROWS, RW_HEAD_DIM, LANES), F32)],
        scratch_shapes=[pltpu.VMEM((SCAN_ROWS, RW_HEAD_DIM, LANES), F32)],
        compiler_params=_cparams("arbitrary", "arbitrary"),
        name="rwkv_scan",
    )(jvec, vvec, s0)


def _to_scan_lanes(x):
    b, t, _ = x.shape
    g = b // SCAN_SEQS
    x = x.reshape(g, SCAN_SEQS, t, RW_HEADS, RW_HEAD_DIM)
    x = jnp.transpose(x, (0, 2, 4, 1, 3)).reshape(g, t, RW_HEAD_DIM, SCAN_SEQS * RW_HEADS)
    return jnp.concatenate([x, x], axis=-1)


def _to_scan_rows(x):
    b, t, _ = x.shape
    g = b // SCAN_SEQS
    x = x.reshape(g, SCAN_SEQS, t, RW_HEADS, 2, SCAN_ROWS)
    return jnp.transpose(x, (0, 2, 5, 4, 1, 3)).reshape(g, t, SCAN_ROWS, LANES)


def _from_scan_rows(y, b):
    g, t = y.shape[0], y.shape[1]
    y = y.reshape(g, t, SCAN_ROWS, 2, SCAN_SEQS, RW_HEADS)
    return jnp.transpose(y, (0, 4, 1, 5, 3, 2)).reshape(b, t, RW_WIDTH)


def _state_to_scan(s):
    b = s.shape[0]
    g = b // SCAN_SEQS
    s = s.reshape(g, SCAN_SEQS, RW_HEADS, 2, SCAN_ROWS, RW_HEAD_DIM)
    return jnp.transpose(s, (0, 4, 5, 3, 1, 2)).reshape(g, SCAN_ROWS, RW_HEAD_DIM, LANES)


def _state_from_scan(s, b):
    g = s.shape[0]
    s = s.reshape(g, SCAN_ROWS, RW_HEAD_DIM, 2, SCAN_SEQS, RW_HEADS)
    return jnp.transpose(s, (0, 4, 5, 3, 1, 2)).reshape(b, RW_HEADS, RW_HEAD_DIM, RW_HEAD_DIM)


def _gdn_prep_kernel(x_ref, init_ref, ab_ref, cw_ref, alog_ref, dt_ref, ones_ref, q_ref, k_ref, v_ref, gb_ref,
                     ext_ref, *, halo, bs, tm):
    cur = x_ref[...]
    _stage_tile(ext_ref, init_ref, cur, halo, tm)
    conv = cur * cw_ref[GDN_CONV - 1:GDN_CONV, :]
    for s in range(1, GDN_CONV):
        off = halo - s * bs
        conv = conv + ext_ref[off:off + tm, :] * cw_ref[GDN_CONV - 1 - s:GDN_CONV - s, :]
    qkv = conv * _sigmoid(conv)
    q = qkv[:, 0:GDN_WIDTH]
    k = qkv[:, GDN_WIDTH:2 * GDN_WIDTH]
    ones = ones_ref[...]
    q_ref[...] = q * lax.rsqrt(_dot_hi(q * q, ones) + 1e-6) * (GDN_HEAD_DIM ** -0.5)
    k_ref[...] = k * lax.rsqrt(_dot_hi(k * k, ones) + 1e-6)
    v_ref[...] = qkv[:, 2 * GDN_WIDTH:3 * GDN_WIDTH]
    ab = ab_ref[...]
    g = -jnp.exp(alog_ref[...]) * _softplus(ab + dt_ref[...])
    lane = lax.broadcasted_iota(jnp.int32, ab.shape, 1)
    gb_ref[...] = jnp.where(lane < GDN_HEADS, g, _sigmoid(ab))


def _gdn_prep(qkv_raw, init, ab, p, nb, tm, bs):
    n = qkv_raw.shape[0]
    nt = n // (nb * tm)
    halo = init.shape[1]
    row = lambda w: pl.BlockSpec((tm, w), lambda b, t: (b * nt + t, 0))
    consts = [p["gdn_conv_w"], p["gdn_alog"], p["gdn_dt"], p["ones_gdn"]]
    return pl.pallas_call(
        functools.partial(_gdn_prep_kernel, halo=halo, bs=bs, tm=tm),
        grid=(nb, nt),
        in_specs=[row(GDN_CONV_CH), pl.BlockSpec((1, halo, GDN_CONV_CH), lambda b, t: (b, 0, 0)), row(LANES)]
        + [_const_spec(c.shape) for c in consts],
        out_specs=[row(GDN_WIDTH)] * 3 + [row(LANES)],
        out_shape=[jax.ShapeDtypeStruct((n, GDN_WIDTH), F32)] * 3 + [jax.ShapeDtypeStruct((n, LANES), F32)],
        scratch_shapes=[pltpu.VMEM((halo + tm, GDN_CONV_CH), F32)],
        compiler_params=_cparams("arbitrary", "arbitrary"),
        name="gdn_prep",
    )(qkv_raw, init, ab, *consts)


def _unit_lower_inverse(a, eye, chunk):
    n = -a
    inv = eye + n
    for _ in range(int(math.log2(chunk)) - 1):
        n = _dot_hi(n, n)
        inv = inv + _dot_hi(inv, n)
    return inv


def _gdn_chunk_kernel(q_ref, k_ref, v_ref, gb_ref, s0_ref, o_ref, sout_ref, s_ref, *, chunk, n_chunks):
    blk = pl.program_id(1)

    @pl.when(blk == 0)
    def _():
        s_ref[...] = s0_ref[0]

    rows = lax.broadcasted_iota(jnp.int32, (chunk, chunk), 0)
    cols = lax.broadcasted_iota(jnp.int32, (chunk, chunk), 1)
    causal = rows >= cols
    strict = rows > cols
    eye = (rows == cols).astype(F32)
    tri = causal.astype(F32)

    def one_chunk(ci, carry):
        r0 = pl.multiple_of(ci * chunk, chunk)
        gb = gb_ref[pl.ds(r0, chunk), :]
        gcs = _dot_hi(tri, gb)
        for h in range(GDN_HEADS):
            hs = slice(h * GDN_HEAD_DIM, (h + 1) * GDN_HEAD_DIM)
            q = q_ref[pl.ds(r0, chunk), hs]
            k = k_ref[pl.ds(r0, chunk), hs]
            v = v_ref[pl.ds(r0, chunk), hs]
            gc = gcs[:, h:h + 1]
            beta = gb[:, GDN_HEADS + h:GDN_HEADS + h + 1]
            gc_row = jnp.sum(eye * gc, axis=0, keepdims=True)
            decay = jnp.exp(jnp.where(causal, gc - gc_row, -jnp.inf))
            kb = k * beta
            a = jnp.where(strict, _dot_nt(kb, k) * decay, 0.0)
            inv = _unit_lower_inverse(a, eye, chunk)
            egc = jnp.exp(gc)
            rhs = jnp.concatenate([v * beta, kb * egc], axis=1)
            sol = _dot_hi(inv, rhs)
            u = sol[:, 0:GDN_HEAD_DIM]
            w = sol[:, GDN_HEAD_DIM:2 * GDN_HEAD_DIM]
            s = s_ref[h]
            v_new = u - _dot(w, s)
            qk = _dot_nt(q, k) * decay
            o_ref[pl.ds(r0, chunk), hs] = _dot(q * egc, s) + _dot(qk, v_new)
            g_last = gc[chunk - 1:chunk, :]
            kd = k * jnp.exp(g_last - gc)
            s_ref[h] = s * jnp.exp(g_last) + _dot(kd.T, v_new)
        return carry

    lax.fori_loop(0, n_chunks, one_chunk, 0)

    @pl.when(blk == pl.num_programs(1) - 1)
    def _():
        sout_ref[0] = s_ref[...]


def _gdn_chunks(q, k, v, gb, s0, t_len, chunk, tb):
    b = s0.shape[0]
    nblk = t_len // tb
    row = lambda w: pl.BlockSpec((tb, w), lambda bi, i: (bi * nblk + i, 0))
    st = pl.BlockSpec((1, GDN_HEADS, GDN_HEAD_DIM, GDN_HEAD_DIM), lambda bi, i: (bi, 0, 0, 0))
    return pl.pallas_call(
        functools.partial(_gdn_chunk_kernel, chunk=chunk, n_chunks=tb // chunk),
        grid=(b, nblk),
        in_specs=[row(GDN_WIDTH)] * 3 + [row(LANES), st],
        out_specs=[row(GDN_WIDTH), st],
        out_shape=[jax.ShapeDtypeStruct(q.shape, F32), jax.ShapeDtypeStruct(s0.shape, F32)],
        scratch_shapes=[pltpu.VMEM((GDN_HEADS, GDN_HEAD_DIM, GDN_HEAD_DIM), F32)],
        compiler_params=_cparams("arbitrary", "arbitrary"),
        name="gdn_chunks",
    )(q, k, v, gb, s0)


def _route(logits):
    lane = lax.broadcasted_iota(jnp.int32, logits.shape, 1)
    neg = -jnp.inf
    is_group = (lane >= N_EXPERTS) & (lane < N_EXPERTS + N_GROUPS)
    gl = jnp.where(is_group, logits, neg)
    gmax = jnp.max(gl, axis=-1, keepdims=True)
    gp = 1.0 / jnp.sum(jnp.exp(gl - gmax), axis=-1, keepdims=True)
    gidx = jnp.min(jnp.where(gl == gmax, lane, LANES), axis=-1, keepdims=True) - N_EXPERTS
    lo = gidx * EXPERTS_PER_GROUP
    el = jnp.where((lane >= lo) & (lane < lo + EXPERTS_PER_GROUP), logits, neg)
    m1 = jnp.max(el, axis=-1, keepdims=True)
    i1 = jnp.min(jnp.where(el == m1, lane, LANES), axis=-1, keepdims=True)
    el2 = jnp.where(lane == i1, neg, el)
    m2 = jnp.max(el2, axis=-1, keepdims=True)
    i2 = jnp.min(jnp.where(el2 == m2, lane, LANES), axis=-1, keepdims=True)
    e2 = jnp.exp(m2 - m1)
    den = 1.0 + e2
    return jnp.where(lane == i1, gp / den, 0.0) + jnp.where(lane == i2, gp * e2 / den, 0.0)


def _merge_kernel(y_ref, r_ref, k_ref, v_ref, g_ref, o_ref, z_ref, gate_ref, h_ref, lnw_ref, lnb_ref, rk_ref,
                  ones_ref, gnw_ref, woa_ref, wob_ref, wo_ref, n2_ref, rw_ref, rb_ref, h1_ref, xt_ref, mg_ref):
    ones = ones_ref[...]
    inv_n = 1.0 / RW_HEAD_DIM
    y = y_ref[...]
    v = v_ref[...]
    mu = _dot_hi(y, ones) * inv_n
    yc = y - mu
    var = _dot_hi(yc * yc, ones) * inv_n
    ya = yc * lax.rsqrt(var + RW_GN_EPS) * lnw_ref[...] + lnb_ref[...]
    bonus = _dot_hi(r_ref[...] * k_ref[...] * rk_ref[...], ones) * v
    ya = (ya + bonus) * g_ref[...]

    z = z_ref[...]
    gnw = gnw_ref[...]
    yb_parts = []
    for h in range(GDN_HEADS):
        hs = slice(h * GDN_HEAD_DIM, (h + 1) * GDN_HEAD_DIM)
        yb_parts.append(_rmsnorm(o_ref[:, hs], gnw))
    yb = jnp.concatenate(yb_parts, axis=1) * (z * _sigmoid(z))

    gates = gate_ref[...]
    merged = _sigmoid(gates[:, 0:D_MODEL]) * _dot(ya, woa_ref[...]) \
        + _sigmoid(gates[:, D_MODEL:2 * D_MODEL]) * _dot(yb, wob_ref[...])
    h1 = h_ref[...] + _dot(merged, wo_ref[...])
    h1_ref[...] = h1
    xt = _rmsnorm(h1, n2_ref[...])
    xt_ref[...] = xt
    mg_ref[...] = _route(_dot_hi(xt, rw_ref[...]) + rb_ref[...])


def _merge(y, r, k, v, g, o, z, gates, h, p, tm):
    n = y.shape[0]
    row = lambda w: pl.BlockSpec((tm, w), lambda i: (i, 0))
    consts = [p["rw_lnx_w"], p["rw_lnx_b"], p["rw_r_k"], p["ones_rw"], p["gdn_norm_w"], p["w_oA"], p["w_oB"],
              p["w_o"], p["norm2_w"], p["router_w"], p["router_b"]]
    return pl.pallas_call(
        _merge_kernel,
        grid=(n // tm,),
        in_specs=[row(RW_WIDTH)] * 7 + [row(2 * D_MODEL), row(D_MODEL)] + [_const_spec(c.shape) for c in consts],
        out_specs=[row(D_MODEL), row(D_MODEL), row(LANES)],
        out_shape=[jax.ShapeDtypeStruct((n, D_MODEL), F32), jax.ShapeDtypeStruct((n, D_MODEL), F32),
                   jax.ShapeDtypeStruct((n, LANES), F32)],
        compiler_params=_cparams("arbitrary"),
        name="merge",
    )(y, r, k, v, g, o, z, gates, h, *consts)


def _moe_kernel(xt_ref, h1_ref, mg_ref, wg_ref, wu_ref, wd_ref, nf_ref, o_ref):
    e = pl.program_id(1)

    @pl.when(e == 0)
    def _():
        o_ref[...] = jnp.zeros_like(o_ref)

    xb = xt_ref[...].astype(BF16)
    hg = jnp.dot(xb, wg_ref[0].astype(BF16), preferred_element_type=F32)
    hu = jnp.dot(xb, wu_ref[0].astype(BF16), preferred_element_type=F32)
    hid = hg * _sigmoid(hg) * hu
    d = jnp.dot(hid.astype(BF16), wd_ref[0].astype(BF16), preferred_element_type=F32)
    mg = mg_ref[...]
    lane = lax.broadcasted_iota(jnp.int32, mg.shape, 1)
    ge = jnp.sum(jnp.where(lane == e, mg, 0.0), axis=-1, keepdims=True)
    o_ref[...] += ge * d

    @pl.when(e == N_EXPERTS - 1)
    def _():
        o_ref[...] = _rmsnorm(h1_ref[...] + o_ref[...], nf_ref[...])


def _moe(xt, h1, mg, wg, wu, wd, nf, tm):
    n = xt.shape[0]
    row = lambda w: pl.BlockSpec((tm, w), lambda i, e: (i, 0))
    return pl.pallas_call(
        _moe_kernel,
        grid=(n // tm, N_EXPERTS),
        in_specs=[row(D_MODEL), row(D_MODEL), row(LANES),
                  pl.BlockSpec((1, D_MODEL, EXPERT_FF), lambda i, e: (e, 0, 0)),
                  pl.BlockSpec((1, D_MODEL, EXPERT_FF), lambda i, e: (e, 0, 0)),
                  pl.BlockSpec((1, EXPERT_FF, D_MODEL), lambda i, e: (e, 0, 0)),
                  _const_spec((1, D_MODEL))],
        out_specs=row(D_MODEL),
        out_shape=jax.ShapeDtypeStruct((n, D_MODEL), F32),
        compiler_params=_cparams("arbitrary", "arbitrary"),
        name="moe",
    )(xt, h1, mg, wg, wu, wd, nf)


def _pad_rows(x, rows):
    return jnp.pad(x, ((0, rows - x.shape[0]),) + ((0, 0),) * (x.ndim - 1))


def _mixer(x_rows, p, *, batch, t_len, time_major, tm, rw_prev, conv_prev, wkv0, gdn0, chunk, with_merge):
    n = batch * t_len
    xn, rw_cur, qkv_raw, z, gates, ab = _in_proj(x_rows, p["norm1_w"], p["w_in"], min(tm, n))

    if time_major:
        nb, bs, tmt = 1, batch, n
        rw_init = rw_prev[None]
        conv_init = jnp.transpose(conv_prev, (1, 0, 2)).reshape(1, (GDN_CONV - 1) * batch, GDN_CONV_CH)
    else:
        nb, bs, tmt = batch, 1, min(tm, t_len)
        rw_init = jnp.pad(rw_prev[:, None, :], ((0, 0), (SUBLANES - 1, 0), (0, 0)))
        conv_init = jnp.pad(conv_prev, ((0, 0), (SUBLANES - (GDN_CONV - 1), 0), (0, 0)))

    r, dec, k2, v, an, bb, g = _rwkv_prep(rw_cur, rw_init, p, nb, tmt, bs)
    q, kg, vg, gb = _gdn_prep(qkv_raw, conv_init, ab, p, nb, tmt, bs)

    def seq_major(a):
        if time_major:
            return jnp.transpose(a.reshape(t_len, batch, a.shape[-1]), (1, 0, 2))
        return a.reshape(batch, t_len, a.shape[-1])

    def rows_like_x(a):
        if time_major:
            a = jnp.transpose(a, (1, 0, 2))
        return a.reshape(n, a.shape[-1])

    bpad = -(-batch // SCAN_SEQS) * SCAN_SEQS
    padb = lambda a: _pad_rows(a, bpad)
    jvec = jnp.concatenate([_to_scan_lanes(padb(seq_major(a))) for a in (dec, an, bb, k2, r)], axis=2)
    vvec = _to_scan_rows(padb(seq_major(v)))
    tc = min(t_len, 32)
    y_scan, s_scan = _rwkv_scan(jvec, vvec, _state_to_scan(padb(wkv0)), tc)
    y_rw = rows_like_x(_from_scan_rows(y_scan, bpad)[:batch])
    wkv = _state_from_scan(s_scan, bpad)[:batch]

    t_pad = -(-t_len // chunk) * chunk
    def gdn_rows(a):
        a = seq_major(a)
        if t_pad != t_len:
            a = jnp.pad(a, ((0, 0), (0, t_pad - t_len), (0, 0)))
        return a.reshape(batch * t_pad, a.shape[-1])
    tb = min(t_pad, 8 * chunk)
    o_pad, gdn = _gdn_chunks(gdn_rows(q), gdn_rows(kg), gdn_rows(vg), gdn_rows(gb), gdn0, t_pad, chunk, tb)
    o = rows_like_x(o_pad.reshape(batch, t_pad, GDN_WIDTH)[:, :t_len])

    out = dict(xn=xn, rw_cur=rw_cur, qkv_raw=qkv_raw, wkv=wkv, gdn=gdn)
    if with_merge:
        out["h1"], out["xt"], out["mg"] = _merge(y_rw, r, k2, v, g, o, z, gates, x_rows, p, min(tm, n))
    return out


def kernel(x_prompt, x_sample, state_shift, state_wkv, state_conv, state_gdn, meta_tokens, norm1_w, w_in, mu_shift, rw_w0, rw_w2, rw_a0, rw_a2, rw_g2, rw_k_k, rw_k_a, rw_r_k, rw_lnx_w, rw_lnx_b, gdn_conv_w, gdn_A_log, gdn_dt_bias, gdn_norm_w, w_oA, w_oB, w_o, norm2_w, router_g, router_g_b, router_e, router_e_b, moe_w_gate, moe_w_up, moe_w_down, norm_f_w):
    bp, sp, _ = x_prompt.shape
    bsm, ss, _ = x_sample.shape
    row = lambda a: a[0].reshape(1, -1)
    w0 = w_in[0]
    gcol = RW_COLS + 4 * GDN_WIDTH + 2 * GDN_HEADS
    w_packed = jnp.concatenate(
        [w0[:, :RW_COLS + 4 * GDN_WIDTH], w0[:, gcol:], w0[:, RW_COLS + 4 * GDN_WIDTH:gcol],
         jnp.zeros((D_MODEL, LANES - 2 * GDN_HEADS), F32)], axis=1).astype(BF16)
    lane_pad = lambda a: jnp.pad(a, ((0, 0), (0, LANES - a.shape[1])))
    p = dict(
        norm1_w=row(norm1_w), w_in=w_packed, mu_shift=row(mu_shift), rw_w0=row(rw_w0), rw_w2=rw_w2[0],
        rw_a0=row(rw_a0), rw_a2=rw_a2[0], rw_g2=rw_g2[0], rw_k_k=row(rw_k_k), rw_k_a=row(rw_k_a),
        rw_r_k=row(rw_r_k), rw_lnx_w=row(rw_lnx_w), rw_lnx_b=row(rw_lnx_b),
        ones_rw=_block_ones(RW_WIDTH, RW_HEAD_DIM), ones_gdn=_block_ones(GDN_WIDTH, GDN_HEAD_DIM),
        gdn_conv_w=gdn_conv_w[0], gdn_alog=lane_pad(row(gdn_A_log)), gdn_dt=lane_pad(row(gdn_dt_bias)),
        gdn_norm_w=row(gdn_norm_w), w_oA=w_oA[0].astype(BF16), w_oB=w_oB[0].astype(BF16),
        w_o=w_o[0].astype(BF16), norm2_w=row(norm2_w),
        router_w=lane_pad(jnp.concatenate([router_e[0], router_g[0]], axis=1)),
        router_b=lane_pad(jnp.concatenate([row(router_e_b), row(router_g_b)], axis=1)),
    )
    nf = norm_f_w.reshape(1, -1)
    dt = x_prompt.dtype

    meta = _mixer(meta_tokens.astype(dt), p, batch=1, t_len=N_META, time_major=False, tm=N_META,
                  rw_prev=jnp.zeros((1, RW_COLS), dt), conv_prev=jnp.zeros((1, GDN_CONV - 1, GDN_CONV_CH), dt),
                  wkv0=jnp.zeros((1, RW_HEADS, RW_HEAD_DIM, RW_HEAD_DIM), dt),
                  gdn0=jnp.zeros((1, GDN_HEADS, GDN_HEAD_DIM, GDN_HEAD_DIM), dt), chunk=N_META, with_merge=False)
    rep = lambda a: jnp.broadcast_to(a, (bp,) + a.shape[1:])

    prompt = _mixer(x_prompt.reshape(bp * sp, D_MODEL), p, batch=bp, t_len=sp, time_major=False, tm=256,
                    rw_prev=rep(meta["rw_cur"][N_META - 1:]), conv_prev=rep(meta["qkv_raw"][None, N_META - 3:]),
                    wkv0=rep(meta["wkv"]), gdn0=rep(meta["gdn"]), chunk=GDN_CHUNK, with_merge=True)

    xs_rows = jnp.transpose(x_sample, (1, 0, 2)).reshape(ss * bsm, D_MODEL)
    prev_s = _in_proj(state_shift[0].astype(dt), p["norm1_w"], w_packed, bsm, norm=False)[1]
    sample = _mixer(xs_rows, p, batch=bsm, t_len=ss, time_major=True, tm=256,
                    rw_prev=prev_s, conv_prev=state_conv[0].astype(dt), wkv0=state_wkv[0], gdn0=state_gdn[0],
                    chunk=SUBLANES, with_merge=True)

    y_p = _moe(prompt["xt"], prompt["h1"], prompt["mg"], moe_w_gate[0], moe_w_up[0], moe_w_down[0], nf, 1024)
    y_s = _moe(sample["xt"], sample["h1"], sample["mg"], moe_w_gate[0], moe_w_up[0], moe_w_down[0], nf, 512)

    y_prompt = y_p.reshape(bp, sp, D_MODEL)
    y_sample = jnp.transpose(y_s.reshape(ss, bsm, D_MODEL), (1, 0, 2))
    shift_p = prompt["xn"].reshape(bp, sp, D_MODEL)[:, -1][None]
    conv_p = prompt["qkv_raw"].reshape(bp, sp, GDN_CONV_CH)[:, sp - (GDN_CONV - 1):][None]
    shift_s = sample["xn"].reshape(ss, bsm, D_MODEL)[-1][None].astype(state_shift.dtype)
    xpad_s = jnp.concatenate([state_conv[0].astype(dt),
                              jnp.transpose(sample["qkv_raw"].reshape(ss, bsm, GDN_CONV_CH), (1, 0, 2))], axis=1)
    conv_s = xpad_s[:, ss:][None].astype(state_conv.dtype)
    return (y_prompt, y_sample, shift_p, prompt["wkv"][None], conv_p, prompt["gdn"][None],
            shift_s, sample["wkv"][None].astype(state_wkv.dtype), conv_s, sample["gdn"][None].astype(state_gdn.dtype))
```

```python
import functools
import math

import jax
import jax.numpy as jnp
from jax import lax
from jax.experimental import pallas as pl
from jax.experimental.pallas import tpu as pltpu

F32 = jnp.float32
BF16 = jnp.bfloat16
HIGHEST = lax.Precision.HIGHEST

D_MODEL = 1024
N_META = 16
RW_WIDTH = 512
RW_HEAD_DIM = 64
RW_HEADS = 8
RW_DECAY_LORA = 64
RW_AAA_LORA = 64
RW_GATE_LORA = 128
RW_COLS = 3 * RW_WIDTH + RW_DECAY_LORA + RW_AAA_LORA + RW_GATE_LORA
RW_GN_EPS = RW_HEAD_DIM * 1e-5
GDN_WIDTH = 512
GDN_HEAD_DIM = 128
GDN_HEADS = 4
GDN_CONV = 4
GDN_CONV_CH = 3 * GDN_WIDTH
GDN_CHUNK = 64
N_GROUPS = 4
EXPERTS_PER_GROUP = 8
N_EXPERTS = 32
EXPERT_FF = 512
RMS_EPS = 1e-6

LANES = 128
SUBLANES = 8
VMEM_LIMIT_BYTES = 56 * 1024 * 1024

COL_RW = 0
COL_QKV = COL_RW + RW_COLS
COL_Z = COL_QKV + GDN_CONV_CH
COL_GATE = COL_Z + GDN_WIDTH
COL_AB = COL_GATE + 2 * D_MODEL
IN_COLS_PACKED = COL_AB + LANES

SCAN_SEQS = 8
SCAN_ROWS = RW_HEAD_DIM // 2
SCAN_ROW_GROUP = 2

GDN_SEQ_BLOCK = 2
GDN_SEQ_GROUP = 2


def _cparams(*sem):
    return pltpu.CompilerParams(dimension_semantics=sem, vmem_limit_bytes=VMEM_LIMIT_BYTES)


def _const_spec(shape):
    zeros = (0,) * len(shape)
    return pl.BlockSpec(shape, lambda *_: zeros)


def _dot(a, b):
    return jnp.dot(a.astype(BF16), b.astype(BF16), preferred_element_type=F32)


def _dot_hi(a, b):
    return jnp.dot(a, b, preferred_element_type=F32, precision=HIGHEST)


def _split_bf16(a):
    hi = a.astype(BF16)
    return hi, (a - hi.astype(F32)).astype(BF16)


def _dot3(a, b):
    d = lambda x, y: jnp.dot(x, y, preferred_element_type=F32)
    return d(a[0], b[0]) + d(a[0], b[1]) + d(a[1], b[0])


def _dot_f32(a, b):
    return _dot3(_split_bf16(a), _split_bf16(b))


def _seg_sum(x, ones):
    hi, lo = _split_bf16(x)
    return jnp.dot(hi, ones, preferred_element_type=F32) + jnp.dot(lo, ones, preferred_element_type=F32)


def _dot_nt(a, b):
    dims = (((1,), (1,)), ((), ()))
    return lax.dot_general(a.astype(BF16), b.astype(BF16), dims, preferred_element_type=F32)


def _sigmoid(x):
    return 1.0 / (1.0 + jnp.exp(-x))


def _softplus(x):
    return jnp.maximum(x, 0.0) + jnp.log1p(jnp.exp(-jnp.abs(x)))


def _rmsnorm(x, w):
    ms = jnp.mean(x * x, axis=-1, keepdims=True)
    return x * lax.rsqrt(ms + RMS_EPS) * w


def _block_ones(width, seg):
    r = jnp.arange(width) // seg
    return (r[:, None] == r[None, :]).astype(BF16)


def _in_proj_kernel(x_ref, nw_ref, w_ref, xn_ref, rw_ref, qkv_ref, z_ref, gate_ref, ab_ref, *, norm):
    x = x_ref[...]
    xn = _rmsnorm(x, nw_ref[...]) if norm else x
    xn_ref[...] = xn
    xb = xn.astype(BF16)
    rw_ref[...] = jnp.dot(xb, w_ref[:, COL_RW:COL_QKV], preferred_element_type=F32)
    qkv_ref[...] = jnp.dot(xb, w_ref[:, COL_QKV:COL_Z], preferred_element_type=F32)
    z_ref[...] = jnp.dot(xb, w_ref[:, COL_Z:COL_GATE], preferred_element_type=F32)
    gate_ref[...] = jnp.dot(xb, w_ref[:, COL_GATE:COL_AB], preferred_element_type=F32)
    ab_ref[...] = jnp.dot(xb, w_ref[:, COL_AB:IN_COLS_PACKED], preferred_element_type=F32)


def _in_proj(x, norm_w, w_packed, tm, norm=True):
    n = x.shape[0]
    widths = (D_MODEL, RW_COLS, GDN_CONV_CH, GDN_WIDTH, 2 * D_MODEL, LANES)
    row = lambda w: pl.BlockSpec((tm, w), lambda i: (i, 0))
    return pl.pallas_call(
        functools.partial(_in_proj_kernel, norm=norm),
        grid=(n // tm,),
        in_specs=[row(D_MODEL), _const_spec((1, D_MODEL)), _const_spec(w_packed.shape)],
        out_specs=[row(w) for w in widths],
        out_shape=[jax.ShapeDtypeStruct((n, w), F32) for w in widths],
        compiler_params=_cparams("arbitrary"),
        name="in_proj",
    )(x, norm_w, w_packed)


def _stage_tile(ext_ref, init_ref, cur, halo, tm):
    t = pl.program_id(1)

    @pl.when(t == 0)
    def _():
        ext_ref[0:halo, :] = init_ref[0]

    @pl.when(t > 0)
    def _():
        ext_ref[0:halo, :] = ext_ref[tm:tm + halo, :]

    ext_ref[halo:halo + tm, :] = cur


def _rwkv_prep_kernel(cur_ref, init_ref, mu_ref, w0_ref, w2_ref, a0_ref, a2_ref, g2_ref, kk_ref, ka_ref,
                      ones_ref, r_ref, w_ref, k_ref, v_ref, an_ref, bb_ref, g_ref, ext_ref, *, halo, bs, tm):
    cur = cur_ref[...]
    _stage_tile(ext_ref, init_ref, cur, halo, tm)
    prev = ext_ref[halo - bs:halo - bs + tm, :]
    zc = cur + (prev - cur) * mu_ref[...]
    c0, c1, c2 = RW_WIDTH, 2 * RW_WIDTH, 3 * RW_WIDTH
    r = zc[:, 0:c0]
    k = zc[:, c0:c1]
    v = zc[:, c1:c2]
    xw = zc[:, c2:c2 + RW_DECAY_LORA]
    xa = zc[:, c2 + RW_DECAY_LORA:c2 + RW_DECAY_LORA + RW_AAA_LORA]
    xg = zc[:, c2 + RW_DECAY_LORA + RW_AAA_LORA:RW_COLS]
    w_raw = -_softplus(-(w0_ref[...] + _dot_f32(jnp.tanh(xw), w2_ref[...]))) - 0.5
    decay = jnp.exp(-jnp.exp(w_raw))
    a = _sigmoid(a0_ref[...] + _dot_f32(xa, a2_ref[...]))
    g = _dot_f32(_sigmoid(xg), g2_ref[...])
    kkr = k * kk_ref[...]
    kk = kkr * lax.rsqrt(_seg_sum(kkr * kkr, ones_ref[...]) + 1e-6)
    r_ref[...] = r
    w_ref[...] = decay
    k_ref[...] = k * (1.0 + (a - 1.0) * ka_ref[...])
    v_ref[...] = v
    an_ref[...] = -kk
    bb_ref[...] = kk * a
    g_ref[...] = g


def _rwkv_prep(rw_cur, init, p, nb, tm, bs):
    n = rw_cur.shape[0]
    nt = n // (nb * tm)
    halo = init.shape[1]
    row = lambda w: pl.BlockSpec((tm, w), lambda b, t: (b * nt + t, 0))
    consts = [p["mu_shift"], p["rw_w0"], p["rw_w2"], p["rw_a0"], p["rw_a2"], p["rw_g2"], p["rw_k_k"], p["rw_k_a"],
              p["ones_rw"]]
    return pl.pallas_call(
        functools.partial(_rwkv_prep_kernel, halo=halo, bs=bs, tm=tm),
        grid=(nb, nt),
        in_specs=[row(RW_COLS), pl.BlockSpec((1, halo, RW_COLS), lambda b, t: (b, 0, 0))]
        + [_const_spec(c.shape) for c in consts],
        out_specs=[row(RW_WIDTH)] * 7,
        out_shape=[jax.ShapeDtypeStruct((n, RW_WIDTH), F32)] * 7,
        scratch_shapes=[pltpu.VMEM((halo + tm, RW_COLS), F32)],
        compiler_params=_cparams("arbitrary", "arbitrary"),
        name="rwkv_prep",
    )(rw_cur, init, *consts)


def _sublane_allsum(x):
    x = x + pltpu.roll(x, 4, 0)
    x = x + pltpu.roll(x, 2, 0)
    return x + pltpu.roll(x, 1, 0)


def _rwkv_scan_kernel(j_ref, v_ref, s0_ref, y_ref, sout_ref, s_ref, *, tc):
    c = pl.program_id(1)
    nj = RW_HEAD_DIM // SUBLANES

    @pl.when(c == 0)
    def _():
        s_ref[...] = s0_ref[0]

    def step(t, carry):
        def vec(which, jb):
            return j_ref[0, t, pl.ds(which * RW_HEAD_DIM + jb * SUBLANES, SUBLANES), :]

        for i0 in range(0, SCAN_ROWS, SCAN_ROW_GROUP):
            rows = range(i0, i0 + SCAN_ROW_GROUP)
            s = {i: [s_ref[i, jb * SUBLANES:(jb + 1) * SUBLANES, :] for jb in range(nj)] for i in rows}
            acc = {}
            for jb in range(nj):
                a = vec(1, jb)
                for i in rows:
                    acc[i] = s[i][jb] * a if jb == 0 else acc[i] + s[i][jb] * a
            sa = {i: _sublane_allsum(acc[i]) for i in rows}
            vb = {i: jnp.broadcast_to(v_ref[0, t, i:i + 1, :], (SUBLANES, LANES)) for i in rows}
            accy = {}
            for jb in range(nj):
                w, b, k, r = vec(0, jb), vec(2, jb), vec(3, jb), vec(4, jb)
                for i in rows:
                    sn = s[i][jb] * w + sa[i] * b + vb[i] * k
                    s_ref[i, jb * SUBLANES:(jb + 1) * SUBLANES, :] = sn
                    accy[i] = sn * r if jb == 0 else accy[i] + sn * r
            for i in rows:
                y_ref[0, t, i:i + 1, :] = _sublane_allsum(accy[i])[0:1, :]
        return carry

    lax.fori_loop(0, tc, step, 0)

    @pl.when(c == pl.num_programs(1) - 1)
    def _():
        sout_ref[0] = s_ref[...]


def _rwkv_scan(jvec, vvec, s0, tc):
    g, t = jvec.shape[0], jvec.shape[1]
    return pl.pallas_call(
        functools.partial(_rwkv_scan_kernel, tc=tc),
        grid=(g, t // tc),
        in_specs=[pl.BlockSpec((1, tc, 5 * RW_HEAD_DIM, LANES), lambda gi, c: (gi, c, 0, 0)),
                  pl.BlockSpec((1, tc, SCAN_ROWS, LANES), lambda gi, c: (gi, c, 0, 0)),
                  pl.BlockSpec((1, SCAN_ROWS, RW_HEAD_DIM, LANES), lambda gi, c: (gi, 0, 0, 0))],
        out_specs=[pl.BlockSpec((1, tc, SCAN_ROWS, LANES), lambda gi, c: (gi, c, 0, 0)),
                   pl.BlockSpec((1, SCAN_ROWS, RW_HEAD_DIM, LANES), lambda gi, c: (gi, 0, 0, 0))],
        out_shape=[jax.ShapeDtypeStruct((g, t, SCAN_ROWS, LANES), F32),
                   jax.ShapeDtypeStruct((g, SCAN_ROWS, RW_HEAD_DIM, LANES), F32)],
        scratch_shapes=[pltpu.VMEM((SCAN_ROWS, RW_HEAD_DIM, LANES), F32)],
        compiler_params=_cparams("arbitrary", "arbitrary"),
        name="rwkv_scan",
    )(jvec, vvec, s0)


def _to_scan_lanes(x):
    b, t, _ = x.shape
    g = b // SCAN_SEQS
    x = x.reshape(g, SCAN_SEQS, t, RW_HEADS, RW_HEAD_DIM)
    x = jnp.transpose(x, (0, 2, 4, 1, 3)).reshape(g, t, RW_HEAD_DIM, SCAN_SEQS * RW_HEADS)
    return jnp.concatenate([x, x], axis=-1)


def _to_scan_rows(x):
    b, t, _ = x.shape
    g = b // SCAN_SEQS
    x = x.reshape(g, SCAN_SEQS, t, RW_HEADS, 2, SCAN_ROWS)
    return jnp.transpose(x, (0, 2, 5, 4, 1, 3)).reshape(g, t, SCAN_ROWS, LANES)


def _from_scan_rows(y, b):
    g, t = y.shape[0], y.shape[1]
    y = y.reshape(g, t, SCAN_ROWS, 2, SCAN_SEQS, RW_HEADS)
    return jnp.transpose(y, (0, 4, 1, 5, 3, 2)).reshape(b, t, RW_WIDTH)


def _state_to_scan(s):
    b = s.shape[0]
    g = b // SCAN_SEQS
    s = s.reshape(g, SCAN_SEQS, RW_HEADS, 2, SCAN_ROWS, RW_HEAD_DIM)
    return jnp.transpose(s, (0, 4, 5, 3, 1, 2)).reshape(g, SCAN_ROWS, RW_HEAD_DIM, LANES)


def _state_from_scan(s, b):
    g = s.shape[0]
    s = s.reshape(g, SCAN_ROWS, RW_HEAD_DIM, 2, SCAN_SEQS, RW_HEADS)
    return jnp.transpose(s, (0, 4, 5, 3, 1, 2)).reshape(b, RW_HEADS, RW_HEAD_DIM, RW_HEAD_DIM)


def _gdn_prep_kernel(x_ref, init_ref, ab_ref, cw_ref, alog_ref, dt_ref, ones_ref, q_ref, k_ref, v_ref, gb_ref,
                     ext_ref, *, halo, bs, tm):
    cur = x_ref[...]
    _stage_tile(ext_ref, init_ref, cur, halo, tm)
    conv = cur * cw_ref[GDN_CONV - 1:GDN_CONV, :]
    for s in range(1, GDN_CONV):
        off = halo - s * bs
        conv = conv + ext_ref[off:off + tm, :] * cw_ref[GDN_CONV - 1 - s:GDN_CONV - s, :]
    qkv = conv * _sigmoid(conv)
    q = qkv[:, 0:GDN_WIDTH]
    k = qkv[:, GDN_WIDTH:2 * GDN_WIDTH]
    ones = ones_ref[...]
    q_ref[...] = q * lax.rsqrt(_seg_sum(q * q, ones) + 1e-6) * (GDN_HEAD_DIM ** -0.5)
    k_ref[...] = k * lax.rsqrt(_seg_sum(k * k, ones) + 1e-6)
    v_ref[...] = qkv[:, 2 * GDN_WIDTH:3 * GDN_WIDTH]
    ab = ab_ref[...]
    g = -jnp.exp(alog_ref[...]) * _softplus(ab + dt_ref[...])
    lane = lax.broadcasted_iota(jnp.int32, ab.shape, 1)
    gb_ref[...] = jnp.where(lane < GDN_HEADS, g, _sigmoid(ab))


def _gdn_prep(qkv_raw, init, ab, p, nb, tm, bs):
    n = qkv_raw.shape[0]
    nt = n // (nb * tm)
    halo = init.shape[1]
    row = lambda w: pl.BlockSpec((tm, w), lambda b, t: (b * nt + t, 0))
    consts = [p["gdn_conv_w"], p["gdn_alog"], p["gdn_dt"], p["ones_gdn"]]
    return pl.pallas_call(
        functools.partial(_gdn_prep_kernel, halo=halo, bs=bs, tm=tm),
        grid=(nb, nt),
        in_specs=[row(GDN_CONV_CH), pl.BlockSpec((1, halo, GDN_CONV_CH), lambda b, t: (b, 0, 0)), row(LANES)]
        + [_const_spec(c.shape) for c in consts],
        out_specs=[row(GDN_WIDTH)] * 3 + [row(LANES)],
        out_shape=[jax.ShapeDtypeStruct((n, GDN_WIDTH), F32)] * 3 + [jax.ShapeDtypeStruct((n, LANES), F32)],
        scratch_shapes=[pltpu.VMEM((halo + tm, GDN_CONV_CH), F32)],
        compiler_params=_cparams("arbitrary", "arbitrary"),
        name="gdn_prep",
    )(qkv_raw, init, ab, *consts)


def _unit_lower_inverses(a_list, eye, chunk):
    n = [-a for a in a_list]
    inv = [eye + x for x in n]
    for _ in range(int(math.log2(chunk)) - 1):
        ns = [_split_bf16(x) for x in n]
        n = [_dot3(x, x) for x in ns]
        ns = [_split_bf16(x) for x in n]
        inv = [x + _dot3(_split_bf16(x), y) for x, y in zip(inv, ns)]
    return inv


def _gdn_chunk_kernel(q_ref, k_ref, v_ref, gb_ref, s0_ref, o_ref, sout_ref, s_ref, *, chunk, nbb, group):
    ci = pl.program_id(1)

    @pl.when(ci == 0)
    def _():
        s_ref[...] = s0_ref[...]

    rows = lax.broadcasted_iota(jnp.int32, (chunk, chunk), 0)
    cols = lax.broadcasted_iota(jnp.int32, (chunk, chunk), 1)
    causal = rows >= cols
    strict = rows > cols
    eye = (rows == cols).astype(F32)
    tri = causal.astype(F32)

    def one_group(gi, carry):
        units = [(gi * group + j, h) for j in range(group) for h in range(GDN_HEADS)]
        lanes = lambda h: slice(h * GDN_HEAD_DIM, (h + 1) * GDN_HEAD_DIM)
        s = [s_ref[b, h] for b, h in units]
        q = [q_ref[b, :, lanes(h)] for b, h in units]
        k = [k_ref[b, :, lanes(h)] for b, h in units]
        v = [v_ref[b, :, lanes(h)] for b, h in units]
        gb = [gb_ref[gi * group + j] for j in range(group)]
        gcs = [_dot_hi(tri, x) for x in gb]
        gc = [gcs[i // GDN_HEADS][:, h:h + 1] for i, (_, h) in enumerate(units)]
        beta = [gb[i // GDN_HEADS][:, GDN_HEADS + h:GDN_HEADS + h + 1] for i, (_, h) in enumerate(units)]
        decay = [jnp.exp(jnp.where(causal, x - jnp.sum(eye * x, axis=0, keepdims=True), -jnp.inf)) for x in gc]
        kb = [x * y for x, y in zip(k, beta)]
        a = [jnp.where(strict, _dot_nt(x, y) * d, 0.0) for x, y, d in zip(kb, k, decay)]
        inv = _unit_lower_inverses(a, eye, chunk)
        egc = [jnp.exp(x) for x in gc]
        rhs = [jnp.concatenate([x * bt, y * e], axis=1) for x, bt, y, e in zip(v, beta, kb, egc)]
        sol = [_dot3(_split_bf16(x), _split_bf16(y)) for x, y in zip(inv, rhs)]
        wS = [_dot(x[:, GDN_HEAD_DIM:2 * GDN_HEAD_DIM], y) for x, y in zip(sol, s)]
        v_new = [x[:, 0:GDN_HEAD_DIM] - y for x, y in zip(sol, wS)]
        qk = [_dot_nt(x, y) * d for x, y, d in zip(q, k, decay)]
        o_state = [_dot(x * e, y) for x, e, y in zip(q, egc, s)]
        o_chunk = [_dot(x, y) for x, y in zip(qk, v_new)]
        g_last = [x[chunk - 1:chunk, :] for x in gc]
        kd_t = [(x * jnp.exp(gl - g)).T for x, gl, g in zip(k, g_last, gc)]
        s_add = [_dot(x, y) for x, y in zip(kd_t, v_new)]
        for i, (b, h) in enumerate(units):
            o_ref[b, :, lanes(h)] = o_state[i] + o_chunk[i]
            s_ref[b, h] = s[i] * jnp.exp(g_last[i]) + s_add[i]
        return carry

    if nbb == group:
        one_group(0, 0)
    else:
        lax.fori_loop(0, nbb // group, one_group, 0)

    @pl.when(ci == pl.num_programs(1) - 1)
    def _():
        sout_ref[...] = s_ref[...]


def _gdn_chunks(q, k, v, gb, s0, chunk):
    b, t_len, _ = q.shape
    nbb = min(b, GDN_SEQ_BLOCK)
    group = min(nbb, GDN_SEQ_GROUP)
    blk = lambda w: pl.BlockSpec((nbb, chunk, w), lambda bi, ci: (bi, ci, 0))
    st = pl.BlockSpec((nbb, GDN_HEADS, GDN_HEAD_DIM, GDN_HEAD_DIM), lambda bi, ci: (bi, 0, 0, 0))
    return pl.pallas_call(
        functools.partial(_gdn_chunk_kernel, chunk=chunk, nbb=nbb, group=group),
        grid=(b // nbb, t_len // chunk),
        in_specs=[blk(GDN_WIDTH)] * 3 + [blk(LANES), st],
        out_specs=[blk(GDN_WIDTH), st],
        out_shape=[jax.ShapeDtypeStruct(q.shape, F32), jax.ShapeDtypeStruct(s0.shape, F32)],
        scratch_shapes=[pltpu.VMEM((nbb, GDN_HEADS, GDN_HEAD_DIM, GDN_HEAD_DIM), F32)],
        compiler_params=_cparams("arbitrary", "arbitrary"),
        name="gdn_chunks",
    )(q, k, v, gb, s0)


def _route(logits):
    lane = lax.broadcasted_iota(jnp.int32, logits.shape, 1)
    neg = -jnp.inf
    is_group = (lane >= N_EXPERTS) & (lane < N_EXPERTS + N_GROUPS)
    gl = jnp.where(is_group, logits, neg)
    gmax = jnp.max(gl, axis=-1, keepdims=True)
    gp = 1.0 / jnp.sum(jnp.exp(gl - gmax), axis=-1, keepdims=True)
    gidx = jnp.min(jnp.where(gl == gmax, lane, LANES), axis=-1, keepdims=True) - N_EXPERTS
    lo = gidx * EXPERTS_PER_GROUP
    el = jnp.where((lane >= lo) & (lane < lo + EXPERTS_PER_GROUP), logits, neg)
    m1 = jnp.max(el, axis=-1, keepdims=True)
    i1 = jnp.min(jnp.where(el == m1, lane, LANES), axis=-1, keepdims=True)
    el2 = jnp.where(lane == i1, neg, el)
    m2 = jnp.max(el2, axis=-1, keepdims=True)
    i2 = jnp.min(jnp.where(el2 == m2, lane, LANES), axis=-1, keepdims=True)
    e2 = jnp.exp(m2 - m1)
    den = 1.0 + e2
    return jnp.where(lane == i1, gp / den, 0.0) + jnp.where(lane == i2, gp * e2 / den, 0.0)


def _merge_kernel(y_ref, r_ref, k_ref, v_ref, g_ref, o_ref, z_ref, gate_ref, h_ref, lnw_ref, lnb_ref, rk_ref,
                  ones_ref, gnw_ref, woa_ref, wob_ref, wo_ref, n2_ref, rw_ref, rb_ref, h1_ref, xt_ref, mg_ref):
    ones = ones_ref[...]
    inv_n = 1.0 / RW_HEAD_DIM
    y = y_ref[...]
    v = v_ref[...]
    mu = _seg_sum(y, ones) * inv_n
    yc = y - mu
    var = _seg_sum(yc * yc, ones) * inv_n
    ya = yc * lax.rsqrt(var + RW_GN_EPS) * lnw_ref[...] + lnb_ref[...]
    bonus = _seg_sum(r_ref[...] * k_ref[...] * rk_ref[...], ones) * v
    ya = (ya + bonus) * g_ref[...]

    z = z_ref[...]
    gnw = gnw_ref[...]
    yb_parts = []
    for h in range(GDN_HEADS):
        hs = slice(h * GDN_HEAD_DIM, (h + 1) * GDN_HEAD_DIM)
        yb_parts.append(_rmsnorm(o_ref[:, hs], gnw))
    yb = jnp.concatenate(yb_parts, axis=1) * (z * _sigmoid(z))

    gates = gate_ref[...]
    merged = _sigmoid(gates[:, 0:D_MODEL]) * _dot(ya, woa_ref[...]) \
        + _sigmoid(gates[:, D_MODEL:2 * D_MODEL]) * _dot(yb, wob_ref[...])
    h1 = h_ref[...] + _dot(merged, wo_ref[...])
    h1_ref[...] = h1
    xt = _rmsnorm(h1, n2_ref[...])
    xt_ref[...] = xt
    mg_ref[...] = _route(_dot_hi(xt, rw_ref[...]) + rb_ref[...])


def _merge(y, r, k, v, g, o, z, gates, h, p, tm):
    n = y.shape[0]
    row = lambda w: pl.BlockSpec((tm, w), lambda i: (i, 0))
    consts = [p["rw_lnx_w"], p["rw_lnx_b"], p["rw_r_k"], p["ones_rw"], p["gdn_norm_w"], p["w_oA"], p["w_oB"],
              p["w_o"], p["norm2_w"], p["router_w"], p["router_b"]]
    return pl.pallas_call(
        _merge_kernel,
        grid=(n // tm,),
        in_specs=[row(RW_WIDTH)] * 7 + [row(2 * D_MODEL), row(D_MODEL)] + [_const_spec(c.shape) for c in consts],
        out_specs=[row(D_MODEL), row(D_MODEL), row(LANES)],
        out_shape=[jax.ShapeDtypeStruct((n, D_MODEL), F32), jax.ShapeDtypeStruct((n, D_MODEL), F32),
                   jax.ShapeDtypeStruct((n, LANES), F32)],
        compiler_params=_cparams("arbitrary"),
        name="merge",
    )(y, r, k, v, g, o, z, gates, h, *consts)


def _moe_kernel(xt_ref, h1_ref, mg_ref, wg_ref, wu_ref, wd_ref, nf_ref, o_ref):
    e = pl.program_id(1)

    @pl.when(e == 0)
    def _():
        o_ref[...] = jnp.zeros_like(o_ref)

    xb = xt_ref[...].astype(BF16)
    hg = jnp.dot(xb, wg_ref[0].astype(BF16), preferred_element_type=F32)
    hu = jnp.dot(xb, wu_ref[0].astype(BF16), preferred_element_type=F32)
    hid = hg * _sigmoid(hg) * hu
    d = jnp.dot(hid.astype(BF16), wd_ref[0].astype(BF16), preferred_element_type=F32)
    mg = mg_ref[...]
    lane = lax.broadcasted_iota(jnp.int32, mg.shape, 1)
    ge = jnp.sum(jnp.where(lane == e, mg, 0.0), axis=-1, keepdims=True)
    o_ref[...] += ge * d

    @pl.when(e == N_EXPERTS - 1)
    def _():
        o_ref[...] = _rmsnorm(h1_ref[...] + o_ref[...], nf_ref[...])


def _moe(xt, h1, mg, wg, wu, wd, nf, tm):
    n = xt.shape[0]
    row = lambda w: pl.BlockSpec((tm, w), lambda i, e: (i, 0))
    return pl.pallas_call(
        _moe_kernel,
        grid=(n // tm, N_EXPERTS),
        in_specs=[row(D_MODEL), row(D_MODEL), row(LANES),
                  pl.BlockSpec((1, D_MODEL, EXPERT_FF), lambda i, e: (e, 0, 0)),
                  pl.BlockSpec((1, D_MODEL, EXPERT_FF), lambda i, e: (e, 0, 0)),
                  pl.BlockSpec((1, EXPERT_FF, D_MODEL), lambda i, e: (e, 0, 0)),
                  _const_spec((1, D_MODEL))],
        out_specs=row(D_MODEL),
        out_shape=jax.ShapeDtypeStruct((n, D_MODEL), F32),
        compiler_params=_cparams("arbitrary", "arbitrary"),
        name="moe",
    )(xt, h1, mg, wg, wu, wd, nf)


def _pad_rows(x, rows):
    return jnp.pad(x, ((0, rows - x.shape[0]),) + ((0, 0),) * (x.ndim - 1))


def _mixer(x_rows, p, *, batch, t_len, time_major, tm, rw_prev, conv_prev, wkv0, gdn0, chunk, with_merge):
    n = batch * t_len
    xn, rw_cur, qkv_raw, z, gates, ab = _in_proj(x_rows, p["norm1_w"], p["w_in"], min(tm, n))

    if time_major:
        nb, bs, tmt = 1, batch, n
        rw_init = rw_prev[None]
        conv_init = jnp.transpose(conv_prev, (1, 0, 2)).reshape(1, (GDN_CONV - 1) * batch, GDN_CONV_CH)
    else:
        nb, bs, tmt = batch, 1, min(tm, t_len)
        rw_init = jnp.pad(rw_prev[:, None, :], ((0, 0), (SUBLANES - 1, 0), (0, 0)))
        conv_init = jnp.pad(conv_prev, ((0, 0), (SUBLANES - (GDN_CONV - 1), 0), (0, 0)))

    r, dec, k2, v, an, bb, g = _rwkv_prep(rw_cur, rw_init, p, nb, tmt, bs)
    q, kg, vg, gb = _gdn_prep(qkv_raw, conv_init, ab, p, nb, tmt, bs)

    def seq_major(a):
        if time_major:
            return jnp.transpose(a.reshape(t_len, batch, a.shape[-1]), (1, 0, 2))
        return a.reshape(batch, t_len, a.shape[-1])

    def rows_like_x(a):
        if time_major:
            a = jnp.transpose(a, (1, 0, 2))
        return a.reshape(n, a.shape[-1])

    bpad = -(-batch // SCAN_SEQS) * SCAN_SEQS
    padb = lambda a: _pad_rows(a, bpad)
    jvec = jnp.concatenate([_to_scan_lanes(padb(seq_major(a))) for a in (dec, an, bb, k2, r)], axis=2)
    vvec = _to_scan_rows(padb(seq_major(v)))
    tc = min(t_len, 32)
    y_scan, s_scan = _rwkv_scan(jvec, vvec, _state_to_scan(padb(wkv0)), tc)
    y_rw = rows_like_x(_from_scan_rows(y_scan, bpad)[:batch])
    wkv = _state_from_scan(s_scan, bpad)[:batch]

    t_pad = -(-t_len // chunk) * chunk
    def gdn_seq(a):
        a = seq_major(a)
        if t_pad != t_len:
            a = jnp.pad(a, ((0, 0), (0, t_pad - t_len), (0, 0)))
        return a
    o_pad, gdn = _gdn_chunks(gdn_seq(q), gdn_seq(kg), gdn_seq(vg), gdn_seq(gb), gdn0, chunk)
    o = rows_like_x(o_pad[:, :t_len])

    out = dict(xn=xn, rw_cur=rw_cur, qkv_raw=qkv_raw, wkv=wkv, gdn=gdn)
    if with_merge:
        out["h1"], out["xt"], out["mg"] = _merge(y_rw, r, k2, v, g, o, z, gates, x_rows, p, min(tm, n))
    return out


def kernel(x_prompt, x_sample, state_shift, state_wkv, state_conv, state_gdn, meta_tokens, norm1_w, w_in, mu_shift, rw_w0, rw_w2, rw_a0, rw_a2, rw_g2, rw_k_k, rw_k_a, rw_r_k, rw_lnx_w, rw_lnx_b, gdn_conv_w, gdn_A_log, gdn_dt_bias, gdn_norm_w, w_oA, w_oB, w_o, norm2_w, router_g, router_g_b, router_e, router_e_b, moe_w_gate, moe_w_up, moe_w_down, norm_f_w):
    bp, sp, _ = x_prompt.shape
    bsm, ss, _ = x_sample.shape
    row = lambda a: a[0].reshape(1, -1)
    w0 = w_in[0]
    gcol = RW_COLS + 4 * GDN_WIDTH + 2 * GDN_HEADS
    w_packed = jnp.concatenate(
        [w0[:, :RW_COLS + 4 * GDN_WIDTH], w0[:, gcol:], w0[:, RW_COLS + 4 * GDN_WIDTH:gcol],
         jnp.zeros((D_MODEL, LANES - 2 * GDN_HEADS), F32)], axis=1).astype(BF16)
    lane_pad = lambda a: jnp.pad(a, ((0, 0), (0, LANES - a.shape[1])))
    p = dict(
        norm1_w=row(norm1_w), w_in=w_packed, mu_shift=row(mu_shift), rw_w0=row(rw_w0), rw_w2=rw_w2[0],
        rw_a0=row(rw_a0), rw_a2=rw_a2[0], rw_g2=rw_g2[0], rw_k_k=row(rw_k_k), rw_k_a=row(rw_k_a),
        rw_r_k=row(rw_r_k), rw_lnx_w=row(rw_lnx_w), rw_lnx_b=row(rw_lnx_b),
        ones_rw=_block_ones(RW_WIDTH, RW_HEAD_DIM), ones_gdn=_block_ones(GDN_WIDTH, GDN_HEAD_DIM),
        gdn_conv_w=gdn_conv_w[0], gdn_alog=lane_pad(row(gdn_A_log)), gdn_dt=lane_pad(row(gdn_dt_bias)),
        gdn_norm_w=row(gdn_norm_w), w_oA=w_oA[0].astype(BF16), w_oB=w_oB[0].astype(BF16),
        w_o=w_o[0].astype(BF16), norm2_w=row(norm2_w),
        router_w=lane_pad(jnp.concatenate([router_e[0], router_g[0]], axis=1)),
        router_b=lane_pad(jnp.concatenate([row(router_e_b), row(router_g_b)], axis=1)),
    )
    nf = norm_f_w.reshape(1, -1)
    dt = x_prompt.dtype

    meta = _mixer(meta_tokens.astype(dt), p, batch=1, t_len=N_META, time_major=False, tm=N_META,
                  rw_prev=jnp.zeros((1, RW_COLS), dt), conv_prev=jnp.zeros((1, GDN_CONV - 1, GDN_CONV_CH), dt),
                  wkv0=jnp.zeros((1, RW_HEADS, RW_HEAD_DIM, RW_HEAD_DIM), dt),
                  gdn0=jnp.zeros((1, GDN_HEADS, GDN_HEAD_DIM, GDN_HEAD_DIM), dt), chunk=N_META, with_merge=False)
    rep = lambda a: jnp.broadcast_to(a, (bp,) + a.shape[1:])

    prompt = _mixer(x_prompt.reshape(bp * sp, D_MODEL), p, batch=bp, t_len=sp, time_major=False, tm=256,
                    rw_prev=rep(meta["rw_cur"][N_META - 1:]), conv_prev=rep(meta["qkv_raw"][None, N_META - 3:]),
                    wkv0=rep(meta["wkv"]), gdn0=rep(meta["gdn"]), chunk=GDN_CHUNK, with_merge=True)

    xs_rows = jnp.transpose(x_sample, (1, 0, 2)).reshape(ss * bsm, D_MODEL)
    prev_s = _in_proj(state_shift[0].astype(dt), p["norm1_w"], w_packed, bsm, norm=False)[1]
    sample = _mixer(xs_rows, p, batch=bsm, t_len=ss, time_major=True, tm=256,
                    rw_prev=prev_s, conv_prev=state_conv[0].astype(dt), wkv0=state_wkv[0], gdn0=state_gdn[0],
                    chunk=SUBLANES, with_merge=True)

    y_p = _moe(prompt["xt"], prompt["h1"], prompt["mg"], moe_w_gate[0], moe_w_up[0], moe_w_down[0], nf, 1024)
    y_s = _moe(sample["xt"], sample["h1"], sample["mg"], moe_w_gate[0], moe_w_up[0], moe_w_down[0], nf, 512)

    y_prompt = y_p.reshape(bp, sp, D_MODEL)
    y_sample = jnp.transpose(y_s.reshape(ss, bsm, D_MODEL), (1, 0, 2))
    shift_p = prompt["xn"].reshape(bp, sp, D_MODEL)[:, -1][None]
    conv_p = prompt["qkv_raw"].reshape(bp, sp, GDN_CONV_CH)[:, sp - (GDN_CONV - 1):][None]
    shift_s = sample["xn"].reshape(ss, bsm, D_MODEL)[-1][None].astype(state_shift.dtype)
    xpad_s = jnp.concatenate([state_conv[0].astype(dt),
                              jnp.transpose(sample["qkv_raw"].reshape(ss, bsm, GDN_CONV_CH), (1, 0, 2))], axis=1)
    conv_s = xpad_s[:, ss:][None].astype(state_conv.dtype)
    return (y_prompt, y_sample, shift_p, prompt["wkv"][None], conv_p, prompt["gdn"][None],
            shift_s, sample["wkv"][None].astype(state_wkv.dtype), conv_s, sample["gdn"][None].astype(state_gdn.dtype))
```

```python
import functools
import math

import jax
import jax.numpy as jnp
from jax import lax
from jax.experimental import pallas as pl
from jax.experimental.pallas import tpu as pltpu

F32 = jnp.float32
BF16 = jnp.bfloat16
HIGHEST = lax.Precision.HIGHEST

D_MODEL = 1024
N_META = 16
RW_WIDTH = 512
RW_HEAD_DIM = 64
RW_HEADS = 8
RW_DECAY_LORA = 64
RW_AAA_LORA = 64
RW_GATE_LORA = 128
RW_COLS = 3 * RW_WIDTH + RW_DECAY_LORA + RW_AAA_LORA + RW_GATE_LORA
RW_GN_EPS = RW_HEAD_DIM * 1e-5
GDN_WIDTH = 512
GDN_HEAD_DIM = 128
GDN_HEADS = 4
GDN_CONV = 4
GDN_CONV_CH = 3 * GDN_WIDTH
GDN_CHUNK = 64
N_GROUPS = 4
EXPERTS_PER_GROUP = 8
N_EXPERTS = 32
EXPERT_FF = 512
RMS_EPS = 1e-6

LANES = 128
SUBLANES = 8
VMEM_LIMIT_BYTES = 56 * 1024 * 1024

COL_RW = 0
COL_QKV = COL_RW + RW_COLS
COL_Z = COL_QKV + GDN_CONV_CH
COL_GATE = COL_Z + GDN_WIDTH
COL_AB = COL_GATE + 2 * D_MODEL
IN_COLS_PACKED = COL_AB + LANES

SCAN_SEQS = 8
SCAN_ROWS = RW_HEAD_DIM // 2
SCAN_ROW_GROUP = 2

GDN_SEQ_BLOCK = 2
GDN_SEQ_GROUP = 2


def _cparams(*sem):
    return pltpu.CompilerParams(dimension_semantics=sem, vmem_limit_bytes=VMEM_LIMIT_BYTES)


def _const_spec(shape):
    zeros = (0,) * len(shape)
    return pl.BlockSpec(shape, lambda *_: zeros)


def _dot(a, b):
    return jnp.dot(a.astype(BF16), b.astype(BF16), preferred_element_type=F32)


def _dot_hi(a, b):
    return jnp.dot(a, b, preferred_element_type=F32, precision=HIGHEST)


def _split_bf16(a):
    hi = a.astype(BF16)
    return hi, (a - hi.astype(F32)).astype(BF16)


def _dot3(a, b):
    d = lambda x, y: jnp.dot(x, y, preferred_element_type=F32)
    return d(a[0], b[0]) + d(a[0], b[1]) + d(a[1], b[0])


def _dot_f32(a, b):
    return _dot3(_split_bf16(a), _split_bf16(b))


def _seg_sum(x, ones):
    hi, lo = _split_bf16(x)
    return jnp.dot(hi, ones, preferred_element_type=F32) + jnp.dot(lo, ones, preferred_element_type=F32)


def _dot_nt(a, b):
    dims = (((1,), (1,)), ((), ()))
    return lax.dot_general(a.astype(BF16), b.astype(BF16), dims, preferred_element_type=F32)


def _sigmoid(x):
    return 1.0 / (1.0 + jnp.exp(-x))


def _softplus(x):
    return jnp.maximum(x, 0.0) + jnp.log1p(jnp.exp(-jnp.abs(x)))


def _rmsnorm(x, w):
    ms = jnp.mean(x * x, axis=-1, keepdims=True)
    return x * lax.rsqrt(ms + RMS_EPS) * w


def _block_ones(width, seg):
    r = jnp.arange(width) // seg
    return (r[:, None] == r[None, :]).astype(BF16)


def _in_proj_kernel(x_ref, nw_ref, w_ref, xn_ref, rw_ref, qkv_ref, z_ref, gate_ref, ab_ref, *, norm):
    x = x_ref[...]
    xn = _rmsnorm(x, nw_ref[...]) if norm else x
    xn_ref[...] = xn
    xb = xn.astype(BF16)
    rw_ref[...] = jnp.dot(xb, w_ref[:, COL_RW:COL_QKV], preferred_element_type=F32)
    qkv_ref[...] = jnp.dot(xb, w_ref[:, COL_QKV:COL_Z], preferred_element_type=F32)
    z_ref[...] = jnp.dot(xb, w_ref[:, COL_Z:COL_GATE], preferred_element_type=F32)
    gate_ref[...] = jnp.dot(xb, w_ref[:, COL_GATE:COL_AB], preferred_element_type=F32)
    ab_ref[...] = jnp.dot(xb, w_ref[:, COL_AB:IN_COLS_PACKED], preferred_element_type=F32)


def _in_proj(x, norm_w, w_packed, tm, norm=True):
    n = x.shape[0]
    widths = (D_MODEL, RW_COLS, GDN_CONV_CH, GDN_WIDTH, 2 * D_MODEL, LANES)
    row = lambda w: pl.BlockSpec((tm, w), lambda i: (i, 0))
    return pl.pallas_call(
        functools.partial(_in_proj_kernel, norm=norm),
        grid=(n // tm,),
        in_specs=[row(D_MODEL), _const_spec((1, D_MODEL)), _const_spec(w_packed.shape)],
        out_specs=[row(w) for w in widths],
        out_shape=[jax.ShapeDtypeStruct((n, w), F32) for w in widths],
        compiler_params=_cparams("arbitrary"),
        name="in_proj",
    )(x, norm_w, w_packed)


def _stage_tile(ext_ref, init_ref, cur, halo, tm):
    t = pl.program_id(1)

    @pl.when(t == 0)
    def _():
        ext_ref[0:halo, :] = init_ref[0]

    @pl.when(t > 0)
    def _():
        ext_ref[0:halo, :] = ext_ref[tm:tm + halo, :]

    ext_ref[halo:halo + tm, :] = cur


def _rwkv_prep_kernel(cur_ref, init_ref, mu_ref, w0_ref, w2_ref, a0_ref, a2_ref, g2_ref, kk_ref, ka_ref,
                      ones_ref, wa_ref, bk_ref, rr_ref, v_ref, g_ref, ext_ref, *, halo, bs, tm):
    cur = cur_ref[...]
    _stage_tile(ext_ref, init_ref, cur, halo, tm)
    prev = ext_ref[halo - bs:halo - bs + tm, :]
    zc = cur + (prev - cur) * mu_ref[...]
    c0, c1, c2 = RW_WIDTH, 2 * RW_WIDTH, 3 * RW_WIDTH
    r = zc[:, 0:c0]
    k = zc[:, c0:c1]
    v = zc[:, c1:c2]
    xw = zc[:, c2:c2 + RW_DECAY_LORA]
    xa = zc[:, c2 + RW_DECAY_LORA:c2 + RW_DECAY_LORA + RW_AAA_LORA]
    xg = zc[:, c2 + RW_DECAY_LORA + RW_AAA_LORA:RW_COLS]
    w_raw = -_softplus(-(w0_ref[...] + _dot_f32(jnp.tanh(xw), w2_ref[...]))) - 0.5
    decay = jnp.exp(-jnp.exp(w_raw))
    a = _sigmoid(a0_ref[...] + _dot_f32(xa, a2_ref[...]))
    g = _dot_f32(_sigmoid(xg), g2_ref[...])
    kkr = k * kk_ref[...]
    kk = kkr * lax.rsqrt(_seg_sum(kkr * kkr, ones_ref[...]) + 1e-6)
    wa_ref[:, 0:RW_WIDTH] = decay
    wa_ref[:, RW_WIDTH:2 * RW_WIDTH] = -kk
    bk_ref[:, 0:RW_WIDTH] = kk * a
    bk_ref[:, RW_WIDTH:2 * RW_WIDTH] = k * (1.0 + (a - 1.0) * ka_ref[...])
    rr_ref[:, 0:RW_WIDTH] = r
    rr_ref[:, RW_WIDTH:2 * RW_WIDTH] = r
    v_ref[...] = v
    g_ref[...] = g


def _rwkv_prep(rw_cur, init, p, nb, tm, bs):
    n = rw_cur.shape[0]
    nt = n // (nb * tm)
    halo = init.shape[1]
    row = lambda w: pl.BlockSpec((tm, w), lambda b, t: (b * nt + t, 0))
    consts = [p["mu_shift"], p["rw_w0"], p["rw_w2"], p["rw_a0"], p["rw_a2"], p["rw_g2"], p["rw_k_k"], p["rw_k_a"],
              p["ones_rw"]]
    return pl.pallas_call(
        functools.partial(_rwkv_prep_kernel, halo=halo, bs=bs, tm=tm),
        grid=(nb, nt),
        in_specs=[row(RW_COLS), pl.BlockSpec((1, halo, RW_COLS), lambda b, t: (b, 0, 0))]
        + [_const_spec(c.shape) for c in consts],
        out_specs=[row(2 * RW_WIDTH)] * 3 + [row(RW_WIDTH)] * 2,
        out_shape=[jax.ShapeDtypeStruct((n, 2 * RW_WIDTH), F32)] * 3 + [jax.ShapeDtypeStruct((n, RW_WIDTH), F32)] * 2,
        scratch_shapes=[pltpu.VMEM((halo + tm, RW_COLS), F32)],
        compiler_params=_cparams("arbitrary", "arbitrary"),
        name="rwkv_prep",
    )(rw_cur, init, *consts)


def _sublane_allsum(x):
    x = x + pltpu.roll(x, 4, 0)
    x = x + pltpu.roll(x, 2, 0)
    return x + pltpu.roll(x, 1, 0)


def _rwkv_scan_kernel(wa_ref, bk_ref, rr_ref, v_ref, s0_ref, y_ref, sout_ref, s_ref, vec_ref, *, tc):
    c = pl.program_id(1)
    nj = RW_HEAD_DIM // SUBLANES
    low_half = lax.broadcasted_iota(jnp.int32, (RW_HEAD_DIM, LANES), 1) < LANES // 2

    @pl.when(c == 0)
    def _():
        s_ref[...] = s0_ref[0]

    def unpack(t, slot):
        for pair, src_ref in enumerate((wa_ref, bk_ref)):
            x = src_ref[0, t]
            rot = pltpu.roll(x, LANES // 2, 1)
            vec_ref[slot, 2 * pair] = jnp.where(low_half, x, rot)
            vec_ref[slot, 2 * pair + 1] = jnp.where(low_half, rot, x)

    def step(t, slot):
        def vec(which, jb):
            rows = pl.ds(jb * SUBLANES, SUBLANES)
            return rr_ref[0, t, rows, :] if which == 4 else vec_ref[slot, which, rows, :]

        for i0 in range(0, SCAN_ROWS, SCAN_ROW_GROUP):
            rows = range(i0, i0 + SCAN_ROW_GROUP)
            s = {i: [s_ref[i, jb * SUBLANES:(jb + 1) * SUBLANES, :] for jb in range(nj)] for i in rows}
            acc = {}
            for jb in range(nj):
                a = vec(1, jb)
                for i in rows:
                    acc[i] = s[i][jb] * a if jb == 0 else acc[i] + s[i][jb] * a
            sa = {i: _sublane_allsum(acc[i]) for i in rows}
            vb = {i: jnp.broadcast_to(v_ref[0, t, i:i + 1, :], (SUBLANES, LANES)) for i in rows}
            accy = {}
            for jb in range(nj):
                w, b, k, r = vec(0, jb), vec(2, jb), vec(3, jb), vec(4, jb)
                for i in rows:
                    sn = s[i][jb] * w + sa[i] * b + vb[i] * k
                    s_ref[i, jb * SUBLANES:(jb + 1) * SUBLANES, :] = sn
                    accy[i] = sn * r if jb == 0 else accy[i] + sn * r
            for i in rows:
                y_ref[0, t, i:i + 1, :] = _sublane_allsum(accy[i])[0:1, :]
        unpack(jnp.minimum(t + 1, tc - 1), 1 - slot)

    def two_steps(t2, carry):
        step(2 * t2, 0)
        step(2 * t2 + 1, 1)
        return carry

    unpack(0, 0)
    lax.fori_loop(0, tc // 2, two_steps, 0)

    @pl.when(c == pl.num_programs(1) - 1)
    def _():
        sout_ref[0] = s_ref[...]


def _rwkv_scan(wa, bk, rr, vvec, s0, tc):
    g, t = wa.shape[0], wa.shape[1]
    jspec = pl.BlockSpec((1, tc, RW_HEAD_DIM, LANES), lambda gi, c: (gi, c, 0, 0))
    return pl.pallas_call(
        functools.partial(_rwkv_scan_kernel, tc=tc),
        grid=(g, t // tc),
        in_specs=[jspec, jspec, jspec,
                  pl.BlockSpec((1, tc, SCAN_ROWS, LANES), lambda gi, c: (gi, c, 0, 0)),
                  pl.BlockSpec((1, SCAN_ROWS, RW_HEAD_DIM, LANES), lambda gi, c: (gi, 0, 0, 0))],
        out_specs=[pl.BlockSpec((1, tc, SCAN_ROWS, LANES), lambda gi, c: (gi, c, 0, 0)),
                   pl.BlockSpec((1, SCAN_ROWS, RW_HEAD_DIM, LANES), lambda gi, c: (gi, 0, 0, 0))],
        out_shape=[jax.ShapeDtypeStruct((g, t, SCAN_ROWS, LANES), F32),
                   jax.ShapeDtypeStruct((g, SCAN_ROWS, RW_HEAD_DIM, LANES), F32)],
        scratch_shapes=[pltpu.VMEM((SCAN_ROWS, RW_HEAD_DIM, LANES), F32), pltpu.VMEM((2, 4, RW_HEAD_DIM, LANES), F32)],
        compiler_params=_cparams("arbitrary", "arbitrary"),
        name="rwkv_scan",
    )(wa, bk, rr, vvec, s0)


def _to_scan_pair(x):
    b, t, _ = x.shape
    g = b // SCAN_SEQS
    x = x.reshape(g, SCAN_SEQS, t, 2, RW_HEADS, RW_HEAD_DIM)
    return jnp.transpose(x, (0, 2, 5, 3, 1, 4)).reshape(g, t, RW_HEAD_DIM, LANES)


def _to_scan_rows(x):
    b, t, _ = x.shape
    g = b // SCAN_SEQS
    x = x.reshape(g, SCAN_SEQS, t, RW_HEADS, 2, SCAN_ROWS)
    return jnp.transpose(x, (0, 2, 5, 4, 1, 3)).reshape(g, t, SCAN_ROWS, LANES)


def _from_scan_rows(y, b):
    g, t = y.shape[0], y.shape[1]
    y = y.reshape(g, t, SCAN_ROWS, 2, SCAN_SEQS, RW_HEADS)
    return jnp.transpose(y, (0, 4, 1, 5, 3, 2)).reshape(b, t, RW_WIDTH)


def _state_to_scan(s):
    b = s.shape[0]
    g = b // SCAN_SEQS
    s = s.reshape(g, SCAN_SEQS, RW_HEADS, 2, SCAN_ROWS, RW_HEAD_DIM)
    return jnp.transpose(s, (0, 4, 5, 3, 1, 2)).reshape(g, SCAN_ROWS, RW_HEAD_DIM, LANES)


def _state_from_scan(s, b):
    g = s.shape[0]
    s = s.reshape(g, SCAN_ROWS, RW_HEAD_DIM, 2, SCAN_SEQS, RW_HEADS)
    return jnp.transpose(s, (0, 4, 5, 3, 1, 2)).reshape(b, RW_HEADS, RW_HEAD_DIM, RW_HEAD_DIM)


def _gdn_prep_kernel(x_ref, init_ref, ab_ref, cw_ref, alog_ref, dt_ref, ones_ref, q_ref, k_ref, v_ref, gb_ref,
                     ext_ref, *, halo, bs, tm):
    cur = x_ref[...]
    _stage_tile(ext_ref, init_ref, cur, halo, tm)
    conv = cur * cw_ref[GDN_CONV - 1:GDN_CONV, :]
    for s in range(1, GDN_CONV):
        off = halo - s * bs
        conv = conv + ext_ref[off:off + tm, :] * cw_ref[GDN_CONV - 1 - s:GDN_CONV - s, :]
    qkv = conv * _sigmoid(conv)
    q = qkv[:, 0:GDN_WIDTH]
    k = qkv[:, GDN_WIDTH:2 * GDN_WIDTH]
    ones = ones_ref[...]
    q_ref[...] = q * lax.rsqrt(_seg_sum(q * q, ones) + 1e-6) * (GDN_HEAD_DIM ** -0.5)
    k_ref[...] = k * lax.rsqrt(_seg_sum(k * k, ones) + 1e-6)
    v_ref[...] = qkv[:, 2 * GDN_WIDTH:3 * GDN_WIDTH]
    ab = ab_ref[...]
    g = -jnp.exp(alog_ref[...]) * _softplus(ab + dt_ref[...])
    lane = lax.broadcasted_iota(jnp.int32, ab.shape, 1)
    gb_ref[...] = jnp.where(lane < GDN_HEADS, g, _sigmoid(ab))


def _gdn_prep(qkv_raw, init, ab, p, nb, tm, bs):
    n = qkv_raw.shape[0]
    nt = n // (nb * tm)
    halo = init.shape[1]
    row = lambda w: pl.BlockSpec((tm, w), lambda b, t: (b * nt + t, 0))
    consts = [p["gdn_conv_w"], p["gdn_alog"], p["gdn_dt"], p["ones_gdn"]]
    return pl.pallas_call(
        functools.partial(_gdn_prep_kernel, halo=halo, bs=bs, tm=tm),
        grid=(nb, nt),
        in_specs=[row(GDN_CONV_CH), pl.BlockSpec((1, halo, GDN_CONV_CH), lambda b, t: (b, 0, 0)), row(LANES)]
        + [_const_spec(c.shape) for c in consts],
        out_specs=[row(GDN_WIDTH)] * 3 + [row(LANES)],
        out_shape=[jax.ShapeDtypeStruct((n, GDN_WIDTH), F32)] * 3 + [jax.ShapeDtypeStruct((n, LANES), F32)],
        scratch_shapes=[pltpu.VMEM((halo + tm, GDN_CONV_CH), F32)],
        compiler_params=_cparams("arbitrary", "arbitrary"),
        name="gdn_prep",
    )(qkv_raw, init, ab, *consts)


def _unit_lower_inverses(a_list, eye, chunk):
    n = [-a for a in a_list]
    inv = [eye + x for x in n]
    for _ in range(int(math.log2(chunk)) - 1):
        ns = [_split_bf16(x) for x in n]
        n = [_dot3(x, x) for x in ns]
        ns = [_split_bf16(x) for x in n]
        inv = [x + _dot3(_split_bf16(x), y) for x, y in zip(inv, ns)]
    return inv


def _gdn_chunk_kernel(q_ref, k_ref, v_ref, gb_ref, s0_ref, o_ref, sout_ref, s_ref, *, chunk, nbb, group):
    ci = pl.program_id(1)

    @pl.when(ci == 0)
    def _():
        s_ref[...] = s0_ref[...]

    rows = lax.broadcasted_iota(jnp.int32, (chunk, chunk), 0)
    cols = lax.broadcasted_iota(jnp.int32, (chunk, chunk), 1)
    causal = rows >= cols
    strict = rows > cols
    eye = (rows == cols).astype(F32)
    tri = causal.astype(F32)

    def one_group(gi, carry):
        units = [(gi * group + j, h) for j in range(group) for h in range(GDN_HEADS)]
        lanes = lambda h: slice(h * GDN_HEAD_DIM, (h + 1) * GDN_HEAD_DIM)
        s = [s_ref[b, h] for b, h in units]
        q = [q_ref[b, :, lanes(h)] for b, h in units]
        k = [k_ref[b, :, lanes(h)] for b, h in units]
        v = [v_ref[b, :, lanes(h)] for b, h in units]
        gb = [gb_ref[gi * group + j] for j in range(group)]
        gcs = [_dot_hi(tri, x) for x in gb]
        gc = [gcs[i // GDN_HEADS][:, h:h + 1] for i, (_, h) in enumerate(units)]
        beta = [gb[i // GDN_HEADS][:, GDN_HEADS + h:GDN_HEADS + h + 1] for i, (_, h) in enumerate(units)]
        decay = [jnp.exp(jnp.where(causal, x - jnp.sum(eye * x, axis=0, keepdims=True), -jnp.inf)) for x in gc]
        kb = [x * y for x, y in zip(k, beta)]
        a = [jnp.where(strict, _dot_nt(x, y) * d, 0.0) for x, y, d in zip(kb, k, decay)]
        inv = _unit_lower_inverses(a, eye, chunk)
        egc = [jnp.exp(x) for x in gc]
        rhs = [jnp.concatenate([x * bt, y * e], axis=1) for x, bt, y, e in zip(v, beta, kb, egc)]
        sol = [_dot3(_split_bf16(x), _split_bf16(y)) for x, y in zip(inv, rhs)]
        wS = [_dot(x[:, GDN_HEAD_DIM:2 * GDN_HEAD_DIM], y) for x, y in zip(sol, s)]
        v_new = [x[:, 0:GDN_HEAD_DIM] - y for x, y in zip(sol, wS)]
        qk = [_dot_nt(x, y) * d for x, y, d in zip(q, k, decay)]
        o_state = [_dot(x * e, y) for x, e, y in zip(q, egc, s)]
        o_chunk = [_dot(x, y) for x, y in zip(qk, v_new)]
        g_last = [x[chunk - 1:chunk, :] for x in gc]
        kd_t = [(x * jnp.exp(gl - g)).T for x, gl, g in zip(k, g_last, gc)]
        s_add = [_dot(x, y) for x, y in zip(kd_t, v_new)]
        for i, (b, h) in enumerate(units):
            o_ref[b, :, lanes(h)] = o_state[i] + o_chunk[i]
            s_ref[b, h] = s[i] * jnp.exp(g_last[i]) + s_add[i]
        return carry

    if nbb == group:
        one_group(0, 0)
    else:
        lax.fori_loop(0, nbb // group, one_group, 0)

    @pl.when(ci == pl.num_programs(1) - 1)
    def _():
        sout_ref[...] = s_ref[...]


def _gdn_chunks(q, k, v, gb, s0, chunk):
    b, t_len, _ = q.shape
    nbb = min(b, GDN_SEQ_BLOCK)
    group = min(nbb, GDN_SEQ_GROUP)
    blk = lambda w: pl.BlockSpec((nbb, chunk, w), lambda bi, ci: (bi, ci, 0))
    st = pl.BlockSpec((nbb, GDN_HEADS, GDN_HEAD_DIM, GDN_HEAD_DIM), lambda bi, ci: (bi, 0, 0, 0))
    return pl.pallas_call(
        functools.partial(_gdn_chunk_kernel, chunk=chunk, nbb=nbb, group=group),
        grid=(b // nbb, t_len // chunk),
        in_specs=[blk(GDN_WIDTH)] * 3 + [blk(LANES), st],
        out_specs=[blk(GDN_WIDTH), st],
        out_shape=[jax.ShapeDtypeStruct(q.shape, F32), jax.ShapeDtypeStruct(s0.shape, F32)],
        scratch_shapes=[pltpu.VMEM((nbb, GDN_HEADS, GDN_HEAD_DIM, GDN_HEAD_DIM), F32)],
        compiler_params=_cparams("arbitrary", "arbitrary"),
        name="gdn_chunks",
    )(q, k, v, gb, s0)


def _route(logits, count_ref):
    lane = lax.broadcasted_iota(jnp.int32, logits.shape, 1)
    neg = -jnp.inf
    is_group = (lane >= N_EXPERTS) & (lane < N_EXPERTS + N_GROUPS)
    gl = jnp.where(is_group, logits, neg)
    gmax = jnp.max(gl, axis=-1, keepdims=True)
    gp = 1.0 / jnp.sum(jnp.exp(gl - gmax), axis=-1, keepdims=True)
    gidx = jnp.min(jnp.where(gl == gmax, lane, LANES), axis=-1, keepdims=True) - N_EXPERTS
    lo = gidx * EXPERTS_PER_GROUP
    el = jnp.where((lane >= lo) & (lane < lo + EXPERTS_PER_GROUP), logits, neg)
    m1 = jnp.max(el, axis=-1, keepdims=True)
    i1 = jnp.min(jnp.where(el == m1, lane, LANES), axis=-1, keepdims=True)
    el2 = jnp.where(lane == i1, neg, el)
    m2 = jnp.max(el2, axis=-1, keepdims=True)
    i2 = jnp.min(jnp.where(el2 == m2, lane, LANES), axis=-1, keepdims=True)
    e2 = jnp.exp(m2 - m1)
    den = 1.0 + e2
    tm = logits.shape[0]
    hit1 = lane == i1
    hit2 = lane == i2
    chosen = jnp.where(hit1 | hit2, 1.0, 0.0)
    before = lax.broadcasted_iota(jnp.int32, (tm, tm), 0) > lax.broadcasted_iota(jnp.int32, (tm, tm), 1)
    seen = jnp.dot(before.astype(BF16), chosen.astype(BF16), preferred_element_type=F32) + count_ref[...]
    rank1 = jnp.sum(jnp.where(hit1, seen, 0.0), axis=-1, keepdims=True)
    rank2 = jnp.sum(jnp.where(hit2, seen, 0.0), axis=-1, keepdims=True)
    count_ref[...] += jnp.sum(chosen, axis=0, keepdims=True)
    fields = (i1.astype(F32), i2.astype(F32), gp / den, gp * e2 / den, rank1, rank2)
    out = jnp.zeros(logits.shape, F32)
    for j, f in enumerate(fields):
        out = jnp.where(lane == j, f, out)
    return out


def _merge_kernel(y_ref, r_ref, k_ref, v_ref, g_ref, o_ref, z_ref, gate_ref, h_ref, lnw_ref, lnb_ref, rk_ref,
                  ones_ref, gnw_ref, woa_ref, wob_ref, wo_ref, n2_ref, rw_ref, rb_ref, h1_ref, xt_ref, mg_ref, cnt_ref):
    @pl.when(pl.program_id(0) == 0)
    def _():
        cnt_ref[...] = jnp.zeros_like(cnt_ref)

    ones = ones_ref[...]
    inv_n = 1.0 / RW_HEAD_DIM
    y = y_ref[...]
    v = v_ref[...]
    mu = _seg_sum(y, ones) * inv_n
    yc = y - mu
    var = _seg_sum(yc * yc, ones) * inv_n
    ya = yc * lax.rsqrt(var + RW_GN_EPS) * lnw_ref[...] + lnb_ref[...]
    bonus = _seg_sum(r_ref[...] * k_ref[...] * rk_ref[...], ones) * v
    ya = (ya + bonus) * g_ref[...]

    z = z_ref[...]
    gnw = gnw_ref[...]
    yb_parts = []
    for h in range(GDN_HEADS):
        hs = slice(h * GDN_HEAD_DIM, (h + 1) * GDN_HEAD_DIM)
        yb_parts.append(_rmsnorm(o_ref[:, hs], gnw))
    yb = jnp.concatenate(yb_parts, axis=1) * (z * _sigmoid(z))

    gates = gate_ref[...]
    merged = _sigmoid(gates[:, 0:D_MODEL]) * _dot(ya, woa_ref[...]) \
        + _sigmoid(gates[:, D_MODEL:2 * D_MODEL]) * _dot(yb, wob_ref[...])
    h1 = h_ref[...] + _dot(merged, wo_ref[...])
    h1_ref[...] = h1
    xt = _rmsnorm(h1, n2_ref[...])
    xt_ref[...] = xt
    mg_ref[...] = _route(_dot_hi(xt, rw_ref[...]) + rb_ref[...], cnt_ref)


def _merge(y, r, k, v, g, o, z, gates, h, p, tm):
    n = y.shape[0]
    row = lambda w: pl.BlockSpec((tm, w), lambda i: (i, 0))
    consts = [p["rw_lnx_w"], p["rw_lnx_b"], p["rw_r_k"], p["ones_rw"], p["gdn_norm_w"], p["w_oA"], p["w_oB"],
              p["w_o"], p["norm2_w"], p["router_w"], p["router_b"]]
    return pl.pallas_call(
        _merge_kernel,
        grid=(n // tm,),
        in_specs=[row(RW_WIDTH), row(RW_WIDTH), pl.BlockSpec((tm, RW_WIDTH), lambda i: (i, 1))] + [row(RW_WIDTH)] * 4
        + [row(2 * D_MODEL), row(D_MODEL)] + [_const_spec(c.shape) for c in consts],
        out_specs=[row(D_MODEL), row(D_MODEL), row(LANES), _const_spec((1, LANES))],
        out_shape=[jax.ShapeDtypeStruct((n, D_MODEL), F32), jax.ShapeDtypeStruct((n, D_MODEL), F32),
                   jax.ShapeDtypeStruct((n, LANES), F32), jax.ShapeDtypeStruct((1, LANES), F32)],
        compiler_params=_cparams("arbitrary"),
        name="merge",
    )(y, r, k, v, g, o, z, gates, h, *consts)


def _start_row_gather(src_hbm, idx_ref, dst_ref, sem, n):
    def body(r, carry):
        pltpu.make_async_copy(src_hbm.at[pl.ds(idx_ref[0, 0, r], 1)], dst_ref.at[pl.ds(r, 1)], sem).start()
        return carry

    lax.fori_loop(0, n, body, 0, unroll=8)


def _wait_row_gather(src_hbm, dst_ref, sem, n):
    def body(r, carry):
        pltpu.make_async_copy(src_hbm.at[pl.ds(0, 1)], dst_ref.at[pl.ds(r, 1)], sem).wait()
        return carry

    lax.fori_loop(0, n, body, 0, unroll=8)


def _moe_experts_kernel(te_ref, tv_ref, cur_ref, nxt_ref, x_hbm, wg_ref, wu_ref, wd_ref, y_ref, xbuf, sems, *, tm):
    i = pl.program_id(0)
    nt = pl.num_programs(0)
    slot = i % 2

    @pl.when(i == 0)
    def _():
        _start_row_gather(x_hbm, cur_ref, xbuf.at[0], sems.at[0], tm)

    @pl.when(i + 1 < nt)
    def _():
        _start_row_gather(x_hbm, nxt_ref, xbuf.at[1 - slot], sems.at[1 - slot], tm)

    _wait_row_gather(x_hbm, xbuf.at[slot], sems.at[slot], tm)

    @pl.when(tv_ref[i] == 1)
    def _():
        xb = xbuf[slot].astype(BF16)
        hg = jnp.dot(xb, wg_ref[0].astype(BF16), preferred_element_type=F32)
        hu = jnp.dot(xb, wu_ref[0].astype(BF16), preferred_element_type=F32)
        hid = hg * _sigmoid(hg) * hu
        y_ref[...] = jnp.dot(hid.astype(BF16), wd_ref[0].astype(BF16), preferred_element_type=F32)

    @pl.when(tv_ref[i] == 0)
    def _():
        y_ref[...] = jnp.zeros_like(y_ref)


def _moe_experts(xt, tok, tile_expert, tile_valid, wg, wu, wd, tm):
    nt = tok.shape[0]
    idx = lambda f: pl.BlockSpec((1, 1, tm), f, memory_space=pltpu.SMEM)
    wspec = lambda shape: pl.BlockSpec((1,) + shape, lambda i, te, tv: (te[i], 0, 0))
    return pl.pallas_call(
        functools.partial(_moe_experts_kernel, tm=tm),
        grid_spec=pltpu.PrefetchScalarGridSpec(
            num_scalar_prefetch=2,
            grid=(nt,),
            in_specs=[idx(lambda i, te, tv: (i, 0, 0)), idx(lambda i, te, tv: (jnp.minimum(i + 1, nt - 1), 0, 0)),
                      pl.BlockSpec(memory_space=pl.ANY),
                      wspec((D_MODEL, EXPERT_FF)), wspec((D_MODEL, EXPERT_FF)), wspec((EXPERT_FF, D_MODEL))],
            out_specs=pl.BlockSpec((tm, D_MODEL), lambda i, te, tv: (i, 0)),
            scratch_shapes=[pltpu.VMEM((2, tm, D_MODEL), F32), pltpu.SemaphoreType.DMA((2,))],
        ),
        out_shape=jax.ShapeDtypeStruct((nt * tm, D_MODEL), F32),
        compiler_params=_cparams("arbitrary"),
        name="moe_experts",
    )(tile_expert, tile_valid, tok, tok, xt, wg, wu, wd)


def _moe_combine_kernel(cur1_ref, cur2_ref, nxt1_ref, nxt2_ref, ys_hbm, h1_ref, mg_ref, nf_ref, o_ref, buf1, buf2,
                        sems, *, tm):
    i = pl.program_id(0)
    nt = pl.num_programs(0)
    slot = i % 2

    @pl.when(i == 0)
    def _():
        _start_row_gather(ys_hbm, cur1_ref, buf1.at[0], sems.at[0, 0], tm)
        _start_row_gather(ys_hbm, cur2_ref, buf2.at[0], sems.at[1, 0], tm)

    @pl.when(i + 1 < nt)
    def _():
        _start_row_gather(ys_hbm, nxt1_ref, buf1.at[1 - slot], sems.at[0, 1 - slot], tm)
        _start_row_gather(ys_hbm, nxt2_ref, buf2.at[1 - slot], sems.at[1, 1 - slot], tm)

    _wait_row_gather(ys_hbm, buf1.at[slot], sems.at[0, slot], tm)
    _wait_row_gather(ys_hbm, buf2.at[slot], sems.at[1, slot], tm)
    mg = mg_ref[...]
    moe = mg[:, 2:3] * buf1[slot] + mg[:, 3:4] * buf2[slot]
    o_ref[...] = _rmsnorm(h1_ref[...] + moe, nf_ref[...])


def _moe_combine(ys, pos1, pos2, h1, mg, nf, tm):
    n = h1.shape[0]
    nt = n // tm
    cur = pl.BlockSpec((1, 1, tm), lambda i: (i, 0, 0), memory_space=pltpu.SMEM)
    nxt = pl.BlockSpec((1, 1, tm), lambda i: (jnp.minimum(i + 1, nt - 1), 0, 0), memory_space=pltpu.SMEM)
    row = lambda w: pl.BlockSpec((tm, w), lambda i: (i, 0))
    return pl.pallas_call(
        functools.partial(_moe_combine_kernel, tm=tm),
        grid=(nt,),
        in_specs=[cur, cur, nxt, nxt, pl.BlockSpec(memory_space=pl.ANY), row(D_MODEL), row(LANES),
                  _const_spec((1, D_MODEL))],
        out_specs=row(D_MODEL),
        out_shape=jax.ShapeDtypeStruct((n, D_MODEL), F32),
        scratch_shapes=[pltpu.VMEM((2, tm, D_MODEL), F32), pltpu.VMEM((2, tm, D_MODEL), F32),
                        pltpu.SemaphoreType.DMA((2, 2))],
        compiler_params=_cparams("arbitrary"),
        name="moe_combine",
    )(pos1, pos2, pos1, pos2, ys, h1, mg, nf)


def _moe(xt, h1, mg, counts, wg, wu, wd, nf, tm_expert, tm_token):
    n = xt.shape[0]
    i32 = jnp.int32
    cnt = counts[0, :N_EXPERTS].astype(i32)
    padded = (cnt + tm_expert - 1) // tm_expert * tm_expert
    ends = jnp.cumsum(padded)
    starts = ends - padded
    e1, e2 = mg[:, 0].astype(i32), mg[:, 1].astype(i32)
    pos1 = starts[e1] + mg[:, 4].astype(i32)
    pos2 = starts[e2] + mg[:, 5].astype(i32)
    nt = (2 * n + N_EXPERTS * (tm_expert - 1)) // tm_expert + 1
    token = jnp.arange(n, dtype=i32)
    tok = jnp.zeros((nt * tm_expert,), i32).at[pos1].set(token).at[pos2].set(token)
    tile_start = jnp.arange(nt, dtype=i32) * tm_expert
    tile_valid = (tile_start < ends[-1]).astype(i32)
    tile_expert = jnp.minimum(jnp.searchsorted(ends, tile_start, side="right"), N_EXPERTS - 1).astype(i32)
    tile_expert = jnp.where(tile_valid == 1, tile_expert, tile_expert[jnp.maximum(ends[-1] // tm_expert - 1, 0)])
    ys = _moe_experts(xt, tok.reshape(nt, 1, tm_expert), tile_expert, tile_valid, wg, wu, wd, tm_expert)
    shape = (n // tm_token, 1, tm_token)
    return _moe_combine(ys, pos1.reshape(shape), pos2.reshape(shape), h1, mg, nf, tm_token)


def _pad_rows(x, rows):
    return jnp.pad(x, ((0, rows - x.shape[0]),) + ((0, 0),) * (x.ndim - 1))


def _mixer(x_rows, p, *, batch, t_len, time_major, tm, rw_prev, conv_prev, wkv0, gdn0, chunk, with_merge):
    n = batch * t_len
    xn, rw_cur, qkv_raw, z, gates, ab = _in_proj(x_rows, p["norm1_w"], p["w_in"], min(tm, n))

    if time_major:
        nb, bs, tmt = 1, batch, n
        rw_init = rw_prev[None]
        conv_init = jnp.transpose(conv_prev, (1, 0, 2)).reshape(1, (GDN_CONV - 1) * batch, GDN_CONV_CH)
    else:
        nb, bs, tmt = batch, 1, min(tm, t_len)
        rw_init = jnp.pad(rw_prev[:, None, :], ((0, 0), (SUBLANES - 1, 0), (0, 0)))
        conv_init = jnp.pad(conv_prev, ((0, 0), (SUBLANES - (GDN_CONV - 1), 0), (0, 0)))

    wa, bk, rr, v, g = _rwkv_prep(rw_cur, rw_init, p, nb, tmt, bs)
    q, kg, vg, gb = _gdn_prep(qkv_raw, conv_init, ab, p, nb, tmt, bs)

    def seq_major(a):
        if time_major:
            return jnp.transpose(a.reshape(t_len, batch, a.shape[-1]), (1, 0, 2))
        return a.reshape(batch, t_len, a.shape[-1])

    def rows_like_x(a):
        if time_major:
            a = jnp.transpose(a, (1, 0, 2))
        return a.reshape(n, a.shape[-1])

    bpad = -(-batch // SCAN_SEQS) * SCAN_SEQS
    padb = lambda a: _pad_rows(a, bpad)
    wa_s, bk_s, rr_s = (_to_scan_pair(padb(seq_major(a))) for a in (wa, bk, rr))
    vvec = _to_scan_rows(padb(seq_major(v)))
    tc = min(t_len, 32)
    y_scan, s_scan = _rwkv_scan(wa_s, bk_s, rr_s, vvec, _state_to_scan(padb(wkv0)), tc)
    y_rw = rows_like_x(_from_scan_rows(y_scan, bpad)[:batch])
    wkv = _state_from_scan(s_scan, bpad)[:batch]

    t_pad = -(-t_len // chunk) * chunk
    def gdn_seq(a):
        a = seq_major(a)
        if t_pad != t_len:
            a = jnp.pad(a, ((0, 0), (0, t_pad - t_len), (0, 0)))
        return a
    o_pad, gdn = _gdn_chunks(gdn_seq(q), gdn_seq(kg), gdn_seq(vg), gdn_seq(gb), gdn0, chunk)
    o = rows_like_x(o_pad[:, :t_len])

    out = dict(xn=xn, rw_cur=rw_cur, qkv_raw=qkv_raw, wkv=wkv, gdn=gdn)
    if with_merge:
        out["h1"], out["xt"], out["mg"], out["counts"] = _merge(y_rw, rr, bk, v, g, o, z, gates, x_rows, p, min(tm, n))
    return out


def kernel(x_prompt, x_sample, state_shift, state_wkv, state_conv, state_gdn, meta_tokens, norm1_w, w_in, mu_shift, rw_w0, rw_w2, rw_a0, rw_a2, rw_g2, rw_k_k, rw_k_a, rw_r_k, rw_lnx_w, rw_lnx_b, gdn_conv_w, gdn_A_log, gdn_dt_bias, gdn_norm_w, w_oA, w_oB, w_o, norm2_w, router_g, router_g_b, router_e, router_e_b, moe_w_gate, moe_w_up, moe_w_down, norm_f_w):
    bp, sp, _ = x_prompt.shape
    bsm, ss, _ = x_sample.shape
    row = lambda a: a[0].reshape(1, -1)
    w0 = w_in[0]
    gcol = RW_COLS + 4 * GDN_WIDTH + 2 * GDN_HEADS
    w_packed = jnp.concatenate(
        [w0[:, :RW_COLS + 4 * GDN_WIDTH], w0[:, gcol:], w0[:, RW_COLS + 4 * GDN_WIDTH:gcol],
         jnp.zeros((D_MODEL, LANES - 2 * GDN_HEADS), F32)], axis=1).astype(BF16)
    lane_pad = lambda a: jnp.pad(a, ((0, 0), (0, LANES - a.shape[1])))
    p = dict(
        norm1_w=row(norm1_w), w_in=w_packed, mu_shift=row(mu_shift), rw_w0=row(rw_w0), rw_w2=rw_w2[0],
        rw_a0=row(rw_a0), rw_a2=rw_a2[0], rw_g2=rw_g2[0], rw_k_k=row(rw_k_k), rw_k_a=row(rw_k_a),
        rw_r_k=row(rw_r_k), rw_lnx_w=row(rw_lnx_w), rw_lnx_b=row(rw_lnx_b),
        ones_rw=_block_ones(RW_WIDTH, RW_HEAD_DIM), ones_gdn=_block_ones(GDN_WIDTH, GDN_HEAD_DIM),
        gdn_conv_w=gdn_conv_w[0], gdn_alog=lane_pad(row(gdn_A_log)), gdn_dt=lane_pad(row(gdn_dt_bias)),
        gdn_norm_w=row(gdn_norm_w), w_oA=w_oA[0].astype(BF16), w_oB=w_oB[0].astype(BF16),
        w_o=w_o[0].astype(BF16), norm2_w=row(norm2_w),
        router_w=lane_pad(jnp.concatenate([router_e[0], router_g[0]], axis=1)),
        router_b=lane_pad(jnp.concatenate([row(router_e_b), row(router_g_b)], axis=1)),
    )
    nf = norm_f_w.reshape(1, -1)
    dt = x_prompt.dtype

    meta = _mixer(meta_tokens.astype(dt), p, batch=1, t_len=N_META, time_major=False, tm=N_META,
                  rw_prev=jnp.zeros((1, RW_COLS), dt), conv_prev=jnp.zeros((1, GDN_CONV - 1, GDN_CONV_CH), dt),
                  wkv0=jnp.zeros((1, RW_HEADS, RW_HEAD_DIM, RW_HEAD_DIM), dt),
                  gdn0=jnp.zeros((1, GDN_HEADS, GDN_HEAD_DIM, GDN_HEAD_DIM), dt), chunk=N_META, with_merge=False)
    rep = lambda a: jnp.broadcast_to(a, (bp,) + a.shape[1:])

    prompt = _mixer(x_prompt.reshape(bp * sp, D_MODEL), p, batch=bp, t_len=sp, time_major=False, tm=256,
                    rw_prev=rep(meta["rw_cur"][N_META - 1:]), conv_prev=rep(meta["qkv_raw"][None, N_META - 3:]),
                    wkv0=rep(meta["wkv"]), gdn0=rep(meta["gdn"]), chunk=GDN_CHUNK, with_merge=True)

    xs_rows = jnp.transpose(x_sample, (1, 0, 2)).reshape(ss * bsm, D_MODEL)
    prev_s = _in_proj(state_shift[0].astype(dt), p["norm1_w"], w_packed, bsm, norm=False)[1]
    sample = _mixer(xs_rows, p, batch=bsm, t_len=ss, time_major=True, tm=256,
                    rw_prev=prev_s, conv_prev=state_conv[0].astype(dt), wkv0=state_wkv[0], gdn0=state_gdn[0],
                    chunk=SUBLANES, with_merge=True)

    experts = (moe_w_gate[0], moe_w_up[0], moe_w_down[0])
    y_p = _moe(prompt["xt"], prompt["h1"], prompt["mg"], prompt["counts"], *experts, nf, 256, 256)
    y_s = _moe(sample["xt"], sample["h1"], sample["mg"], sample["counts"], *experts, nf, 128, 256)

    y_prompt = y_p.reshape(bp, sp, D_MODEL)
    y_sample = jnp.transpose(y_s.reshape(ss, bsm, D_MODEL), (1, 0, 2))
    shift_p = prompt["xn"].reshape(bp, sp, D_MODEL)[:, -1][None]
    conv_p = prompt["qkv_raw"].reshape(bp, sp, GDN_CONV_CH)[:, sp - (GDN_CONV - 1):][None]
    shift_s = sample["xn"].reshape(ss, bsm, D_MODEL)[-1][None].astype(state_shift.dtype)
    xpad_s = jnp.concatenate([state_conv[0].astype(dt),
                              jnp.transpose(sample["qkv_raw"].reshape(ss, bsm, GDN_CONV_CH), (1, 0, 2))], axis=1)
    conv_s = xpad_s[:, ss:][None].astype(state_conv.dtype)
    return (y_prompt, y_sample, shift_p, prompt["wkv"][None], conv_p, prompt["gdn"][None],
            shift_s, sample["wkv"][None].astype(state_wkv.dtype), conv_s, sample["gdn"][None].astype(state_gdn.dtype))
```

```python
import functools
import math

import jax
import jax.numpy as jnp
from jax import lax
from jax.experimental import pallas as pl
from jax.experimental.pallas import tpu as pltpu

F32 = jnp.float32
BF16 = jnp.bfloat16
HIGHEST = lax.Precision.HIGHEST

D_MODEL = 1024
N_META = 16
RW_WIDTH = 512
RW_HEAD_DIM = 64
RW_HEADS = 8
RW_DECAY_LORA = 64
RW_AAA_LORA = 64
RW_GATE_LORA = 128
RW_COLS = 3 * RW_WIDTH + RW_DECAY_LORA + RW_AAA_LORA + RW_GATE_LORA
RW_GN_EPS = RW_HEAD_DIM * 1e-5
GDN_WIDTH = 512
GDN_HEAD_DIM = 128
GDN_HEADS = 4
GDN_CONV = 4
GDN_CONV_CH = 3 * GDN_WIDTH
GDN_CHUNK = 64
N_GROUPS = 4
EXPERTS_PER_GROUP = 8
N_EXPERTS = 32
EXPERT_FF = 512
RMS_EPS = 1e-6

LANES = 128
SUBLANES = 8
VMEM_LIMIT_BYTES = 56 * 1024 * 1024

COL_RW = 0
COL_QKV = COL_RW + RW_COLS
COL_Z = COL_QKV + GDN_CONV_CH
COL_GATE = COL_Z + GDN_WIDTH
COL_AB = COL_GATE + 2 * D_MODEL
IN_COLS_PACKED = COL_AB + LANES

SCAN_SEQS = 8
SCAN_ROWS = RW_HEAD_DIM // 2
SCAN_ROW_GROUP = 2

GDN_SEQ_BLOCK = 2
GDN_SEQ_GROUP = 2


def _cparams(*sem):
    return pltpu.CompilerParams(dimension_semantics=sem, vmem_limit_bytes=VMEM_LIMIT_BYTES)


def _const_spec(shape):
    zeros = (0,) * len(shape)
    return pl.BlockSpec(shape, lambda *_: zeros)


def _dot(a, b):
    return jnp.dot(a.astype(BF16), b.astype(BF16), preferred_element_type=F32)


def _dot_hi(a, b):
    return jnp.dot(a, b, preferred_element_type=F32, precision=HIGHEST)


def _split_bf16(a):
    hi = a.astype(BF16)
    return hi, (a - hi.astype(F32)).astype(BF16)


def _dot3(a, b):
    d = lambda x, y: jnp.dot(x, y, preferred_element_type=F32)
    return d(a[0], b[0]) + d(a[0], b[1]) + d(a[1], b[0])


def _dot_f32(a, b):
    return _dot3(_split_bf16(a), _split_bf16(b))


def _seg_sum(x, ones):
    hi, lo = _split_bf16(x)
    return jnp.dot(hi, ones, preferred_element_type=F32) + jnp.dot(lo, ones, preferred_element_type=F32)


def _dot_nt(a, b):
    dims = (((1,), (1,)), ((), ()))
    return lax.dot_general(a.astype(BF16), b.astype(BF16), dims, preferred_element_type=F32)


def _sigmoid(x):
    return 1.0 / (1.0 + jnp.exp(-x))


def _softplus(x):
    return jnp.maximum(x, 0.0) + jnp.log1p(jnp.exp(-jnp.abs(x)))


def _rmsnorm(x, w):
    ms = jnp.mean(x * x, axis=-1, keepdims=True)
    return x * lax.rsqrt(ms + RMS_EPS) * w


def _block_ones(width, seg):
    r = jnp.arange(width) // seg
    return (r[:, None] == r[None, :]).astype(BF16)


def _in_proj_kernel(x_ref, nw_ref, w_ref, xn_ref, rw_ref, qkv_ref, z_ref, gate_ref, ab_ref, *, norm):
    x = x_ref[...]
    xn = _rmsnorm(x, nw_ref[...]) if norm else x
    xn_ref[...] = xn
    xb = xn.astype(BF16)
    rw_ref[...] = jnp.dot(xb, w_ref[:, COL_RW:COL_QKV], preferred_element_type=F32)
    qkv_ref[...] = jnp.dot(xb, w_ref[:, COL_QKV:COL_Z], preferred_element_type=F32)
    z_ref[...] = jnp.dot(xb, w_ref[:, COL_Z:COL_GATE], preferred_element_type=F32)
    gate_ref[...] = jnp.dot(xb, w_ref[:, COL_GATE:COL_AB], preferred_element_type=F32)
    ab_ref[...] = jnp.dot(xb, w_ref[:, COL_AB:IN_COLS_PACKED], preferred_element_type=F32)


def _in_proj(x, norm_w, w_packed, tm, norm=True):
    n = x.shape[0]
    widths = (D_MODEL, RW_COLS, GDN_CONV_CH, GDN_WIDTH, 2 * D_MODEL, LANES)
    row = lambda w: pl.BlockSpec((tm, w), lambda i: (i, 0))
    return pl.pallas_call(
        functools.partial(_in_proj_kernel, norm=norm),
        grid=(n // tm,),
        in_specs=[row(D_MODEL), _const_spec((1, D_MODEL)), _const_spec(w_packed.shape)],
        out_specs=[row(w) for w in widths],
        out_shape=[jax.ShapeDtypeStruct((n, w), F32) for w in widths],
        compiler_params=_cparams("arbitrary"),
        name="in_proj",
    )(x, norm_w, w_packed)


def _stage_tile(ext_ref, init_ref, cur, halo, tm):
    t = pl.program_id(1)

    @pl.when(t == 0)
    def _():
        ext_ref[0:halo, :] = init_ref[0]

    @pl.when(t > 0)
    def _():
        ext_ref[0:halo, :] = ext_ref[tm:tm + halo, :]

    ext_ref[halo:halo + tm, :] = cur


def _rwkv_prep_kernel(cur_ref, init_ref, mu_ref, w0_ref, w2_ref, a0_ref, a2_ref, g2_ref, kk_ref, ka_ref,
                      ones_ref, wa_ref, bk_ref, rr_ref, v_ref, g_ref, ext_ref, *, halo, bs, tm):
    cur = cur_ref[...]
    _stage_tile(ext_ref, init_ref, cur, halo, tm)
    prev = ext_ref[halo - bs:halo - bs + tm, :]
    zc = cur + (prev - cur) * mu_ref[...]
    c0, c1, c2 = RW_WIDTH, 2 * RW_WIDTH, 3 * RW_WIDTH
    r = zc[:, 0:c0]
    k = zc[:, c0:c1]
    v = zc[:, c1:c2]
    xw = zc[:, c2:c2 + RW_DECAY_LORA]
    xa = zc[:, c2 + RW_DECAY_LORA:c2 + RW_DECAY_LORA + RW_AAA_LORA]
    xg = zc[:, c2 + RW_DECAY_LORA + RW_AAA_LORA:RW_COLS]
    w_raw = -_softplus(-(w0_ref[...] + _dot_f32(jnp.tanh(xw), w2_ref[...]))) - 0.5
    decay = jnp.exp(-jnp.exp(w_raw))
    a = _sigmoid(a0_ref[...] + _dot_f32(xa, a2_ref[...]))
    g = _dot_f32(_sigmoid(xg), g2_ref[...])
    kkr = k * kk_ref[...]
    kk = kkr * lax.rsqrt(_seg_sum(kkr * kkr, ones_ref[...]) + 1e-6)
    wa_ref[:, 0:RW_WIDTH] = decay
    wa_ref[:, RW_WIDTH:2 * RW_WIDTH] = -kk
    bk_ref[:, 0:RW_WIDTH] = kk * a
    bk_ref[:, RW_WIDTH:2 * RW_WIDTH] = k * (1.0 + (a - 1.0) * ka_ref[...])
    rr_ref[:, 0:RW_WIDTH] = r
    rr_ref[:, RW_WIDTH:2 * RW_WIDTH] = r
    v_ref[...] = v
    g_ref[...] = g


def _rwkv_prep(rw_cur, init, p, nb, tm, bs):
    n = rw_cur.shape[0]
    nt = n // (nb * tm)
    halo = init.shape[1]
    row = lambda w: pl.BlockSpec((tm, w), lambda b, t: (b * nt + t, 0))
    consts = [p["mu_shift"], p["rw_w0"], p["rw_w2"], p["rw_a0"], p["rw_a2"], p["rw_g2"], p["rw_k_k"], p["rw_k_a"],
              p["ones_rw"]]
    return pl.pallas_call(
        functools.partial(_rwkv_prep_kernel, halo=halo, bs=bs, tm=tm),
        grid=(nb, nt),
        in_specs=[row(RW_COLS), pl.BlockSpec((1, halo, RW_COLS), lambda b, t: (b, 0, 0))]
        + [_const_spec(c.shape) for c in consts],
        out_specs=[row(2 * RW_WIDTH)] * 3 + [row(RW_WIDTH)] * 2,
        out_shape=[jax.ShapeDtypeStruct((n, 2 * RW_WIDTH), F32)] * 3 + [jax.ShapeDtypeStruct((n, RW_WIDTH), F32)] * 2,
        scratch_shapes=[pltpu.VMEM((halo + tm, RW_COLS), F32)],
        compiler_params=_cparams("arbitrary", "arbitrary"),
        name="rwkv_prep",
    )(rw_cur, init, *consts)


def _sublane_allsum(x):
    x = x + pltpu.roll(x, 4, 0)
    x = x + pltpu.roll(x, 2, 0)
    return x + pltpu.roll(x, 1, 0)


def _rwkv_scan_kernel(wa_ref, bk_ref, rr_ref, v_ref, s0_ref, y_ref, sout_ref, s_ref, vec_ref, *, tc):
    c = pl.program_id(1)
    nj = RW_HEAD_DIM // SUBLANES
    low_half = lax.broadcasted_iota(jnp.int32, (RW_HEAD_DIM, LANES), 1) < LANES // 2

    @pl.when(c == 0)
    def _():
        s_ref[...] = s0_ref[0]

    def unpack(t, slot):
        for pair, src_ref in enumerate((wa_ref, bk_ref)):
            x = src_ref[0, t]
            rot = pltpu.roll(x, LANES // 2, 1)
            vec_ref[slot, 2 * pair] = jnp.where(low_half, x, rot)
            vec_ref[slot, 2 * pair + 1] = jnp.where(low_half, rot, x)

    def step(t, slot):
        def vec(which, jb):
            rows = pl.ds(jb * SUBLANES, SUBLANES)
            return rr_ref[0, t, rows, :] if which == 4 else vec_ref[slot, which, rows, :]

        for i0 in range(0, SCAN_ROWS, SCAN_ROW_GROUP):
            rows = range(i0, i0 + SCAN_ROW_GROUP)
            s = {i: [s_ref[i, jb * SUBLANES:(jb + 1) * SUBLANES, :] for jb in range(nj)] for i in rows}
            acc = {}
            for jb in range(nj):
                a = vec(1, jb)
                for i in rows:
                    acc[i] = s[i][jb] * a if jb == 0 else acc[i] + s[i][jb] * a
            sa = {i: _sublane_allsum(acc[i]) for i in rows}
            vb = {i: jnp.broadcast_to(v_ref[0, t, i:i + 1, :], (SUBLANES, LANES)) for i in rows}
            accy = {}
            for jb in range(nj):
                w, b, k, r = vec(0, jb), vec(2, jb), vec(3, jb), vec(4, jb)
                for i in rows:
                    sn = s[i][jb] * w + sa[i] * b + vb[i] * k
                    s_ref[i, jb * SUBLANES:(jb + 1) * SUBLANES, :] = sn
                    accy[i] = sn * r if jb == 0 else accy[i] + sn * r
            for i in rows:
                y_ref[0, t, i:i + 1, :] = _sublane_allsum(accy[i])[0:1, :]
        unpack(jnp.minimum(t + 1, tc - 1), 1 - slot)

    def two_steps(t2, carry):
        step(2 * t2, 0)
        step(2 * t2 + 1, 1)
        return carry

    unpack(0, 0)
    lax.fori_loop(0, tc // 2, two_steps, 0)

    @pl.when(c == pl.num_programs(1) - 1)
    def _():
        sout_ref[0] = s_ref[...]


def _rwkv_scan(wa, bk, rr, vvec, s0, tc):
    g, t = wa.shape[0], wa.shape[1]
    jspec = pl.BlockSpec((1, tc, RW_HEAD_DIM, LANES), lambda gi, c: (gi, c, 0, 0))
    return pl.pallas_call(
        functools.partial(_rwkv_scan_kernel, tc=tc),
        grid=(g, t // tc),
        in_specs=[jspec, jspec, jspec,
                  pl.BlockSpec((1, tc, SCAN_ROWS, LANES), lambda gi, c: (gi, c, 0, 0)),
                  pl.BlockSpec((1, SCAN_ROWS, RW_HEAD_DIM, LANES), lambda gi, c: (gi, 0, 0, 0))],
        out_specs=[pl.BlockSpec((1, tc, SCAN_ROWS, LANES), lambda gi, c: (gi, c, 0, 0)),
                   pl.BlockSpec((1, SCAN_ROWS, RW_HEAD_DIM, LANES), lambda gi, c: (gi, 0, 0, 0))],
        out_shape=[jax.ShapeDtypeStruct((g, t, SCAN_ROWS, LANES), F32),
                   jax.ShapeDtypeStruct((g, SCAN_ROWS, RW_HEAD_DIM, LANES), F32)],
        scratch_shapes=[pltpu.VMEM((SCAN_ROWS, RW_HEAD_DIM, LANES), F32), pltpu.VMEM((2, 4, RW_HEAD_DIM, LANES), F32)],
        compiler_params=_cparams("arbitrary", "arbitrary"),
        name="rwkv_scan",
    )(wa, bk, rr, vvec, s0)


def _to_scan_pair(x):
    b, t, _ = x.shape
    g = b // SCAN_SEQS
    x = x.reshape(g, SCAN_SEQS, t, 2, RW_HEADS, RW_HEAD_DIM)
    return jnp.transpose(x, (0, 2, 5, 3, 1, 4)).reshape(g, t, RW_HEAD_DIM, LANES)


def _to_scan_rows(x):
    b, t, _ = x.shape
    g = b // SCAN_SEQS
    x = x.reshape(g, SCAN_SEQS, t, RW_HEADS, 2, SCAN_ROWS)
    return jnp.transpose(x, (0, 2, 5, 4, 1, 3)).reshape(g, t, SCAN_ROWS, LANES)


def _from_scan_rows(y, b):
    g, t = y.shape[0], y.shape[1]
    y = y.reshape(g, t, SCAN_ROWS, 2, SCAN_SEQS, RW_HEADS)
    return jnp.transpose(y, (0, 4, 1, 5, 3, 2)).reshape(b, t, RW_WIDTH)


def _state_to_scan(s):
    b = s.shape[0]
    g = b // SCAN_SEQS
    s = s.reshape(g, SCAN_SEQS, RW_HEADS, 2, SCAN_ROWS, RW_HEAD_DIM)
    return jnp.transpose(s, (0, 4, 5, 3, 1, 2)).reshape(g, SCAN_ROWS, RW_HEAD_DIM, LANES)


def _state_from_scan(s, b):
    g = s.shape[0]
    s = s.reshape(g, SCAN_ROWS, RW_HEAD_DIM, 2, SCAN_SEQS, RW_HEADS)
    return jnp.transpose(s, (0, 4, 5, 3, 1, 2)).reshape(b, RW_HEADS, RW_HEAD_DIM, RW_HEAD_DIM)


def _gdn_prep_kernel(x_ref, init_ref, ab_ref, cw_ref, alog_ref, dt_ref, ones_ref, q_ref, k_ref, v_ref, gb_ref,
                     ext_ref, *, halo, bs, tm):
    cur = x_ref[...]
    _stage_tile(ext_ref, init_ref, cur, halo, tm)
    conv = cur * cw_ref[GDN_CONV - 1:GDN_CONV, :]
    for s in range(1, GDN_CONV):
        off = halo - s * bs
        conv = conv + ext_ref[off:off + tm, :] * cw_ref[GDN_CONV - 1 - s:GDN_CONV - s, :]
    qkv = conv * _sigmoid(conv)
    q = qkv[:, 0:GDN_WIDTH]
    k = qkv[:, GDN_WIDTH:2 * GDN_WIDTH]
    ones = ones_ref[...]
    q_ref[...] = q * lax.rsqrt(_seg_sum(q * q, ones) + 1e-6) * (GDN_HEAD_DIM ** -0.5)
    k_ref[...] = k * lax.rsqrt(_seg_sum(k * k, ones) + 1e-6)
    v_ref[...] = qkv[:, 2 * GDN_WIDTH:3 * GDN_WIDTH]
    ab = ab_ref[...]
    g = -jnp.exp(alog_ref[...]) * _softplus(ab + dt_ref[...])
    lane = lax.broadcasted_iota(jnp.int32, ab.shape, 1)
    gb_ref[...] = jnp.where(lane < GDN_HEADS, g, _sigmoid(ab))


def _gdn_prep(qkv_raw, init, ab, p, nb, tm, bs):
    n = qkv_raw.shape[0]
    nt = n // (nb * tm)
    halo = init.shape[1]
    row = lambda w: pl.BlockSpec((tm, w), lambda b, t: (b * nt + t, 0))
    consts = [p["gdn_conv_w"], p["gdn_alog"], p["gdn_dt"], p["ones_gdn"]]
    return pl.pallas_call(
        functools.partial(_gdn_prep_kernel, halo=halo, bs=bs, tm=tm),
        grid=(nb, nt),
        in_specs=[row(GDN_CONV_CH), pl.BlockSpec((1, halo, GDN_CONV_CH), lambda b, t: (b, 0, 0)), row(LANES)]
        + [_const_spec(c.shape) for c in consts],
        out_specs=[row(GDN_WIDTH)] * 3 + [row(LANES)],
        out_shape=[jax.ShapeDtypeStruct((n, GDN_WIDTH), F32)] * 3 + [jax.ShapeDtypeStruct((n, LANES), F32)],
        scratch_shapes=[pltpu.VMEM((halo + tm, GDN_CONV_CH), F32)],
        compiler_params=_cparams("arbitrary", "arbitrary"),
        name="gdn_prep",
    )(qkv_raw, init, ab, *consts)


def _unit_lower_inverses(a_list, eye, chunk):
    n = [-a for a in a_list]
    inv = [eye + x for x in n]
    for _ in range(int(math.log2(chunk)) - 1):
        ns = [_split_bf16(x) for x in n]
        n = [_dot3(x, x) for x in ns]
        ns = [_split_bf16(x) for x in n]
        inv = [x + _dot3(_split_bf16(x), y) for x, y in zip(inv, ns)]
    return inv


def _gdn_chunk_kernel(q_ref, k_ref, v_ref, gb_ref, s0_ref, o_ref, sout_ref, s_ref, *, chunk, nbb, group):
    ci = pl.program_id(1)

    @pl.when(ci == 0)
    def _():
        s_ref[...] = s0_ref[...]

    rows = lax.broadcasted_iota(jnp.int32, (chunk, chunk), 0)
    cols = lax.broadcasted_iota(jnp.int32, (chunk, chunk), 1)
    causal = rows >= cols
    strict = rows > cols
    eye = (rows == cols).astype(F32)
    tri = causal.astype(F32)

    def one_group(gi, carry):
        units = [(gi * group + j, h) for j in range(group) for h in range(GDN_HEADS)]
        lanes = lambda h: slice(h * GDN_HEAD_DIM, (h + 1) * GDN_HEAD_DIM)
        s = [s_ref[b, h] for b, h in units]
        q = [q_ref[b, :, lanes(h)] for b, h in units]
        k = [k_ref[b, :, lanes(h)] for b, h in units]
        v = [v_ref[b, :, lanes(h)] for b, h in units]
        gb = [gb_ref[gi * group + j] for j in range(group)]
        gcs = [_dot_hi(tri, x) for x in gb]
        gc = [gcs[i // GDN_HEADS][:, h:h + 1] for i, (_, h) in enumerate(units)]
        beta = [gb[i // GDN_HEADS][:, GDN_HEADS + h:GDN_HEADS + h + 1] for i, (_, h) in enumerate(units)]
        decay = [jnp.exp(jnp.where(causal, x - jnp.sum(eye * x, axis=0, keepdims=True), -jnp.inf)) for x in gc]
        kb = [x * y for x, y in zip(k, beta)]
        a = [jnp.where(strict, _dot_nt(x, y) * d, 0.0) for x, y, d in zip(kb, k, decay)]
        inv = _unit_lower_inverses(a, eye, chunk)
        egc = [jnp.exp(x) for x in gc]
        rhs = [jnp.concatenate([x * bt, y * e], axis=1) for x, bt, y, e in zip(v, beta, kb, egc)]
        sol = [_dot3(_split_bf16(x), _split_bf16(y)) for x, y in zip(inv, rhs)]
        wS = [_dot(x[:, GDN_HEAD_DIM:2 * GDN_HEAD_DIM], y) for x, y in zip(sol, s)]
        v_new = [x[:, 0:GDN_HEAD_DIM] - y for x, y in zip(sol, wS)]
        qk = [_dot_nt(x, y) * d for x, y, d in zip(q, k, decay)]
        o_state = [_dot(x * e, y) for x, e, y in zip(q, egc, s)]
        o_chunk = [_dot(x, y) for x, y in zip(qk, v_new)]
        g_last = [x[chunk - 1:chunk, :] for x in gc]
        kd_t = [(x * jnp.exp(gl - g)).T for x, gl, g in zip(k, g_last, gc)]
        s_add = [_dot(x, y) for x, y in zip(kd_t, v_new)]
        for i, (b, h) in enumerate(units):
            o_ref[b, :, lanes(h)] = o_state[i] + o_chunk[i]
            s_ref[b, h] = s[i] * jnp.exp(g_last[i]) + s_add[i]
        return carry

    if nbb == group:
        one_group(0, 0)
    else:
        lax.fori_loop(0, nbb // group, one_group, 0)

    @pl.when(ci == pl.num_programs(1) - 1)
    def _():
        sout_ref[...] = s_ref[...]


def _gdn_chunks(q, k, v, gb, s0, chunk):
    b, t_len, _ = q.shape
    nbb = min(b, GDN_SEQ_BLOCK)
    group = min(nbb, GDN_SEQ_GROUP)
    blk = lambda w: pl.BlockSpec((nbb, chunk, w), lambda bi, ci: (bi, ci, 0))
    st = pl.BlockSpec((nbb, GDN_HEADS, GDN_HEAD_DIM, GDN_HEAD_DIM), lambda bi, ci: (bi, 0, 0, 0))
    return pl.pallas_call(
        functools.partial(_gdn_chunk_kernel, chunk=chunk, nbb=nbb, group=group),
        grid=(b // nbb, t_len // chunk),
        in_specs=[blk(GDN_WIDTH)] * 3 + [blk(LANES), st],
        out_specs=[blk(GDN_WIDTH), st],
        out_shape=[jax.ShapeDtypeStruct(q.shape, F32), jax.ShapeDtypeStruct(s0.shape, F32)],
        scratch_shapes=[pltpu.VMEM((nbb, GDN_HEADS, GDN_HEAD_DIM, GDN_HEAD_DIM), F32)],
        compiler_params=_cparams("arbitrary", "arbitrary"),
        name="gdn_chunks",
    )(q, k, v, gb, s0)


def _route(logits, count_ref):
    lane = lax.broadcasted_iota(jnp.int32, logits.shape, 1)
    neg = -jnp.inf
    is_group = (lane >= N_EXPERTS) & (lane < N_EXPERTS + N_GROUPS)
    gl = jnp.where(is_group, logits, neg)
    gmax = jnp.max(gl, axis=-1, keepdims=True)
    gp = 1.0 / jnp.sum(jnp.exp(gl - gmax), axis=-1, keepdims=True)
    gidx = jnp.min(jnp.where(gl == gmax, lane, LANES), axis=-1, keepdims=True) - N_EXPERTS
    lo = gidx * EXPERTS_PER_GROUP
    el = jnp.where((lane >= lo) & (lane < lo + EXPERTS_PER_GROUP), logits, neg)
    m1 = jnp.max(el, axis=-1, keepdims=True)
    i1 = jnp.min(jnp.where(el == m1, lane, LANES), axis=-1, keepdims=True)
    el2 = jnp.where(lane == i1, neg, el)
    m2 = jnp.max(el2, axis=-1, keepdims=True)
    i2 = jnp.min(jnp.where(el2 == m2, lane, LANES), axis=-1, keepdims=True)
    e2 = jnp.exp(m2 - m1)
    den = 1.0 + e2
    tm = logits.shape[0]
    hit1 = lane == i1
    hit2 = lane == i2
    chosen = jnp.where(hit1 | hit2, 1.0, 0.0)
    before = lax.broadcasted_iota(jnp.int32, (tm, tm), 0) > lax.broadcasted_iota(jnp.int32, (tm, tm), 1)
    seen = jnp.dot(before.astype(BF16), chosen.astype(BF16), preferred_element_type=F32) + count_ref[...]
    rank1 = jnp.sum(jnp.where(hit1, seen, 0.0), axis=-1, keepdims=True)
    rank2 = jnp.sum(jnp.where(hit2, seen, 0.0), axis=-1, keepdims=True)
    count_ref[...] += jnp.sum(chosen, axis=0, keepdims=True)
    fields = (i1.astype(F32), i2.astype(F32), gp / den, gp * e2 / den, rank1, rank2)
    out = jnp.zeros(logits.shape, F32)
    for j, f in enumerate(fields):
        out = jnp.where(lane == j, f, out)
    return out


def _merge_kernel(y_ref, r_ref, k_ref, v_ref, g_ref, o_ref, z_ref, gate_ref, h_ref, lnw_ref, lnb_ref, rk_ref,
                  ones_ref, gnw_ref, woa_ref, wob_ref, wo_ref, n2_ref, rw_ref, rb_ref, h1_ref, xt_ref, mg_ref, cnt_ref):
    @pl.when(pl.program_id(0) == 0)
    def _():
        cnt_ref[...] = jnp.zeros_like(cnt_ref)

    ones = ones_ref[...]
    inv_n = 1.0 / RW_HEAD_DIM
    y = y_ref[...]
    v = v_ref[...]
    mu = _seg_sum(y, ones) * inv_n
    yc = y - mu
    var = _seg_sum(yc * yc, ones) * inv_n
    ya = yc * lax.rsqrt(var + RW_GN_EPS) * lnw_ref[...] + lnb_ref[...]
    bonus = _seg_sum(r_ref[...] * k_ref[...] * rk_ref[...], ones) * v
    ya = (ya + bonus) * g_ref[...]

    z = z_ref[...]
    gnw = gnw_ref[...]
    yb_parts = []
    for h in range(GDN_HEADS):
        hs = slice(h * GDN_HEAD_DIM, (h + 1) * GDN_HEAD_DIM)
        yb_parts.append(_rmsnorm(o_ref[:, hs], gnw))
    yb = jnp.concatenate(yb_parts, axis=1) * (z * _sigmoid(z))

    gates = gate_ref[...]
    merged = _sigmoid(gates[:, 0:D_MODEL]) * _dot(ya, woa_ref[...]) \
        + _sigmoid(gates[:, D_MODEL:2 * D_MODEL]) * _dot(yb, wob_ref[...])
    h1 = h_ref[...] + _dot(merged, wo_ref[...])
    h1_ref[...] = h1
    xt = _rmsnorm(h1, n2_ref[...])
    xt_ref[...] = xt
    mg_ref[...] = _route(_dot_f32(xt, rw_ref[...]) + rb_ref[...], cnt_ref)


def _merge(y, r, k, v, g, o, z, gates, h, p, tm):
    n = y.shape[0]
    row = lambda w: pl.BlockSpec((tm, w), lambda i: (i, 0))
    consts = [p["rw_lnx_w"], p["rw_lnx_b"], p["rw_r_k"], p["ones_rw"], p["gdn_norm_w"], p["w_oA"], p["w_oB"],
              p["w_o"], p["norm2_w"], p["router_w"], p["router_b"]]
    return pl.pallas_call(
        _merge_kernel,
        grid=(n // tm,),
        in_specs=[row(RW_WIDTH), row(RW_WIDTH), pl.BlockSpec((tm, RW_WIDTH), lambda i: (i, 1))] + [row(RW_WIDTH)] * 4
        + [row(2 * D_MODEL), row(D_MODEL)] + [_const_spec(c.shape) for c in consts],
        out_specs=[row(D_MODEL), row(D_MODEL), row(LANES), _const_spec((1, LANES))],
        out_shape=[jax.ShapeDtypeStruct((n, D_MODEL), F32), jax.ShapeDtypeStruct((n, D_MODEL), F32),
                   jax.ShapeDtypeStruct((n, LANES), F32), jax.ShapeDtypeStruct((1, LANES), F32)],
        compiler_params=_cparams("arbitrary"),
        name="merge",
    )(y, r, k, v, g, o, z, gates, h, *consts)


def _start_row_gather(src_hbm, idx_ref, dst_ref, sem, n, priority):
    def body(r, carry):
        pltpu.make_async_copy(src_hbm.at[pl.ds(idx_ref[0, 0, r], 1)], dst_ref.at[pl.ds(r, 1)], sem).start(priority)
        return carry

    lax.fori_loop(0, n, body, 0, unroll=8)


def _wait_row_gather(src_hbm, dst_ref, sem, n):
    def body(r, carry):
        pltpu.make_async_copy(src_hbm.at[pl.ds(0, 1)], dst_ref.at[pl.ds(r, 1)], sem).wait()
        return carry

    lax.fori_loop(0, n, body, 0, unroll=8)


def _moe_dispatch_kernel(p1_ref, p2_ref, x_ref, zeros_hbm, xs_hbm, xbuf, sems, *, tm):
    del zeros_hbm
    i = pl.program_id(0)
    nt = pl.num_programs(0)
    slot = i % 2

    def wait_tile(s):
        def body(r, carry):
            pltpu.make_async_copy(xbuf.at[s, pl.ds(0, 1)], xs_hbm.at[pl.ds(0, 1)], sems.at[s]).wait()
            return carry

        lax.fori_loop(0, 2 * tm, body, 0, unroll=8)

    @pl.when(i >= 2)
    def _():
        wait_tile(slot)

    xbuf[slot] = x_ref[...]

    def issue(r, carry):
        src = xbuf.at[slot, pl.ds(r, 1)]
        pltpu.make_async_copy(src, xs_hbm.at[pl.ds(p1_ref[0, 0, r], 1)], sems.at[slot]).start(0)
        pltpu.make_async_copy(src, xs_hbm.at[pl.ds(p2_ref[0, 0, r], 1)], sems.at[slot]).start(1)
        return carry

    lax.fori_loop(0, tm, issue, 0, unroll=8)

    @pl.when(i == nt - 1)
    def _():
        wait_tile(slot)

        @pl.when(nt > 1)
        def _():
            wait_tile(1 - slot)


def _moe_dispatch(xt, pos1, pos2, n_sorted, tm):
    n = xt.shape[0]
    idx = pl.BlockSpec((1, 1, tm), lambda i: (i, 0, 0), memory_space=pltpu.SMEM)
    return pl.pallas_call(
        functools.partial(_moe_dispatch_kernel, tm=tm),
        grid=(n // tm,),
        in_specs=[idx, idx, pl.BlockSpec((tm, D_MODEL), lambda i: (i, 0)), pl.BlockSpec(memory_space=pl.ANY)],
        out_specs=pl.BlockSpec(memory_space=pl.ANY),
        out_shape=jax.ShapeDtypeStruct((n_sorted, D_MODEL), F32),
        scratch_shapes=[pltpu.VMEM((2, tm, D_MODEL), F32), pltpu.SemaphoreType.DMA((2,))],
        input_output_aliases={3: 0},
        compiler_params=_cparams("arbitrary"),
        name="moe_dispatch",
    )(pos1, pos2, xt, jnp.zeros((n_sorted, D_MODEL), F32))


def _moe_experts_kernel(te_ref, tv_ref, x_ref, wg_ref, wu_ref, wd_ref, y_ref):
    i = pl.program_id(0)

    @pl.when(tv_ref[i] == 1)
    def _():
        xb = x_ref[...].astype(BF16)
        hg = jnp.dot(xb, wg_ref[0].astype(BF16), preferred_element_type=F32)
        hu = jnp.dot(xb, wu_ref[0].astype(BF16), preferred_element_type=F32)
        hid = hg * _sigmoid(hg) * hu
        y_ref[...] = jnp.dot(hid.astype(BF16), wd_ref[0].astype(BF16), preferred_element_type=F32)

    @pl.when(tv_ref[i] == 0)
    def _():
        y_ref[...] = jnp.zeros_like(y_ref)


def _moe_experts(xs, tile_expert, tile_valid, wg, wu, wd, tm):
    nt = xs.shape[0] // tm
    wspec = lambda shape: pl.BlockSpec((1,) + shape, lambda i, te, tv: (te[i], 0, 0))
    rows = pl.BlockSpec((tm, D_MODEL), lambda i, te, tv: (i, 0))
    return pl.pallas_call(
        _moe_experts_kernel,
        grid_spec=pltpu.PrefetchScalarGridSpec(
            num_scalar_prefetch=2,
            grid=(nt,),
            in_specs=[rows, wspec((D_MODEL, EXPERT_FF)), wspec((D_MODEL, EXPERT_FF)), wspec((EXPERT_FF, D_MODEL))],
            out_specs=rows,
        ),
        out_shape=jax.ShapeDtypeStruct(xs.shape, F32),
        compiler_params=_cparams("arbitrary"),
        name="moe_experts",
    )(tile_expert, tile_valid, xs, wg, wu, wd)


def _moe_combine_kernel(cur1_ref, cur2_ref, nxt1_ref, nxt2_ref, ys_hbm, h1_ref, mg_ref, nf_ref, o_ref, buf1, buf2,
                        sems, *, tm):
    i = pl.program_id(0)
    nt = pl.num_programs(0)
    slot = i % 2

    @pl.when(i == 0)
    def _():
        _start_row_gather(ys_hbm, cur1_ref, buf1.at[0], sems.at[0, 0], tm, 0)
        _start_row_gather(ys_hbm, cur2_ref, buf2.at[0], sems.at[1, 0], tm, 1)

    @pl.when(i + 1 < nt)
    def _():
        _start_row_gather(ys_hbm, nxt1_ref, buf1.at[1 - slot], sems.at[0, 1 - slot], tm, 0)
        _start_row_gather(ys_hbm, nxt2_ref, buf2.at[1 - slot], sems.at[1, 1 - slot], tm, 1)

    _wait_row_gather(ys_hbm, buf1.at[slot], sems.at[0, slot], tm)
    _wait_row_gather(ys_hbm, buf2.at[slot], sems.at[1, slot], tm)
    mg = mg_ref[...]
    moe = mg[:, 2:3] * buf1[slot] + mg[:, 3:4] * buf2[slot]
    o_ref[...] = _rmsnorm(h1_ref[...] + moe, nf_ref[...])


def _moe_combine(ys, pos1, pos2, h1, mg, nf, tm):
    n = h1.shape[0]
    nt = n // tm
    cur = pl.BlockSpec((1, 1, tm), lambda i: (i, 0, 0), memory_space=pltpu.SMEM)
    nxt = pl.BlockSpec((1, 1, tm), lambda i: (jnp.minimum(i + 1, nt - 1), 0, 0), memory_space=pltpu.SMEM)
    row = lambda w: pl.BlockSpec((tm, w), lambda i: (i, 0))
    return pl.pallas_call(
        functools.partial(_moe_combine_kernel, tm=tm),
        grid=(nt,),
        in_specs=[cur, cur, nxt, nxt, pl.BlockSpec(memory_space=pl.ANY), row(D_MODEL), row(LANES),
                  _const_spec((1, D_MODEL))],
        out_specs=row(D_MODEL),
        out_shape=jax.ShapeDtypeStruct((n, D_MODEL), F32),
        scratch_shapes=[pltpu.VMEM((2, tm, D_MODEL), F32), pltpu.VMEM((2, tm, D_MODEL), F32),
                        pltpu.SemaphoreType.DMA((2, 2))],
        compiler_params=_cparams("arbitrary"),
        name="moe_combine",
    )(pos1, pos2, pos1, pos2, ys, h1, mg, nf)


def _moe(xt, h1, mg, counts, wg, wu, wd, nf, tm_expert, tm_token):
    n = xt.shape[0]
    i32 = jnp.int32
    expert = jnp.arange(N_EXPERTS, dtype=i32)
    cnt = counts[0, :N_EXPERTS].astype(i32)
    padded = (cnt + tm_expert - 1) // tm_expert * tm_expert
    ends = jnp.sum(jnp.where(expert[:, None] >= expert[None, :], padded[None, :], 0), axis=1)
    starts = ends - padded
    start_of = lambda e: jnp.sum(jnp.where(e[:, None] == expert[None, :], starts[None, :], 0), axis=1)
    pos1 = start_of(mg[:, 0].astype(i32)) + mg[:, 4].astype(i32)
    pos2 = start_of(mg[:, 1].astype(i32)) + mg[:, 5].astype(i32)
    nt = (2 * n + N_EXPERTS * (tm_expert - 1)) // tm_expert + 1
    tile_start = jnp.arange(nt, dtype=i32) * tm_expert
    tile_valid = (tile_start < jnp.sum(padded)).astype(i32)
    tile_expert = jnp.sum((tile_start[:, None] >= ends[None, :]).astype(i32), axis=1)
    last_used = jnp.max(jnp.where(padded > 0, expert, 0))
    tile_expert = jnp.where(tile_valid == 1, tile_expert, last_used)
    shape = (n // tm_token, 1, tm_token)
    pos1, pos2 = pos1.reshape(shape), pos2.reshape(shape)
    xs = _moe_dispatch(xt, pos1, pos2, nt * tm_expert, tm_token)
    ys = _moe_experts(xs, tile_expert, tile_valid, wg, wu, wd, tm_expert)
    return _moe_combine(ys, pos1, pos2, h1, mg, nf, tm_token)


def _pad_rows(x, rows):
    return jnp.pad(x, ((0, rows - x.shape[0]),) + ((0, 0),) * (x.ndim - 1))


def _mixer(x_rows, p, *, batch, t_len, time_major, tm, rw_prev, conv_prev, wkv0, gdn0, chunk, with_merge):
    n = batch * t_len
    xn, rw_cur, qkv_raw, z, gates, ab = _in_proj(x_rows, p["norm1_w"], p["w_in"], min(tm, n))

    if time_major:
        nb, bs, tmt = 1, batch, n
        rw_init = rw_prev[None]
        conv_init = jnp.transpose(conv_prev, (1, 0, 2)).reshape(1, (GDN_CONV - 1) * batch, GDN_CONV_CH)
    else:
        nb, bs, tmt = batch, 1, min(tm, t_len)
        rw_init = jnp.pad(rw_prev[:, None, :], ((0, 0), (SUBLANES - 1, 0), (0, 0)))
        conv_init = jnp.pad(conv_prev, ((0, 0), (SUBLANES - (GDN_CONV - 1), 0), (0, 0)))

    wa, bk, rr, v, g = _rwkv_prep(rw_cur, rw_init, p, nb, tmt, bs)
    q, kg, vg, gb = _gdn_prep(qkv_raw, conv_init, ab, p, nb, tmt, bs)

    def seq_major(a):
        if time_major:
            return jnp.transpose(a.reshape(t_len, batch, a.shape[-1]), (1, 0, 2))
        return a.reshape(batch, t_len, a.shape[-1])

    def rows_like_x(a):
        if time_major:
            a = jnp.transpose(a, (1, 0, 2))
        return a.reshape(n, a.shape[-1])

    t_pad = -(-t_len // chunk) * chunk
    def gdn_seq(a):
        a = seq_major(a)
        if t_pad != t_len:
            a = jnp.pad(a, ((0, 0), (0, t_pad - t_len), (0, 0)))
        return a
    o_pad, gdn = _gdn_chunks(gdn_seq(q), gdn_seq(kg), gdn_seq(vg), gdn_seq(gb), gdn0, chunk)

    bpad = -(-batch // SCAN_SEQS) * SCAN_SEQS
    padb = lambda a: _pad_rows(a, bpad)
    scan_in = tuple(_to_scan_pair(padb(seq_major(a))) for a in (wa, bk, rr)) + (_to_scan_rows(padb(seq_major(v))),)
    scan_in, o_pad = lax.optimization_barrier((scan_in, o_pad))
    tc = min(t_len, 32)
    y_scan, s_scan = _rwkv_scan(*scan_in, _state_to_scan(padb(wkv0)), tc)
    y_rw = rows_like_x(_from_scan_rows(y_scan, bpad)[:batch])
    wkv = _state_from_scan(s_scan, bpad)[:batch]
    o = rows_like_x(o_pad[:, :t_len])

    out = dict(xn=xn, rw_cur=rw_cur, qkv_raw=qkv_raw, wkv=wkv, gdn=gdn)
    if with_merge:
        out["h1"], out["xt"], out["mg"], out["counts"] = _merge(y_rw, rr, bk, v, g, o, z, gates, x_rows, p, min(tm, n))
    return out


def kernel(x_prompt, x_sample, state_shift, state_wkv, state_conv, state_gdn, meta_tokens, norm1_w, w_in, mu_shift, rw_w0, rw_w2, rw_a0, rw_a2, rw_g2, rw_k_k, rw_k_a, rw_r_k, rw_lnx_w, rw_lnx_b, gdn_conv_w, gdn_A_log, gdn_dt_bias, gdn_norm_w, w_oA, w_oB, w_o, norm2_w, router_g, router_g_b, router_e, router_e_b, moe_w_gate, moe_w_up, moe_w_down, norm_f_w):
    bp, sp, _ = x_prompt.shape
    bsm, ss, _ = x_sample.shape
    row = lambda a: a[0].reshape(1, -1)
    w0 = w_in[0]
    gcol = RW_COLS + 4 * GDN_WIDTH + 2 * GDN_HEADS
    w_packed = jnp.concatenate(
        [w0[:, :RW_COLS + 4 * GDN_WIDTH], w0[:, gcol:], w0[:, RW_COLS + 4 * GDN_WIDTH:gcol],
         jnp.zeros((D_MODEL, LANES - 2 * GDN_HEADS), F32)], axis=1).astype(BF16)
    lane_pad = lambda a: jnp.pad(a, ((0, 0), (0, LANES - a.shape[1])))
    p = dict(
        norm1_w=row(norm1_w), w_in=w_packed, mu_shift=row(mu_shift), rw_w0=row(rw_w0), rw_w2=rw_w2[0],
        rw_a0=row(rw_a0), rw_a2=rw_a2[0], rw_g2=rw_g2[0], rw_k_k=row(rw_k_k), rw_k_a=row(rw_k_a),
        rw_r_k=row(rw_r_k), rw_lnx_w=row(rw_lnx_w), rw_lnx_b=row(rw_lnx_b),
        ones_rw=_block_ones(RW_WIDTH, RW_HEAD_DIM), ones_gdn=_block_ones(GDN_WIDTH, GDN_HEAD_DIM),
        gdn_conv_w=gdn_conv_w[0], gdn_alog=lane_pad(row(gdn_A_log)), gdn_dt=lane_pad(row(gdn_dt_bias)),
        gdn_norm_w=row(gdn_norm_w), w_oA=w_oA[0].astype(BF16), w_oB=w_oB[0].astype(BF16),
        w_o=w_o[0].astype(BF16), norm2_w=row(norm2_w),
        router_w=lane_pad(jnp.concatenate([router_e[0], router_g[0]], axis=1)),
        router_b=lane_pad(jnp.concatenate([row(router_e_b), row(router_g_b)], axis=1)),
    )
    nf = norm_f_w.reshape(1, -1)
    dt = x_prompt.dtype

    meta = _mixer(meta_tokens.astype(dt), p, batch=1, t_len=N_META, time_major=False, tm=N_META,
                  rw_prev=jnp.zeros((1, RW_COLS), dt), conv_prev=jnp.zeros((1, GDN_CONV - 1, GDN_CONV_CH), dt),
                  wkv0=jnp.zeros((1, RW_HEADS, RW_HEAD_DIM, RW_HEAD_DIM), dt),
                  gdn0=jnp.zeros((1, GDN_HEADS, GDN_HEAD_DIM, GDN_HEAD_DIM), dt), chunk=N_META, with_merge=False)
    rep = lambda a: jnp.broadcast_to(a, (bp,) + a.shape[1:])

    prompt = _mixer(x_prompt.reshape(bp * sp, D_MODEL), p, batch=bp, t_len=sp, time_major=False, tm=256,
                    rw_prev=rep(meta["rw_cur"][N_META - 1:]), conv_prev=rep(meta["qkv_raw"][None, N_META - 3:]),
                    wkv0=rep(meta["wkv"]), gdn0=rep(meta["gdn"]), chunk=GDN_CHUNK, with_merge=True)

    xs_rows = jnp.transpose(x_sample, (1, 0, 2)).reshape(ss * bsm, D_MODEL)
    prev_s = _in_proj(state_shift[0].astype(dt), p["norm1_w"], w_packed, bsm, norm=False)[1]
    sample = _mixer(xs_rows, p, batch=bsm, t_len=ss, time_major=True, tm=256,
                    rw_prev=prev_s, conv_prev=state_conv[0].astype(dt), wkv0=state_wkv[0], gdn0=state_gdn[0],
                    chunk=SUBLANES, with_merge=True)

    experts = (moe_w_gate[0], moe_w_up[0], moe_w_down[0])
    y_p = _moe(prompt["xt"], prompt["h1"], prompt["mg"], prompt["counts"], *experts, nf, 256, 256)
    y_s = _moe(sample["xt"], sample["h1"], sample["mg"], sample["counts"], *experts, nf, 128, 256)

    y_prompt = y_p.reshape(bp, sp, D_MODEL)
    y_sample = jnp.transpose(y_s.reshape(ss, bsm, D_MODEL), (1, 0, 2))
    shift_p = prompt["xn"].reshape(bp, sp, D_MODEL)[:, -1][None]
    conv_p = prompt["qkv_raw"].reshape(bp, sp, GDN_CONV_CH)[:, sp - (GDN_CONV - 1):][None]
    shift_s = sample["xn"].reshape(ss, bsm, D_MODEL)[-1][None].astype(state_shift.dtype)
    xpad_s = jnp.concatenate([state_conv[0].astype(dt),
                              jnp.transpose(sample["qkv_raw"].reshape(ss, bsm, GDN_CONV_CH), (1, 0, 2))], axis=1)
    conv_s = xpad_s[:, ss:][None].astype(state_conv.dtype)
    return (y_prompt, y_sample, shift_p, prompt["wkv"][None], conv_p, prompt["gdn"][None],
            shift_s, sample["wkv"][None].astype(state_wkv.dtype), conv_s, sample["gdn"][None].astype(state_gdn.dtype))
```

```python
import functools
import math

import jax
import jax.numpy as jnp
from jax import lax
from jax.experimental import pallas as pl
from jax.experimental.pallas import tpu as pltpu

F32 = jnp.float32
BF16 = jnp.bfloat16
HIGHEST = lax.Precision.HIGHEST

D_MODEL = 1024
N_META = 16
RW_WIDTH = 512
RW_HEAD_DIM = 64
RW_HEADS = 8
RW_DECAY_LORA = 64
RW_AAA_LORA = 64
RW_GATE_LORA = 128
RW_COLS = 3 * RW_WIDTH + RW_DECAY_LORA + RW_AAA_LORA + RW_GATE_LORA
RW_GN_EPS = RW_HEAD_DIM * 1e-5
GDN_WIDTH = 512
GDN_HEAD_DIM = 128
GDN_HEADS = 4
GDN_CONV = 4
GDN_CONV_CH = 3 * GDN_WIDTH
GDN_CHUNK = 64
N_GROUPS = 4
EXPERTS_PER_GROUP = 8
N_EXPERTS = 32
EXPERT_FF = 512
RMS_EPS = 1e-6

LANES = 128
SUBLANES = 8
VMEM_LIMIT_BYTES = 56 * 1024 * 1024

COL_RW = 0
COL_QKV = COL_RW + RW_COLS
COL_Z = COL_QKV + GDN_CONV_CH
COL_GATE = COL_Z + GDN_WIDTH
COL_AB = COL_GATE + 2 * D_MODEL
IN_COLS_PACKED = COL_AB + LANES

SCAN_SEQS = 8
SCAN_ROWS = RW_HEAD_DIM // 2
SCAN_ROW_GROUP = 2

GDN_SEQ_BLOCK = 2
GDN_SEQ_GROUP = 2


def _cparams(*sem):
    return pltpu.CompilerParams(dimension_semantics=sem, vmem_limit_bytes=VMEM_LIMIT_BYTES)


def _const_spec(shape):
    zeros = (0,) * len(shape)
    return pl.BlockSpec(shape, lambda *_: zeros)


def _dot(a, b):
    return jnp.dot(a.astype(BF16), b.astype(BF16), preferred_element_type=F32)


def _dot_hi(a, b):
    return jnp.dot(a, b, preferred_element_type=F32, precision=HIGHEST)


def _split_bf16(a):
    hi = a.astype(BF16)
    return hi, (a - hi.astype(F32)).astype(BF16)


def _dot3(a, b):
    d = lambda x, y: jnp.dot(x, y, preferred_element_type=F32)
    return d(a[0], b[0]) + d(a[0], b[1]) + d(a[1], b[0])


def _dot_f32(a, b):
    return _dot3(_split_bf16(a), _split_bf16(b))


def _seg_sum(x, ones):
    hi, lo = _split_bf16(x)
    return jnp.dot(hi, ones, preferred_element_type=F32) + jnp.dot(lo, ones, preferred_element_type=F32)


def _dot_nt(a, b):
    dims = (((1,), (1,)), ((), ()))
    return lax.dot_general(a.astype(BF16), b.astype(BF16), dims, preferred_element_type=F32)


def _sigmoid(x):
    return 1.0 / (1.0 + jnp.exp(-x))


def _softplus(x):
    return jnp.maximum(x, 0.0) + jnp.log1p(jnp.exp(-jnp.abs(x)))


def _rmsnorm(x, w):
    ms = jnp.mean(x * x, axis=-1, keepdims=True)
    return x * lax.rsqrt(ms + RMS_EPS) * w


def _block_ones(width, seg):
    r = jnp.arange(width) // seg
    return (r[:, None] == r[None, :]).astype(BF16)


def _in_proj_kernel(x_ref, nw_ref, w_ref, xn_ref, rw_ref, qkv_ref, z_ref, gate_ref, ab_ref, *, norm):
    x = x_ref[...]
    xn = _rmsnorm(x, nw_ref[...]) if norm else x
    xn_ref[...] = xn
    xb = xn.astype(BF16)
    rw_ref[...] = jnp.dot(xb, w_ref[:, COL_RW:COL_QKV], preferred_element_type=F32)
    qkv_ref[...] = jnp.dot(xb, w_ref[:, COL_QKV:COL_Z], preferred_element_type=F32)
    z_ref[...] = jnp.dot(xb, w_ref[:, COL_Z:COL_GATE], preferred_element_type=F32)
    gate_ref[...] = jnp.dot(xb, w_ref[:, COL_GATE:COL_AB], preferred_element_type=F32)
    ab_ref[...] = jnp.dot(xb, w_ref[:, COL_AB:IN_COLS_PACKED], preferred_element_type=F32)


def _in_proj(x, norm_w, w_packed, tm, norm=True):
    n = x.shape[0]
    widths = (D_MODEL, RW_COLS, GDN_CONV_CH, GDN_WIDTH, 2 * D_MODEL, LANES)
    row = lambda w: pl.BlockSpec((tm, w), lambda i: (i, 0))
    return pl.pallas_call(
        functools.partial(_in_proj_kernel, norm=norm),
        grid=(n // tm,),
        in_specs=[row(D_MODEL), _const_spec((1, D_MODEL)), _const_spec(w_packed.shape)],
        out_specs=[row(w) for w in widths],
        out_shape=[jax.ShapeDtypeStruct((n, w), F32) for w in widths],
        compiler_params=_cparams("arbitrary"),
        name="in_proj",
    )(x, norm_w, w_packed)


def _stage_tile(ext_ref, init_ref, cur, halo, tm):
    t = pl.program_id(1)

    @pl.when(t == 0)
    def _():
        ext_ref[0:halo, :] = init_ref[0]

    @pl.when(t > 0)
    def _():
        ext_ref[0:halo, :] = ext_ref[tm:tm + halo, :]

    ext_ref[halo:halo + tm, :] = cur


def _rwkv_prep_kernel(cur_ref, init_ref, mu_ref, w0_ref, w2_ref, a0_ref, a2_ref, g2_ref, kk_ref, ka_ref,
                      ones_ref, wa_ref, bk_ref, rr_ref, v_ref, g_ref, ext_ref, *, halo, bs, tm):
    cur = cur_ref[...]
    _stage_tile(ext_ref, init_ref, cur, halo, tm)
    prev = ext_ref[halo - bs:halo - bs + tm, :]
    zc = cur + (prev - cur) * mu_ref[...]
    c0, c1, c2 = RW_WIDTH, 2 * RW_WIDTH, 3 * RW_WIDTH
    r = zc[:, 0:c0]
    k = zc[:, c0:c1]
    v = zc[:, c1:c2]
    xw = zc[:, c2:c2 + RW_DECAY_LORA]
    xa = zc[:, c2 + RW_DECAY_LORA:c2 + RW_DECAY_LORA + RW_AAA_LORA]
    xg = zc[:, c2 + RW_DECAY_LORA + RW_AAA_LORA:RW_COLS]
    w_raw = -_softplus(-(w0_ref[...] + _dot_f32(jnp.tanh(xw), w2_ref[...]))) - 0.5
    decay = jnp.exp(-jnp.exp(w_raw))
    a = _sigmoid(a0_ref[...] + _dot_f32(xa, a2_ref[...]))
    g = _dot_f32(_sigmoid(xg), g2_ref[...])
    kkr = k * kk_ref[...]
    kk = kkr * lax.rsqrt(_seg_sum(kkr * kkr, ones_ref[...]) + 1e-6)
    wa_ref[:, 0:RW_WIDTH] = decay
    wa_ref[:, RW_WIDTH:2 * RW_WIDTH] = -kk
    bk_ref[:, 0:RW_WIDTH] = kk * a
    bk_ref[:, RW_WIDTH:2 * RW_WIDTH] = k * (1.0 + (a - 1.0) * ka_ref[...])
    rr_ref[:, 0:RW_WIDTH] = r
    rr_ref[:, RW_WIDTH:2 * RW_WIDTH] = r
    v_ref[...] = v
    g_ref[...] = g


def _rwkv_prep(rw_cur, init, p, nb, tm, bs, wide):
    n = rw_cur.shape[0]
    nt = n // (nb * tm)
    halo = init.shape[1]
    row = lambda w: pl.BlockSpec((tm, w), lambda b, t: (b * nt + t, 0))
    if wide:
        out_row = lambda w: pl.BlockSpec((tm, w), lambda b, t: (t, b))
        out_shape = lambda w: jax.ShapeDtypeStruct((n // nb, nb * w), F32)
    else:
        out_row = row
        out_shape = lambda w: jax.ShapeDtypeStruct((n, w), F32)
    consts = [p["mu_shift"], p["rw_w0"], p["rw_w2"], p["rw_a0"], p["rw_a2"], p["rw_g2"], p["rw_k_k"], p["rw_k_a"],
              p["ones_rw"]]
    return pl.pallas_call(
        functools.partial(_rwkv_prep_kernel, halo=halo, bs=bs, tm=tm),
        grid=(nb, nt),
        in_specs=[row(RW_COLS), pl.BlockSpec((1, halo, RW_COLS), lambda b, t: (b, 0, 0))]
        + [_const_spec(c.shape) for c in consts],
        out_specs=[out_row(2 * RW_WIDTH)] * 3 + [out_row(RW_WIDTH), row(RW_WIDTH)],
        out_shape=[out_shape(2 * RW_WIDTH)] * 3 + [out_shape(RW_WIDTH), jax.ShapeDtypeStruct((n, RW_WIDTH), F32)],
        scratch_shapes=[pltpu.VMEM((halo + tm, RW_COLS), F32)],
        compiler_params=_cparams("arbitrary", "arbitrary"),
        name="rwkv_prep",
    )(rw_cur, init, *consts)


def _sublane_allsum(x):
    x = x + pltpu.roll(x, 4, 0)
    x = x + pltpu.roll(x, 2, 0)
    return x + pltpu.roll(x, 1, 0)


def _rwkv_scan_kernel(wa_ref, bk_ref, rr_ref, v_ref, s0_ref, y_ref, sout_ref, s_ref, vec_ref, *, tc):
    c = pl.program_id(1)
    nj = RW_HEAD_DIM // SUBLANES
    first = (lax.broadcasted_iota(jnp.int32, (RW_HEAD_DIM, LANES), 1) // RW_HEADS) % 2 == 0

    @pl.when(c == 0)
    def _():
        s_ref[...] = s0_ref[0]

    def unpack(t, slot):
        for pair, src_ref in enumerate((wa_ref, bk_ref)):
            x = src_ref[0, t]
            vec_ref[slot, 2 * pair] = jnp.where(first, x, pltpu.roll(x, RW_HEADS, 1))
            vec_ref[slot, 2 * pair + 1] = jnp.where(first, pltpu.roll(x, LANES - RW_HEADS, 1), x)

    def step(t, slot):
        def vec(which, jb):
            rows = pl.ds(jb * SUBLANES, SUBLANES)
            return rr_ref[0, t, rows, :] if which == 4 else vec_ref[slot, which, rows, :]

        for i0 in range(0, SCAN_ROWS, SCAN_ROW_GROUP):
            rows = range(i0, i0 + SCAN_ROW_GROUP)
            s = {i: [s_ref[i, jb * SUBLANES:(jb + 1) * SUBLANES, :] for jb in range(nj)] for i in rows}
            acc = {}
            for jb in range(nj):
                a = vec(1, jb)
                for i in rows:
                    acc[i] = s[i][jb] * a if jb == 0 else acc[i] + s[i][jb] * a
            sa = {i: _sublane_allsum(acc[i]) for i in rows}
            vb = {i: jnp.broadcast_to(v_ref[0, t, i:i + 1, :], (SUBLANES, LANES)) for i in rows}
            accy = {}
            for jb in range(nj):
                w, b, k, r = vec(0, jb), vec(2, jb), vec(3, jb), vec(4, jb)
                for i in rows:
                    sn = s[i][jb] * w + sa[i] * b + vb[i] * k
                    s_ref[i, jb * SUBLANES:(jb + 1) * SUBLANES, :] = sn
                    accy[i] = sn * r if jb == 0 else accy[i] + sn * r
            for i in rows:
                y_ref[0, t, i:i + 1, :] = _sublane_allsum(accy[i])[0:1, :]
        unpack(jnp.minimum(t + 1, tc - 1), 1 - slot)

    def two_steps(t2, carry):
        step(2 * t2, 0)
        step(2 * t2 + 1, 1)
        return carry

    unpack(0, 0)
    lax.fori_loop(0, tc // 2, two_steps, 0)

    @pl.when(c == pl.num_programs(1) - 1)
    def _():
        sout_ref[0] = s_ref[...]


def _rwkv_scan(wa, bk, rr, vvec, s0, tc):
    g, t = wa.shape[0], wa.shape[1]
    jspec = pl.BlockSpec((1, tc, RW_HEAD_DIM, LANES), lambda gi, c: (gi, c, 0, 0))
    return pl.pallas_call(
        functools.partial(_rwkv_scan_kernel, tc=tc),
        grid=(g, t // tc),
        in_specs=[jspec, jspec, jspec,
                  pl.BlockSpec((1, tc, SCAN_ROWS, LANES), lambda gi, c: (gi, c, 0, 0)),
                  pl.BlockSpec((1, SCAN_ROWS, RW_HEAD_DIM, LANES), lambda gi, c: (gi, 0, 0, 0))],
        out_specs=[pl.BlockSpec((1, tc, SCAN_ROWS, LANES), lambda gi, c: (gi, c, 0, 0)),
                   pl.BlockSpec((1, SCAN_ROWS, RW_HEAD_DIM, LANES), lambda gi, c: (gi, 0, 0, 0))],
        out_shape=[jax.ShapeDtypeStruct((g, t, SCAN_ROWS, LANES), F32),
                   jax.ShapeDtypeStruct((g, SCAN_ROWS, RW_HEAD_DIM, LANES), F32)],
        scratch_shapes=[pltpu.VMEM((SCAN_ROWS, RW_HEAD_DIM, LANES), F32), pltpu.VMEM((2, 4, RW_HEAD_DIM, LANES), F32)],
        compiler_params=_cparams("arbitrary", "arbitrary"),
        name="rwkv_scan",
    )(wa, bk, rr, vvec, s0)


def _to_scan_pair(x, b):
    t, g = x.shape[0], b // SCAN_SEQS
    x = x.reshape(t, g, SCAN_SEQS, 2, RW_HEADS, RW_HEAD_DIM)
    return jnp.transpose(x, (1, 0, 5, 2, 3, 4)).reshape(g, t, RW_HEAD_DIM, LANES)


def _to_scan_rows(x, b):
    t, g = x.shape[0], b // SCAN_SEQS
    x = x.reshape(t, g, SCAN_SEQS, RW_HEADS, 2, SCAN_ROWS)
    return jnp.transpose(x, (1, 0, 5, 2, 4, 3)).reshape(g, t, SCAN_ROWS, LANES)


def _from_scan_rows(y):
    g, t = y.shape[0], y.shape[1]
    y = y.reshape(g, t, SCAN_ROWS, SCAN_SEQS, 2, RW_HEADS)
    return jnp.transpose(y, (1, 0, 3, 5, 4, 2)).reshape(t, g * SCAN_SEQS * RW_WIDTH)


def _state_to_scan(s):
    b = s.shape[0]
    g = b // SCAN_SEQS
    s = s.reshape(g, SCAN_SEQS, RW_HEADS, 2, SCAN_ROWS, RW_HEAD_DIM)
    return jnp.transpose(s, (0, 4, 5, 1, 3, 2)).reshape(g, SCAN_ROWS, RW_HEAD_DIM, LANES)


def _state_from_scan(s, b):
    g = s.shape[0]
    s = s.reshape(g, SCAN_ROWS, RW_HEAD_DIM, SCAN_SEQS, 2, RW_HEADS)
    return jnp.transpose(s, (0, 3, 5, 4, 1, 2)).reshape(b, RW_HEADS, RW_HEAD_DIM, RW_HEAD_DIM)


def _gdn_prep_kernel(x_ref, init_ref, ab_ref, cw_ref, alog_ref, dt_ref, ones_ref, q_ref, k_ref, v_ref, gb_ref,
                     ext_ref, *, halo, bs, tm):
    cur = x_ref[...]
    _stage_tile(ext_ref, init_ref, cur, halo, tm)
    conv = cur * cw_ref[GDN_CONV - 1:GDN_CONV, :]
    for s in range(1, GDN_CONV):
        off = halo - s * bs
        conv = conv + ext_ref[off:off + tm, :] * cw_ref[GDN_CONV - 1 - s:GDN_CONV - s, :]
    qkv = conv * _sigmoid(conv)
    q = qkv[:, 0:GDN_WIDTH]
    k = qkv[:, GDN_WIDTH:2 * GDN_WIDTH]
    ones = ones_ref[...]
    q_ref[...] = q * lax.rsqrt(_seg_sum(q * q, ones) + 1e-6) * (GDN_HEAD_DIM ** -0.5)
    k_ref[...] = k * lax.rsqrt(_seg_sum(k * k, ones) + 1e-6)
    v_ref[...] = qkv[:, 2 * GDN_WIDTH:3 * GDN_WIDTH]
    ab = ab_ref[...]
    g = -jnp.exp(alog_ref[...]) * _softplus(ab + dt_ref[...])
    lane = lax.broadcasted_iota(jnp.int32, ab.shape, 1)
    gb_ref[...] = jnp.where(lane < GDN_HEADS, g, _sigmoid(ab))


def _gdn_prep(qkv_raw, init, ab, p, nb, tm, bs):
    n = qkv_raw.shape[0]
    nt = n // (nb * tm)
    halo = init.shape[1]
    row = lambda w: pl.BlockSpec((tm, w), lambda b, t: (b * nt + t, 0))
    consts = [p["gdn_conv_w"], p["gdn_alog"], p["gdn_dt"], p["ones_gdn"]]
    return pl.pallas_call(
        functools.partial(_gdn_prep_kernel, halo=halo, bs=bs, tm=tm),
        grid=(nb, nt),
        in_specs=[row(GDN_CONV_CH), pl.BlockSpec((1, halo, GDN_CONV_CH), lambda b, t: (b, 0, 0)), row(LANES)]
        + [_const_spec(c.shape) for c in consts],
        out_specs=[row(GDN_WIDTH)] * 3 + [row(LANES)],
        out_shape=[jax.ShapeDtypeStruct((n, GDN_WIDTH), F32)] * 3 + [jax.ShapeDtypeStruct((n, LANES), F32)],
        scratch_shapes=[pltpu.VMEM((halo + tm, GDN_CONV_CH), F32)],
        compiler_params=_cparams("arbitrary", "arbitrary"),
        name="gdn_prep",
    )(qkv_raw, init, ab, *consts)


def _unit_lower_inverses(a_list, eye, chunk):
    n = [-a for a in a_list]
    inv = [eye + x for x in n]
    for _ in range(int(math.log2(chunk)) - 1):
        ns = [_split_bf16(x) for x in n]
        n = [_dot3(x, x) for x in ns]
        ns = [_split_bf16(x) for x in n]
        inv = [x + _dot3(_split_bf16(x), y) for x, y in zip(inv, ns)]
    return inv


def _gdn_chunk_kernel(q_ref, k_ref, v_ref, gb_ref, s0_ref, o_ref, sout_ref, s_ref, *, chunk, nbb, group):
    ci = pl.program_id(1)

    @pl.when(ci == 0)
    def _():
        s_ref[...] = s0_ref[...]

    rows = lax.broadcasted_iota(jnp.int32, (chunk, chunk), 0)
    cols = lax.broadcasted_iota(jnp.int32, (chunk, chunk), 1)
    causal = rows >= cols
    strict = rows > cols
    eye = (rows == cols).astype(F32)
    tri = causal.astype(F32)

    def one_group(gi, carry):
        units = [(gi * group + j, h) for j in range(group) for h in range(GDN_HEADS)]
        lanes = lambda h: slice(h * GDN_HEAD_DIM, (h + 1) * GDN_HEAD_DIM)
        s = [s_ref[b, h] for b, h in units]
        q = [q_ref[b, :, lanes(h)] for b, h in units]
        k = [k_ref[b, :, lanes(h)] for b, h in units]
        v = [v_ref[b, :, lanes(h)] for b, h in units]
        gb = [gb_ref[gi * group + j] for j in range(group)]
        gcs = [_dot_hi(tri, x) for x in gb]
        gc = [gcs[i // GDN_HEADS][:, h:h + 1] for i, (_, h) in enumerate(units)]
        beta = [gb[i // GDN_HEADS][:, GDN_HEADS + h:GDN_HEADS + h + 1] for i, (_, h) in enumerate(units)]
        decay = [jnp.exp(jnp.where(causal, x - jnp.sum(eye * x, axis=0, keepdims=True), -jnp.inf)) for x in gc]
        kb = [x * y for x, y in zip(k, beta)]
        a = [jnp.where(strict, _dot_nt(x, y) * d, 0.0) for x, y, d in zip(kb, k, decay)]
        inv = _unit_lower_inverses(a, eye, chunk)
        egc = [jnp.exp(x) for x in gc]
        rhs = [jnp.concatenate([x * bt, y * e], axis=1) for x, bt, y, e in zip(v, beta, kb, egc)]
        sol = [_dot3(_split_bf16(x), _split_bf16(y)) for x, y in zip(inv, rhs)]
        wS = [_dot(x[:, GDN_HEAD_DIM:2 * GDN_HEAD_DIM], y) for x, y in zip(sol, s)]
        v_new = [x[:, 0:GDN_HEAD_DIM] - y for x, y in zip(sol, wS)]
        qk = [_dot_nt(x, y) * d for x, y, d in zip(q, k, decay)]
        o_state = [_dot(x * e, y) for x, e, y in zip(q, egc, s)]
        o_chunk = [_dot(x, y) for x, y in zip(qk, v_new)]
        g_last = [x[chunk - 1:chunk, :] for x in gc]
        kd_t = [(x * jnp.exp(gl - g)).T for x, gl, g in zip(k, g_last, gc)]
        s_add = [_dot(x, y) for x, y in zip(kd_t, v_new)]
        for i, (b, h) in enumerate(units):
            o_ref[b, :, lanes(h)] = o_state[i] + o_chunk[i]
            s_ref[b, h] = s[i] * jnp.exp(g_last[i]) + s_add[i]
        return carry

    if nbb == group:
        one_group(0, 0)
    else:
        lax.fori_loop(0, nbb // group, one_group, 0)

    @pl.when(ci == pl.num_programs(1) - 1)
    def _():
        sout_ref[...] = s_ref[...]


def _gdn_chunks(q, k, v, gb, s0, chunk):
    b, t_len, _ = q.shape
    nbb = min(b, GDN_SEQ_BLOCK)
    group = min(nbb, GDN_SEQ_GROUP)
    blk = lambda w: pl.BlockSpec((nbb, chunk, w), lambda bi, ci: (bi, ci, 0))
    st = pl.BlockSpec((nbb, GDN_HEADS, GDN_HEAD_DIM, GDN_HEAD_DIM), lambda bi, ci: (bi, 0, 0, 0))
    return pl.pallas_call(
        functools.partial(_gdn_chunk_kernel, chunk=chunk, nbb=nbb, group=group),
        grid=(b // nbb, t_len // chunk),
        in_specs=[blk(GDN_WIDTH)] * 3 + [blk(LANES), st],
        out_specs=[blk(GDN_WIDTH), st],
        out_shape=[jax.ShapeDtypeStruct(q.shape, F32), jax.ShapeDtypeStruct(s0.shape, F32)],
        scratch_shapes=[pltpu.VMEM((nbb, GDN_HEADS, GDN_HEAD_DIM, GDN_HEAD_DIM), F32)],
        compiler_params=_cparams("arbitrary", "arbitrary"),
        name="gdn_chunks",
    )(q, k, v, gb, s0)


def _route(logits, count_ref):
    lane = lax.broadcasted_iota(jnp.int32, logits.shape, 1)
    neg = -jnp.inf
    is_group = (lane >= N_EXPERTS) & (lane < N_EXPERTS + N_GROUPS)
    gl = jnp.where(is_group, logits, neg)
    gmax = jnp.max(gl, axis=-1, keepdims=True)
    gp = 1.0 / jnp.sum(jnp.exp(gl - gmax), axis=-1, keepdims=True)
    gidx = jnp.min(jnp.where(gl == gmax, lane, LANES), axis=-1, keepdims=True) - N_EXPERTS
    lo = gidx * EXPERTS_PER_GROUP
    el = jnp.where((lane >= lo) & (lane < lo + EXPERTS_PER_GROUP), logits, neg)
    m1 = jnp.max(el, axis=-1, keepdims=True)
    i1 = jnp.min(jnp.where(el == m1, lane, LANES), axis=-1, keepdims=True)
    el2 = jnp.where(lane == i1, neg, el)
    m2 = jnp.max(el2, axis=-1, keepdims=True)
    i2 = jnp.min(jnp.where(el2 == m2, lane, LANES), axis=-1, keepdims=True)
    e2 = jnp.exp(m2 - m1)
    den = 1.0 + e2
    tm = logits.shape[0]
    hit1 = lane == i1
    hit2 = lane == i2
    chosen = jnp.where(hit1 | hit2, 1.0, 0.0)
    before = lax.broadcasted_iota(jnp.int32, (tm, tm), 0) > lax.broadcasted_iota(jnp.int32, (tm, tm), 1)
    seen = jnp.dot(before.astype(BF16), chosen.astype(BF16), preferred_element_type=F32) + count_ref[...]
    rank1 = jnp.sum(jnp.where(hit1, seen, 0.0), axis=-1, keepdims=True)
    rank2 = jnp.sum(jnp.where(hit2, seen, 0.0), axis=-1, keepdims=True)
    count_ref[...] += jnp.sum(chosen, axis=0, keepdims=True)
    fields = (i1.astype(F32), i2.astype(F32), gp / den, gp * e2 / den, rank1, rank2)
    out = jnp.zeros(logits.shape, F32)
    for j, f in enumerate(fields):
        out = jnp.where(lane == j, f, out)
    return out


def _merge_kernel(y_ref, r_ref, k_ref, v_ref, g_ref, o_ref, z_ref, gate_ref, h_ref, lnw_ref, lnb_ref, rk_ref,
                  ones_ref, gnw_ref, woa_ref, wob_ref, wo_ref, n2_ref, rw_ref, rb_ref, h1_ref, xt_ref, mg_ref, cnt_ref):
    @pl.when(pl.program_id(0) == 0)
    def _():
        cnt_ref[...] = jnp.zeros_like(cnt_ref)

    ones = ones_ref[...]
    inv_n = 1.0 / RW_HEAD_DIM
    y = y_ref[...]
    v = v_ref[...]
    mu = _seg_sum(y, ones) * inv_n
    yc = y - mu
    var = _seg_sum(yc * yc, ones) * inv_n
    ya = yc * lax.rsqrt(var + RW_GN_EPS) * lnw_ref[...] + lnb_ref[...]
    bonus = _seg_sum(r_ref[...] * k_ref[...] * rk_ref[...], ones) * v
    ya = (ya + bonus) * g_ref[...]

    z = z_ref[...]
    gnw = gnw_ref[...]
    yb_parts = []
    for h in range(GDN_HEADS):
        hs = slice(h * GDN_HEAD_DIM, (h + 1) * GDN_HEAD_DIM)
        yb_parts.append(_rmsnorm(o_ref[:, hs], gnw))
    yb = jnp.concatenate(yb_parts, axis=1) * (z * _sigmoid(z))

    gates = gate_ref[...]
    merged = _sigmoid(gates[:, 0:D_MODEL]) * _dot(ya, woa_ref[...]) \
        + _sigmoid(gates[:, D_MODEL:2 * D_MODEL]) * _dot(yb, wob_ref[...])
    h1 = h_ref[...] + _dot(merged, wo_ref[...])
    h1_ref[...] = h1
    xt = _rmsnorm(h1, n2_ref[...])
    xt_ref[...] = xt
    mg_ref[...] = _route(_dot_f32(xt, rw_ref[...]) + rb_ref[...], cnt_ref)


def _merge(y, rr, bk, v, g, o, z, gates, h, p, tm, wide_nt):
    n = h.shape[0]
    row = lambda w: pl.BlockSpec((tm, w), lambda i: (i, 0))
    if wide_nt is None:
        part = lambda per_seq, k: pl.BlockSpec((tm, RW_WIDTH), lambda i: (i, k))
    else:
        part = lambda per_seq, k: pl.BlockSpec((tm, RW_WIDTH), lambda i: (i % wide_nt, (i // wide_nt) * per_seq + k))
    consts = [p["rw_lnx_w"], p["rw_lnx_b"], p["rw_r_k"], p["ones_rw"], p["gdn_norm_w"], p["w_oA"], p["w_oB"],
              p["w_o"], p["norm2_w"], p["router_w"], p["router_b"]]
    return pl.pallas_call(
        _merge_kernel,
        grid=(n // tm,),
        in_specs=[part(1, 0), part(2, 0), part(2, 1), part(1, 0)] + [row(RW_WIDTH)] * 3
        + [row(2 * D_MODEL), row(D_MODEL)] + [_const_spec(c.shape) for c in consts],
        out_specs=[row(D_MODEL), row(D_MODEL), row(LANES), _const_spec((1, LANES))],
        out_shape=[jax.ShapeDtypeStruct((n, D_MODEL), F32), jax.ShapeDtypeStruct((n, D_MODEL), F32),
                   jax.ShapeDtypeStruct((n, LANES), F32), jax.ShapeDtypeStruct((1, LANES), F32)],
        compiler_params=_cparams("arbitrary"),
        name="merge",
    )(y, rr, bk, v, g, o, z, gates, h, *consts)


def _start_row_gather(src_hbm, idx_ref, dst_ref, sem, n, priority):
    def body(r, carry):
        pltpu.make_async_copy(src_hbm.at[pl.ds(idx_ref[0, 0, r], 1)], dst_ref.at[pl.ds(r, 1)], sem).start(priority)
        return carry

    lax.fori_loop(0, n, body, 0, unroll=8)


def _wait_row_gather(src_hbm, dst_ref, sem, n):
    def body(r, carry):
        pltpu.make_async_copy(src_hbm.at[pl.ds(0, 1)], dst_ref.at[pl.ds(r, 1)], sem).wait()
        return carry

    lax.fori_loop(0, n, body, 0, unroll=8)


def _moe_dispatch_kernel(p1_ref, p2_ref, x_ref, zeros_hbm, xs_hbm, xbuf, sems, *, tm):
    del zeros_hbm
    i = pl.program_id(0)
    nt = pl.num_programs(0)
    slot = i % 2

    def wait_tile(s):
        def body(r, carry):
            pltpu.make_async_copy(xbuf.at[s, pl.ds(0, 1)], xs_hbm.at[pl.ds(0, 1)], sems.at[s]).wait()
            return carry

        lax.fori_loop(0, 2 * tm, body, 0, unroll=8)

    @pl.when(i >= 2)
    def _():
        wait_tile(slot)

    xbuf[slot] = x_ref[...]

    def issue(r, carry):
        src = xbuf.at[slot, pl.ds(r, 1)]
        pltpu.make_async_copy(src, xs_hbm.at[pl.ds(p1_ref[0, 0, r], 1)], sems.at[slot]).start(0)
        pltpu.make_async_copy(src, xs_hbm.at[pl.ds(p2_ref[0, 0, r], 1)], sems.at[slot]).start(1)
        return carry

    lax.fori_loop(0, tm, issue, 0, unroll=8)

    @pl.when(i == nt - 1)
    def _():
        wait_tile(slot)

        @pl.when(nt > 1)
        def _():
            wait_tile(1 - slot)


def _moe_dispatch(xt, pos1, pos2, n_sorted, tm):
    n = xt.shape[0]
    idx = pl.BlockSpec((1, 1, tm), lambda i: (i, 0, 0), memory_space=pltpu.SMEM)
    return pl.pallas_call(
        functools.partial(_moe_dispatch_kernel, tm=tm),
        grid=(n // tm,),
        in_specs=[idx, idx, pl.BlockSpec((tm, D_MODEL), lambda i: (i, 0)), pl.BlockSpec(memory_space=pl.ANY)],
        out_specs=pl.BlockSpec(memory_space=pl.ANY),
        out_shape=jax.ShapeDtypeStruct((n_sorted, D_MODEL), F32),
        scratch_shapes=[pltpu.VMEM((2, tm, D_MODEL), F32), pltpu.SemaphoreType.DMA((2,))],
        input_output_aliases={3: 0},
        compiler_params=_cparams("arbitrary"),
        name="moe_dispatch",
    )(pos1, pos2, xt, jnp.zeros((n_sorted, D_MODEL), F32))


def _moe_experts_kernel(te_ref, tv_ref, x_ref, wg_ref, wu_ref, wd_ref, y_ref):
    i = pl.program_id(0)

    @pl.when(tv_ref[i] == 1)
    def _():
        xb = x_ref[...].astype(BF16)
        hg = jnp.dot(xb, wg_ref[0].astype(BF16), preferred_element_type=F32)
        hu = jnp.dot(xb, wu_ref[0].astype(BF16), preferred_element_type=F32)
        hid = hg * _sigmoid(hg) * hu
        y_ref[...] = jnp.dot(hid.astype(BF16), wd_ref[0].astype(BF16), preferred_element_type=F32)

    @pl.when(tv_ref[i] == 0)
    def _():
        y_ref[...] = jnp.zeros_like(y_ref)


def _moe_experts(xs, tile_expert, tile_valid, wg, wu, wd, tm):
    nt = xs.shape[0] // tm
    wspec = lambda shape: pl.BlockSpec((1,) + shape, lambda i, te, tv: (te[i], 0, 0))
    rows = pl.BlockSpec((tm, D_MODEL), lambda i, te, tv: (i, 0))
    return pl.pallas_call(
        _moe_experts_kernel,
        grid_spec=pltpu.PrefetchScalarGridSpec(
            num_scalar_prefetch=2,
            grid=(nt,),
            in_specs=[rows, wspec((D_MODEL, EXPERT_FF)), wspec((D_MODEL, EXPERT_FF)), wspec((EXPERT_FF, D_MODEL))],
            out_specs=rows,
        ),
        out_shape=jax.ShapeDtypeStruct(xs.shape, F32),
        compiler_params=_cparams("arbitrary"),
        name="moe_experts",
    )(tile_expert, tile_valid, xs, wg, wu, wd)


def _moe_combine_kernel(cur1_ref, cur2_ref, nxt1_ref, nxt2_ref, ys_hbm, h1_ref, mg_ref, nf_ref, o_ref, buf1, buf2,
                        sems, *, tm):
    i = pl.program_id(0)
    nt = pl.num_programs(0)
    slot = i % 2

    @pl.when(i == 0)
    def _():
        _start_row_gather(ys_hbm, cur1_ref, buf1.at[0], sems.at[0, 0], tm, 0)
        _start_row_gather(ys_hbm, cur2_ref, buf2.at[0], sems.at[1, 0], tm, 1)

    @pl.when(i + 1 < nt)
    def _():
        _start_row_gather(ys_hbm, nxt1_ref, buf1.at[1 - slot], sems.at[0, 1 - slot], tm, 0)
        _start_row_gather(ys_hbm, nxt2_ref, buf2.at[1 - slot], sems.at[1, 1 - slot], tm, 1)

    _wait_row_gather(ys_hbm, buf1.at[slot], sems.at[0, slot], tm)
    _wait_row_gather(ys_hbm, buf2.at[slot], sems.at[1, slot], tm)
    mg = mg_ref[...]
    moe = mg[:, 2:3] * buf1[slot] + mg[:, 3:4] * buf2[slot]
    o_ref[...] = _rmsnorm(h1_ref[...] + moe, nf_ref[...])


def _moe_combine(ys, pos1, pos2, h1, mg, nf, tm):
    n = h1.shape[0]
    nt = n // tm
    cur = pl.BlockSpec((1, 1, tm), lambda i: (i, 0, 0), memory_space=pltpu.SMEM)
    nxt = pl.BlockSpec((1, 1, tm), lambda i: (jnp.minimum(i + 1, nt - 1), 0, 0), memory_space=pltpu.SMEM)
    row = lambda w: pl.BlockSpec((tm, w), lambda i: (i, 0))
    return pl.pallas_call(
        functools.partial(_moe_combine_kernel, tm=tm),
        grid=(nt,),
        in_specs=[cur, cur, nxt, nxt, pl.BlockSpec(memory_space=pl.ANY), row(D_MODEL), row(LANES),
                  _const_spec((1, D_MODEL))],
        out_specs=row(D_MODEL),
        out_shape=jax.ShapeDtypeStruct((n, D_MODEL), F32),
        scratch_shapes=[pltpu.VMEM((2, tm, D_MODEL), F32), pltpu.VMEM((2, tm, D_MODEL), F32),
                        pltpu.SemaphoreType.DMA((2, 2))],
        compiler_params=_cparams("arbitrary"),
        name="moe_combine",
    )(pos1, pos2, pos1, pos2, ys, h1, mg, nf)


def _moe(xt, h1, mg, counts, wg, wu, wd, nf, tm_expert, tm_token):
    n = xt.shape[0]
    i32 = jnp.int32
    expert = jnp.arange(N_EXPERTS, dtype=i32)
    cnt = counts[0, :N_EXPERTS].astype(i32)
    padded = (cnt + tm_expert - 1) // tm_expert * tm_expert
    ends = jnp.sum(jnp.where(expert[:, None] >= expert[None, :], padded[None, :], 0), axis=1)
    starts = ends - padded
    start_of = lambda e: jnp.sum(jnp.where(e[:, None] == expert[None, :], starts[None, :], 0), axis=1)
    pos1 = start_of(mg[:, 0].astype(i32)) + mg[:, 4].astype(i32)
    pos2 = start_of(mg[:, 1].astype(i32)) + mg[:, 5].astype(i32)
    nt = (2 * n + N_EXPERTS * (tm_expert - 1)) // tm_expert + 1
    tile_start = jnp.arange(nt, dtype=i32) * tm_expert
    tile_valid = (tile_start < jnp.sum(padded)).astype(i32)
    tile_expert = jnp.sum((tile_start[:, None] >= ends[None, :]).astype(i32), axis=1)
    last_used = jnp.max(jnp.where(padded > 0, expert, 0))
    tile_expert = jnp.where(tile_valid == 1, tile_expert, last_used)
    shape = (n // tm_token, 1, tm_token)
    pos1, pos2 = pos1.reshape(shape), pos2.reshape(shape)
    xs = _moe_dispatch(xt, pos1, pos2, nt * tm_expert, tm_token)
    ys = _moe_experts(xs, tile_expert, tile_valid, wg, wu, wd, tm_expert)
    return _moe_combine(ys, pos1, pos2, h1, mg, nf, tm_token)


def _pad_rows(x, rows):
    return jnp.pad(x, ((0, rows - x.shape[0]),) + ((0, 0),) * (x.ndim - 1))


def _mixer(x_rows, p, *, batch, t_len, time_major, tm, rw_prev, conv_prev, wkv0, gdn0, chunk, with_merge):
    n = batch * t_len
    xn, rw_cur, qkv_raw, z, gates, ab = _in_proj(x_rows, p["norm1_w"], p["w_in"], min(tm, n))

    if time_major:
        nb, bs, tmt = 1, batch, n
        rw_init = rw_prev[None]
        conv_init = jnp.transpose(conv_prev, (1, 0, 2)).reshape(1, (GDN_CONV - 1) * batch, GDN_CONV_CH)
    else:
        nb, bs, tmt = batch, 1, min(tm, t_len)
        rw_init = jnp.pad(rw_prev[:, None, :], ((0, 0), (SUBLANES - 1, 0), (0, 0)))
        conv_init = jnp.pad(conv_prev, ((0, 0), (SUBLANES - (GDN_CONV - 1), 0), (0, 0)))

    wide = not time_major
    wa, bk, rr, v, g = _rwkv_prep(rw_cur, rw_init, p, nb, tmt, bs, wide)
    q, kg, vg, gb = _gdn_prep(qkv_raw, conv_init, ab, p, nb, tmt, bs)

    def seq_major(a):
        if time_major:
            return jnp.transpose(a.reshape(t_len, batch, a.shape[-1]), (1, 0, 2))
        return a.reshape(batch, t_len, a.shape[-1])

    def rows_like_x(a):
        if time_major:
            a = jnp.transpose(a, (1, 0, 2))
        return a.reshape(n, a.shape[-1])

    t_pad = -(-t_len // chunk) * chunk
    def gdn_seq(a):
        a = seq_major(a)
        if t_pad != t_len:
            a = jnp.pad(a, ((0, 0), (0, t_pad - t_len), (0, 0)))
        return a
    o_pad, gdn = _gdn_chunks(gdn_seq(q), gdn_seq(kg), gdn_seq(vg), gdn_seq(gb), gdn0, chunk)

    bpad = -(-batch // SCAN_SEQS) * SCAN_SEQS

    def by_time(a):
        a = a.reshape(t_len, -1)
        return jnp.pad(a, ((0, 0), (0, a.shape[1] // batch * (bpad - batch))))

    scan_in = tuple(_to_scan_pair(by_time(a), bpad) for a in (wa, bk, rr)) + (_to_scan_rows(by_time(v), bpad),)
    scan_in, o_pad = lax.optimization_barrier((scan_in, o_pad))
    tc = min(t_len, 32)
    y_scan, s_scan = _rwkv_scan(*scan_in, _state_to_scan(_pad_rows(wkv0, bpad)), tc)
    y_rw = _from_scan_rows(y_scan)[:, :batch * RW_WIDTH]
    if time_major:
        y_rw = y_rw.reshape(n, RW_WIDTH)
    wkv = _state_from_scan(s_scan, bpad)[:batch]
    o = rows_like_x(o_pad[:, :t_len])

    out = dict(xn=xn, rw_cur=rw_cur, qkv_raw=qkv_raw, wkv=wkv, gdn=gdn)
    if with_merge:
        out["h1"], out["xt"], out["mg"], out["counts"] = _merge(y_rw, rr, bk, v, g, o, z, gates, x_rows, p, min(tm, n),
                                                                 t_len // tmt if wide else None)
    return out


def kernel(x_prompt, x_sample, state_shift, state_wkv, state_conv, state_gdn, meta_tokens, norm1_w, w_in, mu_shift, rw_w0, rw_w2, rw_a0, rw_a2, rw_g2, rw_k_k, rw_k_a, rw_r_k, rw_lnx_w, rw_lnx_b, gdn_conv_w, gdn_A_log, gdn_dt_bias, gdn_norm_w, w_oA, w_oB, w_o, norm2_w, router_g, router_g_b, router_e, router_e_b, moe_w_gate, moe_w_up, moe_w_down, norm_f_w):
    bp, sp, _ = x_prompt.shape
    bsm, ss, _ = x_sample.shape
    row = lambda a: a[0].reshape(1, -1)
    w0 = w_in[0]
    gcol = RW_COLS + 4 * GDN_WIDTH + 2 * GDN_HEADS
    w_packed = jnp.concatenate(
        [w0[:, :RW_COLS + 4 * GDN_WIDTH], w0[:, gcol:], w0[:, RW_COLS + 4 * GDN_WIDTH:gcol],
         jnp.zeros((D_MODEL, LANES - 2 * GDN_HEADS), F32)], axis=1).astype(BF16)
    lane_pad = lambda a: jnp.pad(a, ((0, 0), (0, LANES - a.shape[1])))
    p = dict(
        norm1_w=row(norm1_w), w_in=w_packed, mu_shift=row(mu_shift), rw_w0=row(rw_w0), rw_w2=rw_w2[0],
        rw_a0=row(rw_a0), rw_a2=rw_a2[0], rw_g2=rw_g2[0], rw_k_k=row(rw_k_k), rw_k_a=row(rw_k_a),
        rw_r_k=row(rw_r_k), rw_lnx_w=row(rw_lnx_w), rw_lnx_b=row(rw_lnx_b),
        ones_rw=_block_ones(RW_WIDTH, RW_HEAD_DIM), ones_gdn=_block_ones(GDN_WIDTH, GDN_HEAD_DIM),
        gdn_conv_w=gdn_conv_w[0], gdn_alog=lane_pad(row(gdn_A_log)), gdn_dt=lane_pad(row(gdn_dt_bias)),
        gdn_norm_w=row(gdn_norm_w), w_oA=w_oA[0].astype(BF16), w_oB=w_oB[0].astype(BF16),
        w_o=w_o[0].astype(BF16), norm2_w=row(norm2_w),
        router_w=lane_pad(jnp.concatenate([router_e[0], router_g[0]], axis=1)),
        router_b=lane_pad(jnp.concatenate([row(router_e_b), row(router_g_b)], axis=1)),
    )
    nf = norm_f_w.reshape(1, -1)
    dt = x_prompt.dtype

    meta = _mixer(meta_tokens.astype(dt), p, batch=1, t_len=N_META, time_major=False, tm=N_META,
                  rw_prev=jnp.zeros((1, RW_COLS), dt), conv_prev=jnp.zeros((1, GDN_CONV - 1, GDN_CONV_CH), dt),
                  wkv0=jnp.zeros((1, RW_HEADS, RW_HEAD_DIM, RW_HEAD_DIM), dt),
                  gdn0=jnp.zeros((1, GDN_HEADS, GDN_HEAD_DIM, GDN_HEAD_DIM), dt), chunk=N_META, with_merge=False)
    rep = lambda a: jnp.broadcast_to(a, (bp,) + a.shape[1:])

    prompt = _mixer(x_prompt.reshape(bp * sp, D_MODEL), p, batch=bp, t_len=sp, time_major=False, tm=256,
                    rw_prev=rep(meta["rw_cur"][N_META - 1:]), conv_prev=rep(meta["qkv_raw"][None, N_META - 3:]),
                    wkv0=rep(meta["wkv"]), gdn0=rep(meta["gdn"]), chunk=GDN_CHUNK, with_merge=True)

    xs_rows = jnp.transpose(x_sample, (1, 0, 2)).reshape(ss * bsm, D_MODEL)
    prev_s = _in_proj(state_shift[0].astype(dt), p["norm1_w"], w_packed, bsm, norm=False)[1]
    sample = _mixer(xs_rows, p, batch=bsm, t_len=ss, time_major=True, tm=256,
                    rw_prev=prev_s, conv_prev=state_conv[0].astype(dt), wkv0=state_wkv[0], gdn0=state_gdn[0],
                    chunk=SUBLANES, with_merge=True)

    experts = (moe_w_gate[0], moe_w_up[0], moe_w_down[0])
    y_p = _moe(prompt["xt"], prompt["h1"], prompt["mg"], prompt["counts"], *experts, nf, 256, 256)
    y_s = _moe(sample["xt"], sample["h1"], sample["mg"], sample["counts"], *experts, nf, 128, 256)

    y_prompt = y_p.reshape(bp, sp, D_MODEL)
    y_sample = jnp.transpose(y_s.reshape(ss, bsm, D_MODEL), (1, 0, 2))
    shift_p = prompt["xn"].reshape(bp, sp, D_MODEL)[:, -1][None]
    conv_p = prompt["qkv_raw"].reshape(bp, sp, GDN_CONV_CH)[:, sp - (GDN_CONV - 1):][None]
    shift_s = sample["xn"].reshape(ss, bsm, D_MODEL)[-1][None].astype(state_shift.dtype)
    xpad_s = jnp.concatenate([state_conv[0].astype(dt),
                              jnp.transpose(sample["qkv_raw"].reshape(ss, bsm, GDN_CONV_CH), (1, 0, 2))], axis=1)
    conv_s = xpad_s[:, ss:][None].astype(state_conv.dtype)
    return (y_prompt, y_sample, shift_p, prompt["wkv"][None], conv_p, prompt["gdn"][None],
            shift_s, sample["wkv"][None].astype(state_wkv.dtype), conv_s, sample["gdn"][None].astype(state_gdn.dtype))
```

```python
import functools
import math

import jax
import jax.numpy as jnp
from jax import lax
from jax.experimental import pallas as pl
from jax.experimental.pallas import tpu as pltpu

F32 = jnp.float32
BF16 = jnp.bfloat16
HIGHEST = lax.Precision.HIGHEST

D_MODEL = 1024
N_META = 16
RW_WIDTH = 512
RW_HEAD_DIM = 64
RW_HEADS = 8
RW_DECAY_LORA = 64
RW_AAA_LORA = 64
RW_GATE_LORA = 128
RW_COLS = 3 * RW_WIDTH + RW_DECAY_LORA + RW_AAA_LORA + RW_GATE_LORA
RW_GN_EPS = RW_HEAD_DIM * 1e-5
GDN_WIDTH = 512
GDN_HEAD_DIM = 128
GDN_HEADS = 4
GDN_CONV = 4
GDN_CONV_CH = 3 * GDN_WIDTH
GDN_CHUNK = 64
N_GROUPS = 4
EXPERTS_PER_GROUP = 8
N_EXPERTS = 32
EXPERT_FF = 512
RMS_EPS = 1e-6

LANES = 128
SUBLANES = 8
VMEM_LIMIT_BYTES = 56 * 1024 * 1024

COL_RW = 0
COL_QKV = COL_RW + RW_COLS
COL_Z = COL_QKV + GDN_CONV_CH
COL_GATE = COL_Z + GDN_WIDTH
COL_AB = COL_GATE + 2 * D_MODEL
IN_COLS_PACKED = COL_AB + LANES

SCAN_SEQS = 8
SCAN_ROWS = RW_HEAD_DIM // 2
SCAN_ROW_GROUP = 1

GDN_SEQ_BLOCK = 8
GDN_SEQ_GROUP = 8


def _cparams(*sem):
    return pltpu.CompilerParams(dimension_semantics=sem, vmem_limit_bytes=VMEM_LIMIT_BYTES)


def _const_spec(shape):
    zeros = (0,) * len(shape)
    return pl.BlockSpec(shape, lambda *_: zeros)


def _dot(a, b):
    return jnp.dot(a.astype(BF16), b.astype(BF16), preferred_element_type=F32)


def _dot_hi(a, b):
    return jnp.dot(a, b, preferred_element_type=F32, precision=HIGHEST)


def _split_bf16(a):
    hi = a.astype(BF16)
    return hi, (a - hi.astype(F32)).astype(BF16)


def _dot3(a, b):
    d = lambda x, y: jnp.dot(x, y, preferred_element_type=F32)
    return d(a[0], b[0]) + d(a[0], b[1]) + d(a[1], b[0])


def _dot_f32(a, b):
    return _dot3(_split_bf16(a), _split_bf16(b))


def _seg_sum(x, ones):
    hi, lo = _split_bf16(x)
    return jnp.dot(hi, ones, preferred_element_type=F32) + jnp.dot(lo, ones, preferred_element_type=F32)


def _dot_nt(a, b):
    dims = (((1,), (1,)), ((), ()))
    return lax.dot_general(a.astype(BF16), b.astype(BF16), dims, preferred_element_type=F32)


def _sigmoid(x):
    return 1.0 / (1.0 + jnp.exp(-x))


def _softplus(x):
    return jnp.maximum(x, 0.0) + jnp.log1p(jnp.exp(-jnp.abs(x)))


def _rmsnorm(x, w):
    ms = jnp.mean(x * x, axis=-1, keepdims=True)
    return x * lax.rsqrt(ms + RMS_EPS) * w


def _block_ones(width, seg):
    r = jnp.arange(width) // seg
    return (r[:, None] == r[None, :]).astype(BF16)


def _in_proj_kernel(x_ref, nw_ref, w_ref, xn_ref, rw_ref, qkv_ref, z_ref, gate_ref, ab_ref, *, norm):
    x = x_ref[...]
    xn = _rmsnorm(x, nw_ref[...]) if norm else x
    xn_ref[...] = xn
    xb = xn.astype(BF16)
    rw_ref[...] = jnp.dot(xb, w_ref[:, COL_RW:COL_QKV], preferred_element_type=F32)
    qkv_ref[...] = jnp.dot(xb, w_ref[:, COL_QKV:COL_Z], preferred_element_type=F32)
    z_ref[...] = jnp.dot(xb, w_ref[:, COL_Z:COL_GATE], preferred_element_type=F32)
    gate_ref[...] = jnp.dot(xb, w_ref[:, COL_GATE:COL_AB], preferred_element_type=F32)
    ab_ref[...] = jnp.dot(xb, w_ref[:, COL_AB:IN_COLS_PACKED], preferred_element_type=F32)


def _in_proj(x, norm_w, w_packed, tm, norm=True):
    n = x.shape[0]
    widths = (D_MODEL, RW_COLS, GDN_CONV_CH, GDN_WIDTH, 2 * D_MODEL, LANES)
    row = lambda w: pl.BlockSpec((tm, w), lambda i: (i, 0))
    return pl.pallas_call(
        functools.partial(_in_proj_kernel, norm=norm),
        grid=(n // tm,),
        in_specs=[row(D_MODEL), _const_spec((1, D_MODEL)), _const_spec(w_packed.shape)],
        out_specs=[row(w) for w in widths],
        out_shape=[jax.ShapeDtypeStruct((n, w), F32) for w in widths],
        compiler_params=_cparams("arbitrary"),
        name="in_proj",
    )(x, norm_w, w_packed)


def _stage_tile(ext_ref, init_ref, cur, halo, tm):
    t = pl.program_id(1)

    @pl.when(t == 0)
    def _():
        ext_ref[0:halo, :] = init_ref[0]

    @pl.when(t > 0)
    def _():
        ext_ref[0:halo, :] = ext_ref[tm:tm + halo, :]

    ext_ref[halo:halo + tm, :] = cur


def _rwkv_prep_kernel(cur_ref, init_ref, mu_ref, w0_ref, w2_ref, a0_ref, a2_ref, g2_ref, kk_ref, ka_ref,
                      ones_ref, wa_ref, bk_ref, rr_ref, v_ref, g_ref, ext_ref, *, halo, bs, tm):
    cur = cur_ref[...]
    _stage_tile(ext_ref, init_ref, cur, halo, tm)
    prev = ext_ref[halo - bs:halo - bs + tm, :]
    zc = cur + (prev - cur) * mu_ref[...]
    c0, c1, c2 = RW_WIDTH, 2 * RW_WIDTH, 3 * RW_WIDTH
    r = zc[:, 0:c0]
    k = zc[:, c0:c1]
    v = zc[:, c1:c2]
    xw = zc[:, c2:c2 + RW_DECAY_LORA]
    xa = zc[:, c2 + RW_DECAY_LORA:c2 + RW_DECAY_LORA + RW_AAA_LORA]
    xg = zc[:, c2 + RW_DECAY_LORA + RW_AAA_LORA:RW_COLS]
    w_raw = -_softplus(-(w0_ref[...] + _dot_f32(jnp.tanh(xw), w2_ref[...]))) - 0.5
    decay = jnp.exp(-jnp.exp(w_raw))
    a = _sigmoid(a0_ref[...] + _dot_f32(xa, a2_ref[...]))
    g = _dot_f32(_sigmoid(xg), g2_ref[...])
    kkr = k * kk_ref[...]
    kk = kkr * lax.rsqrt(_seg_sum(kkr * kkr, ones_ref[...]) + 1e-6)
    wa_ref[:, 0:RW_WIDTH] = decay
    wa_ref[:, RW_WIDTH:2 * RW_WIDTH] = -kk
    bk_ref[:, 0:RW_WIDTH] = kk * a
    bk_ref[:, RW_WIDTH:2 * RW_WIDTH] = k * (1.0 + (a - 1.0) * ka_ref[...])
    rr_ref[:, 0:RW_WIDTH] = r
    rr_ref[:, RW_WIDTH:2 * RW_WIDTH] = r
    v_ref[...] = v
    g_ref[...] = g


def _rwkv_prep(rw_cur, init, p, nb, tm, bs, wide):
    n = rw_cur.shape[0]
    nt = n // (nb * tm)
    halo = init.shape[1]
    row = lambda w: pl.BlockSpec((tm, w), lambda b, t: (b * nt + t, 0))
    if wide:
        out_row = lambda w: pl.BlockSpec((tm, w), lambda b, t: (t, b))
        out_shape = lambda w: jax.ShapeDtypeStruct((n // nb, nb * w), F32)
    else:
        out_row = row
        out_shape = lambda w: jax.ShapeDtypeStruct((n, w), F32)
    consts = [p["mu_shift"], p["rw_w0"], p["rw_w2"], p["rw_a0"], p["rw_a2"], p["rw_g2"], p["rw_k_k"], p["rw_k_a"],
              p["ones_rw"]]
    return pl.pallas_call(
        functools.partial(_rwkv_prep_kernel, halo=halo, bs=bs, tm=tm),
        grid=(nb, nt),
        in_specs=[row(RW_COLS), pl.BlockSpec((1, halo, RW_COLS), lambda b, t: (b, 0, 0))]
        + [_const_spec(c.shape) for c in consts],
        out_specs=[out_row(2 * RW_WIDTH)] * 3 + [out_row(RW_WIDTH), row(RW_WIDTH)],
        out_shape=[out_shape(2 * RW_WIDTH)] * 3 + [out_shape(RW_WIDTH), jax.ShapeDtypeStruct((n, RW_WIDTH), F32)],
        scratch_shapes=[pltpu.VMEM((halo + tm, RW_COLS), F32)],
        compiler_params=_cparams("arbitrary", "arbitrary"),
        name="rwkv_prep",
    )(rw_cur, init, *consts)


def _sublane_allsum(x):
    x = x + pltpu.roll(x, 4, 0)
    x = x + pltpu.roll(x, 2, 0)
    return x + pltpu.roll(x, 1, 0)


def _rwkv_scan_kernel(wa_ref, bk_ref, rr_ref, v_ref, s0_ref, y_ref, sout_ref, s_ref, vec_ref, *, tc):
    c = pl.program_id(1)
    nj = RW_HEAD_DIM // SUBLANES
    first8 = (lax.broadcasted_iota(jnp.int32, (SUBLANES, LANES), 1) // RW_HEADS) % 2 == 0

    @pl.when(c == 0)
    def _():
        s_ref[...] = s0_ref[0]

    def unpack(t, slot):
        for pair, src_ref in enumerate((wa_ref, bk_ref)):
            for jb in range(nj):
                rows = pl.ds(jb * SUBLANES, SUBLANES)
                x = src_ref[0, t, rows, :]
                vec_ref[slot, 2 * pair, rows, :] = jnp.where(first8, x, pltpu.roll(x, RW_HEADS, 1))
                vec_ref[slot, 2 * pair + 1, rows, :] = jnp.where(first8, pltpu.roll(x, LANES - RW_HEADS, 1), x)

    def step(t, slot):
        unpack(jnp.minimum(t + 1, tc - 1), 1 - slot)

        def vec(which, jb):
            rows = pl.ds(jb * SUBLANES, SUBLANES)
            return rr_ref[0, t, rows, :] if which == 4 else vec_ref[slot, which, rows, :]

        for i0 in range(0, SCAN_ROWS, SCAN_ROW_GROUP):
            rows = range(i0, i0 + SCAN_ROW_GROUP)
            s = {i: [s_ref[i, jb * SUBLANES:(jb + 1) * SUBLANES, :] for jb in range(nj)] for i in rows}
            acc = {}
            for jb in range(nj):
                a = vec(1, jb)
                for i in rows:
                    acc[i] = s[i][jb] * a if jb == 0 else acc[i] + s[i][jb] * a
            sa = {i: _sublane_allsum(acc[i]) for i in rows}
            vb = {i: jnp.broadcast_to(v_ref[0, t, i:i + 1, :], (SUBLANES, LANES)) for i in rows}
            accy = {}
            for jb in range(nj):
                w, b, k, r = vec(0, jb), vec(2, jb), vec(3, jb), vec(4, jb)
                for i in rows:
                    sn = s[i][jb] * w + sa[i] * b + vb[i] * k
                    s_ref[i, jb * SUBLANES:(jb + 1) * SUBLANES, :] = sn
                    accy[i] = sn * r if jb == 0 else accy[i] + sn * r
            for i in rows:
                y_ref[0, t, i:i + 1, :] = _sublane_allsum(accy[i])[0:1, :]

    def two_steps(t2, carry):
        step(2 * t2, 0)
        step(2 * t2 + 1, 1)
        return carry

    unpack(0, 0)
    lax.fori_loop(0, tc // 2, two_steps, 0)

    @pl.when(c == pl.num_programs(1) - 1)
    def _():
        sout_ref[0] = s_ref[...]


def _rwkv_scan(wa, bk, rr, vvec, s0, tc):
    g, t = wa.shape[0], wa.shape[1]
    jspec = pl.BlockSpec((1, tc, RW_HEAD_DIM, LANES), lambda gi, c: (gi, c, 0, 0))
    return pl.pallas_call(
        functools.partial(_rwkv_scan_kernel, tc=tc),
        grid=(g, t // tc),
        in_specs=[jspec, jspec, jspec,
                  pl.BlockSpec((1, tc, SCAN_ROWS, LANES), lambda gi, c: (gi, c, 0, 0)),
                  pl.BlockSpec((1, SCAN_ROWS, RW_HEAD_DIM, LANES), lambda gi, c: (gi, 0, 0, 0))],
        out_specs=[pl.BlockSpec((1, tc, SCAN_ROWS, LANES), lambda gi, c: (gi, c, 0, 0)),
                   pl.BlockSpec((1, SCAN_ROWS, RW_HEAD_DIM, LANES), lambda gi, c: (gi, 0, 0, 0))],
        out_shape=[jax.ShapeDtypeStruct((g, t, SCAN_ROWS, LANES), F32),
                   jax.ShapeDtypeStruct((g, SCAN_ROWS, RW_HEAD_DIM, LANES), F32)],
        scratch_shapes=[pltpu.VMEM((SCAN_ROWS, RW_HEAD_DIM, LANES), F32), pltpu.VMEM((2, 4, RW_HEAD_DIM, LANES), F32)],
        compiler_params=_cparams("arbitrary", "arbitrary"),
        name="rwkv_scan",
    )(wa, bk, rr, vvec, s0)


def _to_scan_pair(x, b):
    t, g = x.shape[0], b // SCAN_SEQS
    x = jnp.swapaxes(x.reshape(t, g, LANES, RW_HEAD_DIM), 2, 3)
    return x.reshape(1, t, RW_HEAD_DIM, LANES) if g == 1 else jnp.swapaxes(x, 0, 1)


def _to_scan_rows(x, b):
    t, g = x.shape[0], b // SCAN_SEQS
    x = x.reshape(t, g, SCAN_SEQS, RW_HEADS, 2, SCAN_ROWS)
    return jnp.transpose(x, (1, 0, 5, 2, 4, 3)).reshape(g, t, SCAN_ROWS, LANES)


def _from_scan_rows(y):
    g, t = y.shape[0], y.shape[1]
    y = y.reshape(g, t, SCAN_ROWS, SCAN_SEQS, 2, RW_HEADS)
    return jnp.transpose(y, (1, 0, 3, 5, 4, 2)).reshape(t, g * SCAN_SEQS * RW_WIDTH)


def _state_to_scan(s):
    b = s.shape[0]
    g = b // SCAN_SEQS
    s = s.reshape(g, SCAN_SEQS, RW_HEADS, 2, SCAN_ROWS, RW_HEAD_DIM)
    return jnp.transpose(s, (0, 4, 5, 1, 3, 2)).reshape(g, SCAN_ROWS, RW_HEAD_DIM, LANES)


def _state_from_scan(s, b):
    g = s.shape[0]
    s = s.reshape(g, SCAN_ROWS, RW_HEAD_DIM, SCAN_SEQS, 2, RW_HEADS)
    return jnp.transpose(s, (0, 3, 5, 4, 1, 2)).reshape(b, RW_HEADS, RW_HEAD_DIM, RW_HEAD_DIM)


def _gdn_prep_kernel(x_ref, init_ref, ab_ref, cw_ref, alog_ref, dt_ref, ones_ref, q_ref, k_ref, v_ref, gb_ref,
                     ext_ref, *, halo, bs, tm):
    cur = x_ref[...]
    _stage_tile(ext_ref, init_ref, cur, halo, tm)
    conv = cur * cw_ref[GDN_CONV - 1:GDN_CONV, :]
    for s in range(1, GDN_CONV):
        off = halo - s * bs
        conv = conv + ext_ref[off:off + tm, :] * cw_ref[GDN_CONV - 1 - s:GDN_CONV - s, :]
    qkv = conv * _sigmoid(conv)
    q = qkv[:, 0:GDN_WIDTH]
    k = qkv[:, GDN_WIDTH:2 * GDN_WIDTH]
    ones = ones_ref[...]
    q_ref[...] = q * lax.rsqrt(_seg_sum(q * q, ones) + 1e-6) * (GDN_HEAD_DIM ** -0.5)
    k_ref[...] = k * lax.rsqrt(_seg_sum(k * k, ones) + 1e-6)
    v_ref[...] = qkv[:, 2 * GDN_WIDTH:3 * GDN_WIDTH]
    ab = ab_ref[...]
    g = -jnp.exp(alog_ref[...]) * _softplus(ab + dt_ref[...])
    lane = lax.broadcasted_iota(jnp.int32, ab.shape, 1)
    gb_ref[...] = jnp.where(lane < GDN_HEADS, g, _sigmoid(ab))


def _gdn_prep(qkv_raw, init, ab, p, nb, tm, bs):
    n = qkv_raw.shape[0]
    nt = n // (nb * tm)
    halo = init.shape[1]
    row = lambda w: pl.BlockSpec((tm, w), lambda b, t: (b * nt + t, 0))
    consts = [p["gdn_conv_w"], p["gdn_alog"], p["gdn_dt"], p["ones_gdn"]]
    return pl.pallas_call(
        functools.partial(_gdn_prep_kernel, halo=halo, bs=bs, tm=tm),
        grid=(nb, nt),
        in_specs=[row(GDN_CONV_CH), pl.BlockSpec((1, halo, GDN_CONV_CH), lambda b, t: (b, 0, 0)), row(LANES)]
        + [_const_spec(c.shape) for c in consts],
        out_specs=[row(GDN_WIDTH)] * 3 + [row(LANES)],
        out_shape=[jax.ShapeDtypeStruct((n, GDN_WIDTH), F32)] * 3 + [jax.ShapeDtypeStruct((n, LANES), F32)],
        scratch_shapes=[pltpu.VMEM((halo + tm, GDN_CONV_CH), F32)],
        compiler_params=_cparams("arbitrary", "arbitrary"),
        name="gdn_prep",
    )(qkv_raw, init, ab, *consts)


def _unit_lower_inverses(a_list, eye, chunk):
    n = [-a for a in a_list]
    inv = [eye + x for x in n]
    for _ in range(int(math.log2(chunk)) - 1):
        ns = [_split_bf16(x) for x in n]
        n = [_dot3(x, x) for x in ns]
        ns = [_split_bf16(x) for x in n]
        inv = [x + _dot3(_split_bf16(x), y) for x, y in zip(inv, ns)]
    return inv


def _gdn_chunk_kernel(q_ref, k_ref, v_ref, gb_ref, s0_ref, o_ref, sout_ref, s_ref, *, chunk, nbb, group):
    ci = pl.program_id(1)

    @pl.when(ci == 0)
    def _():
        s_ref[...] = s0_ref[...]

    rows = lax.broadcasted_iota(jnp.int32, (chunk, chunk), 0)
    cols = lax.broadcasted_iota(jnp.int32, (chunk, chunk), 1)
    causal = rows >= cols
    strict = rows > cols
    eye = (rows == cols).astype(F32)
    tri = causal.astype(F32)

    def one_group(gi, carry):
        units = [(gi * group + j, h) for j in range(group) for h in range(GDN_HEADS)]
        lanes = lambda h: slice(h * GDN_HEAD_DIM, (h + 1) * GDN_HEAD_DIM)
        s = [s_ref[b, h] for b, h in units]
        q = [q_ref[b, :, lanes(h)] for b, h in units]
        k = [k_ref[b, :, lanes(h)] for b, h in units]
        v = [v_ref[b, :, lanes(h)] for b, h in units]
        gb = [gb_ref[gi * group + j] for j in range(group)]
        gcs = [_dot_hi(tri, x) for x in gb]
        gc = [gcs[i // GDN_HEADS][:, h:h + 1] for i, (_, h) in enumerate(units)]
        beta = [gb[i // GDN_HEADS][:, GDN_HEADS + h:GDN_HEADS + h + 1] for i, (_, h) in enumerate(units)]
        decay = [jnp.exp(jnp.where(causal, x - jnp.sum(eye * x, axis=0, keepdims=True), -jnp.inf)) for x in gc]
        kb = [x * y for x, y in zip(k, beta)]
        a = [jnp.where(strict, _dot_nt(x, y) * d, 0.0) for x, y, d in zip(kb, k, decay)]
        inv = _unit_lower_inverses(a, eye, chunk)
        egc = [jnp.exp(x) for x in gc]
        rhs = [jnp.concatenate([x * bt, y * e], axis=1) for x, bt, y, e in zip(v, beta, kb, egc)]
        sol = [_dot3(_split_bf16(x), _split_bf16(y)) for x, y in zip(inv, rhs)]
        wS = [_dot(x[:, GDN_HEAD_DIM:2 * GDN_HEAD_DIM], y) for x, y in zip(sol, s)]
        v_new = [x[:, 0:GDN_HEAD_DIM] - y for x, y in zip(sol, wS)]
        qk = [_dot_nt(x, y) * d for x, y, d in zip(q, k, decay)]
        o_state = [_dot(x * e, y) for x, e, y in zip(q, egc, s)]
        o_chunk = [_dot(x, y) for x, y in zip(qk, v_new)]
        g_last = [x[chunk - 1:chunk, :] for x in gc]
        kd_t = [(x * jnp.exp(gl - g)).T for x, gl, g in zip(k, g_last, gc)]
        s_add = [_dot(x, y) for x, y in zip(kd_t, v_new)]
        for i, (b, h) in enumerate(units):
            o_ref[b, :, lanes(h)] = o_state[i] + o_chunk[i]
            s_ref[b, h] = s[i] * jnp.exp(g_last[i]) + s_add[i]
        return carry

    if nbb == group:
        one_group(0, 0)
    else:
        lax.fori_loop(0, nbb // group, one_group, 0)

    @pl.when(ci == pl.num_programs(1) - 1)
    def _():
        sout_ref[...] = s_ref[...]


def _gdn_chunks(q, k, v, gb, s0, chunk):
    b, t_len, _ = q.shape
    nbb = min(b, GDN_SEQ_BLOCK)
    group = min(nbb, GDN_SEQ_GROUP)
    blk = lambda w: pl.BlockSpec((nbb, chunk, w), lambda bi, ci: (bi, ci, 0))
    st = pl.BlockSpec((nbb, GDN_HEADS, GDN_HEAD_DIM, GDN_HEAD_DIM), lambda bi, ci: (bi, 0, 0, 0))
    return pl.pallas_call(
        functools.partial(_gdn_chunk_kernel, chunk=chunk, nbb=nbb, group=group),
        grid=(b // nbb, t_len // chunk),
        in_specs=[blk(GDN_WIDTH)] * 3 + [blk(LANES), st],
        out_specs=[blk(GDN_WIDTH), st],
        out_shape=[jax.ShapeDtypeStruct(q.shape, F32), jax.ShapeDtypeStruct(s0.shape, F32)],
        scratch_shapes=[pltpu.VMEM((nbb, GDN_HEADS, GDN_HEAD_DIM, GDN_HEAD_DIM), F32)],
        compiler_params=_cparams("arbitrary", "arbitrary"),
        name="gdn_chunks",
    )(q, k, v, gb, s0)


def _route(logits, count_ref):
    lane = lax.broadcasted_iota(jnp.int32, logits.shape, 1)
    neg = -jnp.inf
    is_group = (lane >= N_EXPERTS) & (lane < N_EXPERTS + N_GROUPS)
    gl = jnp.where(is_group, logits, neg)
    gmax = jnp.max(gl, axis=-1, keepdims=True)
    gp = 1.0 / jnp.sum(jnp.exp(gl - gmax), axis=-1, keepdims=True)
    gidx = jnp.min(jnp.where(gl == gmax, lane, LANES), axis=-1, keepdims=True) - N_EXPERTS
    lo = gidx * EXPERTS_PER_GROUP
    el = jnp.where((lane >= lo) & (lane < lo + EXPERTS_PER_GROUP), logits, neg)
    m1 = jnp.max(el, axis=-1, keepdims=True)
    i1 = jnp.min(jnp.where(el == m1, lane, LANES), axis=-1, keepdims=True)
    el2 = jnp.where(lane == i1, neg, el)
    m2 = jnp.max(el2, axis=-1, keepdims=True)
    i2 = jnp.min(jnp.where(el2 == m2, lane, LANES), axis=-1, keepdims=True)
    e2 = jnp.exp(m2 - m1)
    den = 1.0 + e2
    tm = logits.shape[0]
    hit1 = lane == i1
    hit2 = lane == i2
    chosen = jnp.where(hit1 | hit2, 1.0, 0.0)
    before = lax.broadcasted_iota(jnp.int32, (tm, tm), 0) > lax.broadcasted_iota(jnp.int32, (tm, tm), 1)
    seen = jnp.dot(before.astype(BF16), chosen.astype(BF16), preferred_element_type=F32) + count_ref[...]
    rank1 = jnp.sum(jnp.where(hit1, seen, 0.0), axis=-1, keepdims=True)
    rank2 = jnp.sum(jnp.where(hit2, seen, 0.0), axis=-1, keepdims=True)
    count_ref[...] += jnp.sum(chosen, axis=0, keepdims=True)
    fields = (i1.astype(F32), i2.astype(F32), gp / den, gp * e2 / den, rank1, rank2)
    out = jnp.zeros(logits.shape, F32)
    for j, f in enumerate(fields):
        out = jnp.where(lane == j, f, out)
    return out


def _merge_kernel(y_ref, r_ref, k_ref, v_ref, g_ref, o_ref, z_ref, gate_ref, h_ref, lnw_ref, lnb_ref, rk_ref,
                  ones_ref, gnw_ref, woa_ref, wob_ref, wo_ref, n2_ref, rw_ref, rb_ref, h1_ref, xt_ref, mg_ref, cnt_ref):
    @pl.when(pl.program_id(0) == 0)
    def _():
        cnt_ref[...] = jnp.zeros_like(cnt_ref)

    tm = h_ref.shape[0]
    halves = [pl.ds(0, tm // 2), pl.ds(tm // 2, tm // 2)]
    ones = ones_ref[...]
    inv_n = 1.0 / RW_HEAD_DIM
    y = [y_ref[p, :] for p in halves]
    mu = [_seg_sum(a, ones) * inv_n for a in y]
    rkk = [_seg_sum(r_ref[p, :] * k_ref[p, :] * rk_ref[...], ones) for p in halves]
    yc = [a - m for a, m in zip(y, mu)]
    var = [_seg_sum(c * c, ones) * inv_n for c in yc]
    ya = [(c * lax.rsqrt(s + RW_GN_EPS) * lnw_ref[...] + lnb_ref[...] + b * v_ref[p, :]) * g_ref[p, :]
          for c, s, b, p in zip(yc, var, rkk, halves)]

    gnw = gnw_ref[...]
    yb = []
    for p in halves:
        z = z_ref[p, :]
        heads = [_rmsnorm(o_ref[p, h * GDN_HEAD_DIM:(h + 1) * GDN_HEAD_DIM], gnw) for h in range(GDN_HEADS)]
        yb.append(jnp.concatenate(heads, axis=1) * (z * _sigmoid(z)))

    out_a = [_dot(a, woa_ref[...]) for a in ya]
    out_b = [_dot(b, wob_ref[...]) for b in yb]
    merged = [_sigmoid(gate_ref[p, 0:D_MODEL]) * a + _sigmoid(gate_ref[p, D_MODEL:2 * D_MODEL]) * b
              for p, a, b in zip(halves, out_a, out_b)]
    h1 = [h_ref[p, :] + _dot(m, wo_ref[...]) for p, m in zip(halves, merged)]
    xt = [_rmsnorm(a, n2_ref[...]) for a in h1]
    for p, a, b in zip(halves, h1, xt):
        h1_ref[p, :] = a
        xt_ref[p, :] = b
    logits = jnp.concatenate([_dot_f32(a, rw_ref[...]) for a in xt], axis=0) + rb_ref[...]
    mg_ref[...] = _route(logits, cnt_ref)


def _merge(y, rr, bk, v, g, o, z, gates, h, p, tm, wide_nt):
    n = h.shape[0]
    row = lambda w: pl.BlockSpec((tm, w), lambda i: (i, 0))
    if wide_nt is None:
        part = lambda per_seq, k: pl.BlockSpec((tm, RW_WIDTH), lambda i: (i, k))
    else:
        part = lambda per_seq, k: pl.BlockSpec((tm, RW_WIDTH), lambda i: (i % wide_nt, (i // wide_nt) * per_seq + k))
    consts = [p["rw_lnx_w"], p["rw_lnx_b"], p["rw_r_k"], p["ones_rw"], p["gdn_norm_w"], p["w_oA"], p["w_oB"],
              p["w_o"], p["norm2_w"], p["router_w"], p["router_b"]]
    return pl.pallas_call(
        _merge_kernel,
        grid=(n // tm,),
        in_specs=[part(1, 0), part(2, 0), part(2, 1), part(1, 0)] + [row(RW_WIDTH)] * 3
        + [row(2 * D_MODEL), row(D_MODEL)] + [_const_spec(c.shape) for c in consts],
        out_specs=[row(D_MODEL), row(D_MODEL), row(LANES), _const_spec((1, LANES))],
        out_shape=[jax.ShapeDtypeStruct((n, D_MODEL), F32), jax.ShapeDtypeStruct((n, D_MODEL), F32),
                   jax.ShapeDtypeStruct((n, LANES), F32), jax.ShapeDtypeStruct((1, LANES), F32)],
        compiler_params=_cparams("arbitrary"),
        name="merge",
    )(y, rr, bk, v, g, o, z, gates, h, *consts)


def _start_row_gather(src_hbm, idx_ref, dst_ref, sem, n, priority):
    def body(r, carry):
        pltpu.make_async_copy(src_hbm.at[pl.ds(idx_ref[0, 0, r], 1)], dst_ref.at[pl.ds(r, 1)], sem).start(priority)
        return carry

    lax.fori_loop(0, n, body, 0, unroll=8)


def _wait_row_gather(src_hbm, dst_ref, sem, n):
    def body(r, carry):
        pltpu.make_async_copy(src_hbm.at[pl.ds(0, 1)], dst_ref.at[pl.ds(r, 1)], sem).wait()
        return carry

    lax.fori_loop(0, n, body, 0, unroll=8)


def _moe_dispatch_kernel(p1_ref, p2_ref, x_ref, zeros_hbm, xs_hbm, xbuf, sems, *, tm):
    del zeros_hbm
    i = pl.program_id(0)
    nt = pl.num_programs(0)
    slot = i % 2

    def wait_tile(s):
        def body(r, carry):
            pltpu.make_async_copy(xbuf.at[s, pl.ds(0, 1)], xs_hbm.at[pl.ds(0, 1)], sems.at[s]).wait()
            return carry

        lax.fori_loop(0, 2 * tm, body, 0, unroll=8)

    @pl.when(i >= 2)
    def _():
        wait_tile(slot)

    xbuf[slot] = x_ref[...]

    def issue(r, carry):
        src = xbuf.at[slot, pl.ds(r, 1)]
        pltpu.make_async_copy(src, xs_hbm.at[pl.ds(p1_ref[0, 0, r], 1)], sems.at[slot]).start(0)
        pltpu.make_async_copy(src, xs_hbm.at[pl.ds(p2_ref[0, 0, r], 1)], sems.at[slot]).start(1)
        return carry

    lax.fori_loop(0, tm, issue, 0, unroll=8)

    @pl.when(i == nt - 1)
    def _():
        wait_tile(slot)

        @pl.when(nt > 1)
        def _():
            wait_tile(1 - slot)


def _moe_dispatch(xt, pos1, pos2, n_sorted, tm):
    n = xt.shape[0]
    idx = pl.BlockSpec((1, 1, tm), lambda i: (i, 0, 0), memory_space=pltpu.SMEM)
    return pl.pallas_call(
        functools.partial(_moe_dispatch_kernel, tm=tm),
        grid=(n // tm,),
        in_specs=[idx, idx, pl.BlockSpec((tm, D_MODEL), lambda i: (i, 0)), pl.BlockSpec(memory_space=pl.ANY)],
        out_specs=pl.BlockSpec(memory_space=pl.ANY),
        out_shape=jax.ShapeDtypeStruct((n_sorted, D_MODEL), F32),
        scratch_shapes=[pltpu.VMEM((2, tm, D_MODEL), F32), pltpu.SemaphoreType.DMA((2,))],
        input_output_aliases={3: 0},
        compiler_params=_cparams("arbitrary"),
        name="moe_dispatch",
    )(pos1, pos2, xt, jnp.zeros((n_sorted, D_MODEL), F32))


def _moe_experts_kernel(te_ref, tv_ref, x_ref, wg_ref, wu_ref, wd_ref, y_ref):
    i = pl.program_id(0)

    @pl.when(tv_ref[i] == 1)
    def _():
        xb = x_ref[...].astype(BF16)
        hg = jnp.dot(xb, wg_ref[0].astype(BF16), preferred_element_type=F32)
        hu = jnp.dot(xb, wu_ref[0].astype(BF16), preferred_element_type=F32)
        hid = hg * _sigmoid(hg) * hu
        y_ref[...] = jnp.dot(hid.astype(BF16), wd_ref[0].astype(BF16), preferred_element_type=F32)

    @pl.when(tv_ref[i] == 0)
    def _():
        y_ref[...] = jnp.zeros_like(y_ref)


def _moe_experts(xs, tile_expert, tile_valid, wg, wu, wd, tm):
    nt = xs.shape[0] // tm
    wspec = lambda shape: pl.BlockSpec((1,) + shape, lambda i, te, tv: (te[i], 0, 0))
    rows = pl.BlockSpec((tm, D_MODEL), lambda i, te, tv: (i, 0))
    return pl.pallas_call(
        _moe_experts_kernel,
        grid_spec=pltpu.PrefetchScalarGridSpec(
            num_scalar_prefetch=2,
            grid=(nt,),
            in_specs=[rows, wspec((D_MODEL, EXPERT_FF)), wspec((D_MODEL, EXPERT_FF)), wspec((EXPERT_FF, D_MODEL))],
            out_specs=rows,
        ),
        out_shape=jax.ShapeDtypeStruct(xs.shape, F32),
        compiler_params=_cparams("arbitrary"),
        name="moe_experts",
    )(tile_expert, tile_valid, xs, wg, wu, wd)


def _moe_combine_kernel(cur1_ref, cur2_ref, nxt1_ref, nxt2_ref, ys_hbm, h1_ref, mg_ref, nf_ref, o_ref, buf1, buf2,
                        sems, *, tm):
    i = pl.program_id(0)
    nt = pl.num_programs(0)
    slot = i % 2

    @pl.when(i == 0)
    def _():
        _start_row_gather(ys_hbm, cur1_ref, buf1.at[0], sems.at[0, 0], tm, 0)
        _start_row_gather(ys_hbm, cur2_ref, buf2.at[0], sems.at[1, 0], tm, 1)

    @pl.when(i + 1 < nt)
    def _():
        _start_row_gather(ys_hbm, nxt1_ref, buf1.at[1 - slot], sems.at[0, 1 - slot], tm, 0)
        _start_row_gather(ys_hbm, nxt2_ref, buf2.at[1 - slot], sems.at[1, 1 - slot], tm, 1)

    _wait_row_gather(ys_hbm, buf1.at[slot], sems.at[0, slot], tm)
    _wait_row_gather(ys_hbm, buf2.at[slot], sems.at[1, slot], tm)
    mg = mg_ref[...]
    moe = mg[:, 2:3] * buf1[slot] + mg[:, 3:4] * buf2[slot]
    o_ref[...] = _rmsnorm(h1_ref[...] + moe, nf_ref[...])


def _moe_combine(ys, pos1, pos2, h1, mg, nf, tm):
    n = h1.shape[0]
    nt = n // tm
    cur = pl.BlockSpec((1, 1, tm), lambda i: (i, 0, 0), memory_space=pltpu.SMEM)
    nxt = pl.BlockSpec((1, 1, tm), lambda i: (jnp.minimum(i + 1, nt - 1), 0, 0), memory_space=pltpu.SMEM)
    row = lambda w: pl.BlockSpec((tm, w), lambda i: (i, 0))
    return pl.pallas_call(
        functools.partial(_moe_combine_kernel, tm=tm),
        grid=(nt,),
        in_specs=[cur, cur, nxt, nxt, pl.BlockSpec(memory_space=pl.ANY), row(D_MODEL), row(LANES),
                  _const_spec((1, D_MODEL))],
        out_specs=row(D_MODEL),
        out_shape=jax.ShapeDtypeStruct((n, D_MODEL), F32),
        scratch_shapes=[pltpu.VMEM((2, tm, D_MODEL), F32), pltpu.VMEM((2, tm, D_MODEL), F32),
                        pltpu.SemaphoreType.DMA((2, 2))],
        compiler_params=_cparams("arbitrary"),
        name="moe_combine",
    )(pos1, pos2, pos1, pos2, ys, h1, mg, nf)


def _moe(xt, h1, mg, counts, wg, wu, wd, nf, tm_expert, tm_token):
    n = xt.shape[0]
    i32 = jnp.int32
    expert = jnp.arange(N_EXPERTS, dtype=i32)
    cnt = counts[0, :N_EXPERTS].astype(i32)
    padded = (cnt + tm_expert - 1) // tm_expert * tm_expert
    ends = jnp.sum(jnp.where(expert[:, None] >= expert[None, :], padded[None, :], 0), axis=1)
    starts = ends - padded
    start_of = lambda e: jnp.sum(jnp.where(e[:, None] == expert[None, :], starts[None, :], 0), axis=1)
    pos1 = start_of(mg[:, 0].astype(i32)) + mg[:, 4].astype(i32)
    pos2 = start_of(mg[:, 1].astype(i32)) + mg[:, 5].astype(i32)
    nt = (2 * n + N_EXPERTS * (tm_expert - 1)) // tm_expert + 1
    tile_start = jnp.arange(nt, dtype=i32) * tm_expert
    tile_valid = (tile_start < jnp.sum(padded)).astype(i32)
    tile_expert = jnp.sum((tile_start[:, None] >= ends[None, :]).astype(i32), axis=1)
    last_used = jnp.max(jnp.where(padded > 0, expert, 0))
    tile_expert = jnp.where(tile_valid == 1, tile_expert, last_used)
    shape = (n // tm_token, 1, tm_token)
    pos1, pos2 = pos1.reshape(shape), pos2.reshape(shape)
    xs = _moe_dispatch(xt, pos1, pos2, nt * tm_expert, tm_token)
    ys = _moe_experts(xs, tile_expert, tile_valid, wg, wu, wd, tm_expert)
    return _moe_combine(ys, pos1, pos2, h1, mg, nf, tm_token)


def _pad_rows(x, rows):
    return jnp.pad(x, ((0, rows - x.shape[0]),) + ((0, 0),) * (x.ndim - 1))


def _mixer(x_rows, p, *, batch, t_len, time_major, tm, rw_prev, conv_prev, wkv0, gdn0, chunk, with_merge):
    n = batch * t_len
    xn, rw_cur, qkv_raw, z, gates, ab = _in_proj(x_rows, p["norm1_w"], p["w_in"], min(tm, n))

    if time_major:
        nb, bs, tmt = 1, batch, n
        rw_init = rw_prev[None]
        conv_init = jnp.transpose(conv_prev, (1, 0, 2)).reshape(1, (GDN_CONV - 1) * batch, GDN_CONV_CH)
    else:
        nb, bs, tmt = batch, 1, min(tm, t_len)
        rw_init = jnp.pad(rw_prev[:, None, :], ((0, 0), (SUBLANES - 1, 0), (0, 0)))
        conv_init = jnp.pad(conv_prev, ((0, 0), (SUBLANES - (GDN_CONV - 1), 0), (0, 0)))

    wide = not time_major
    wa, bk, rr, v, g = _rwkv_prep(rw_cur, rw_init, p, nb, tmt, bs, wide)
    q, kg, vg, gb = _gdn_prep(qkv_raw, conv_init, ab, p, nb, tmt, bs)

    def seq_major(a):
        if time_major:
            return jnp.transpose(a.reshape(t_len, batch, a.shape[-1]), (1, 0, 2))
        return a.reshape(batch, t_len, a.shape[-1])

    def rows_like_x(a):
        if time_major:
            a = jnp.transpose(a, (1, 0, 2))
        return a.reshape(n, a.shape[-1])

    t_pad = -(-t_len // chunk) * chunk
    def gdn_seq(a):
        a = seq_major(a)
        if t_pad != t_len:
            a = jnp.pad(a, ((0, 0), (0, t_pad - t_len), (0, 0)))
        return a
    o_pad, gdn = _gdn_chunks(gdn_seq(q), gdn_seq(kg), gdn_seq(vg), gdn_seq(gb), gdn0, chunk)

    bpad = -(-batch // SCAN_SEQS) * SCAN_SEQS

    def by_time(a):
        a = a.reshape(t_len, -1)
        return jnp.pad(a, ((0, 0), (0, a.shape[1] // batch * (bpad - batch))))

    scan_in = tuple(_to_scan_pair(by_time(a), bpad) for a in (wa, bk, rr)) + (_to_scan_rows(by_time(v), bpad),)
    scan_in, o_pad = lax.optimization_barrier((scan_in, o_pad))
    tc = min(t_len, 32)
    y_scan, s_scan = _rwkv_scan(*scan_in, _state_to_scan(_pad_rows(wkv0, bpad)), tc)
    y_rw = _from_scan_rows(y_scan)[:, :batch * RW_WIDTH]
    if time_major:
        y_rw = y_rw.reshape(n, RW_WIDTH)
    wkv = _state_from_scan(s_scan, bpad)[:batch]
    o = rows_like_x(o_pad[:, :t_len])

    out = dict(xn=xn, rw_cur=rw_cur, qkv_raw=qkv_raw, wkv=wkv, gdn=gdn)
    if with_merge:
        out["h1"], out["xt"], out["mg"], out["counts"] = _merge(y_rw, rr, bk, v, g, o, z, gates, x_rows, p, min(tm, n),
                                                                 t_len // tmt if wide else None)
    return out


def kernel(x_prompt, x_sample, state_shift, state_wkv, state_conv, state_gdn, meta_tokens, norm1_w, w_in, mu_shift, rw_w0, rw_w2, rw_a0, rw_a2, rw_g2, rw_k_k, rw_k_a, rw_r_k, rw_lnx_w, rw_lnx_b, gdn_conv_w, gdn_A_log, gdn_dt_bias, gdn_norm_w, w_oA, w_oB, w_o, norm2_w, router_g, router_g_b, router_e, router_e_b, moe_w_gate, moe_w_up, moe_w_down, norm_f_w):
    bp, sp, _ = x_prompt.shape
    bsm, ss, _ = x_sample.shape
    row = lambda a: a[0].reshape(1, -1)
    w0 = w_in[0]
    gcol = RW_COLS + 4 * GDN_WIDTH + 2 * GDN_HEADS
    w_packed = jnp.concatenate(
        [w0[:, :RW_COLS + 4 * GDN_WIDTH], w0[:, gcol:], w0[:, RW_COLS + 4 * GDN_WIDTH:gcol],
         jnp.zeros((D_MODEL, LANES - 2 * GDN_HEADS), F32)], axis=1).astype(BF16)
    lane_pad = lambda a: jnp.pad(a, ((0, 0), (0, LANES - a.shape[1])))
    p = dict(
        norm1_w=row(norm1_w), w_in=w_packed, mu_shift=row(mu_shift), rw_w0=row(rw_w0), rw_w2=rw_w2[0],
        rw_a0=row(rw_a0), rw_a2=rw_a2[0], rw_g2=rw_g2[0], rw_k_k=row(rw_k_k), rw_k_a=row(rw_k_a),
        rw_r_k=row(rw_r_k), rw_lnx_w=row(rw_lnx_w), rw_lnx_b=row(rw_lnx_b),
        ones_rw=_block_ones(RW_WIDTH, RW_HEAD_DIM), ones_gdn=_block_ones(GDN_WIDTH, GDN_HEAD_DIM),
        gdn_conv_w=gdn_conv_w[0], gdn_alog=lane_pad(row(gdn_A_log)), gdn_dt=lane_pad(row(gdn_dt_bias)),
        gdn_norm_w=row(gdn_norm_w), w_oA=w_oA[0].astype(BF16), w_oB=w_oB[0].astype(BF16),
        w_o=w_o[0].astype(BF16), norm2_w=row(norm2_w),
        router_w=lane_pad(jnp.concatenate([router_e[0], router_g[0]], axis=1)),
        router_b=lane_pad(jnp.concatenate([row(router_e_b), row(router_g_b)], axis=1)),
    )
    nf = norm_f_w.reshape(1, -1)
    dt = x_prompt.dtype

    meta = _mixer(meta_tokens.astype(dt), p, batch=1, t_len=N_META, time_major=False, tm=N_META,
                  rw_prev=jnp.zeros((1, RW_COLS), dt), conv_prev=jnp.zeros((1, GDN_CONV - 1, GDN_CONV_CH), dt),
                  wkv0=jnp.zeros((1, RW_HEADS, RW_HEAD_DIM, RW_HEAD_DIM), dt),
                  gdn0=jnp.zeros((1, GDN_HEADS, GDN_HEAD_DIM, GDN_HEAD_DIM), dt), chunk=N_META, with_merge=False)
    rep = lambda a: jnp.broadcast_to(a, (bp,) + a.shape[1:])

    prompt = _mixer(x_prompt.reshape(bp * sp, D_MODEL), p, batch=bp, t_len=sp, time_major=False, tm=256,
                    rw_prev=rep(meta["rw_cur"][N_META - 1:]), conv_prev=rep(meta["qkv_raw"][None, N_META - 3:]),
                    wkv0=rep(meta["wkv"]), gdn0=rep(meta["gdn"]), chunk=GDN_CHUNK, with_merge=True)

    xs_rows = jnp.transpose(x_sample, (1, 0, 2)).reshape(ss * bsm, D_MODEL)
    prev_s = _in_proj(state_shift[0].astype(dt), p["norm1_w"], w_packed, bsm, norm=False)[1]
    sample = _mixer(xs_rows, p, batch=bsm, t_len=ss, time_major=True, tm=256,
                    rw_prev=prev_s, conv_prev=state_conv[0].astype(dt), wkv0=state_wkv[0], gdn0=state_gdn[0],
                    chunk=SUBLANES, with_merge=True)

    experts = (moe_w_gate[0], moe_w_up[0], moe_w_down[0])
    y_p = _moe(prompt["xt"], prompt["h1"], prompt["mg"], prompt["counts"], *experts, nf, 256, 256)
    y_s = _moe(sample["xt"], sample["h1"], sample["mg"], sample["counts"], *experts, nf, 128, 256)

    y_prompt = y_p.reshape(bp, sp, D_MODEL)
    y_sample = jnp.transpose(y_s.reshape(ss, bsm, D_MODEL), (1, 0, 2))
    shift_p = prompt["xn"].reshape(bp, sp, D_MODEL)[:, -1][None]
    conv_p = prompt["qkv_raw"].reshape(bp, sp, GDN_CONV_CH)[:, sp - (GDN_CONV - 1):][None]
    shift_s = sample["xn"].reshape(ss, bsm, D_MODEL)[-1][None].astype(state_shift.dtype)
    xpad_s = jnp.concatenate([state_conv[0].astype(dt),
                              jnp.transpose(sample["qkv_raw"].reshape(ss, bsm, GDN_CONV_CH), (1, 0, 2))], axis=1)
    conv_s = xpad_s[:, ss:][None].astype(state_conv.dtype)
    return (y_prompt, y_sample, shift_p, prompt["wkv"][None], conv_p, prompt["gdn"][None],
            shift_s, sample["wkv"][None].astype(state_wkv.dtype), conv_s, sample["gdn"][None].astype(state_gdn.dtype))
```

```python
import functools
import math

import jax
import jax.numpy as jnp
from jax import lax
from jax.experimental import pallas as pl
from jax.experimental.pallas import tpu as pltpu

F32 = jnp.float32
BF16 = jnp.bfloat16
HIGHEST = lax.Precision.HIGHEST

D_MODEL = 1024
N_META = 16
RW_WIDTH = 512
RW_HEAD_DIM = 64
RW_HEADS = 8
RW_DECAY_LORA = 64
RW_AAA_LORA = 64
RW_GATE_LORA = 128
RW_COLS = 3 * RW_WIDTH + RW_DECAY_LORA + RW_AAA_LORA + RW_GATE_LORA
RW_GN_EPS = RW_HEAD_DIM * 1e-5
GDN_WIDTH = 512
GDN_HEAD_DIM = 128
GDN_HEADS = 4
GDN_CONV = 4
GDN_CONV_CH = 3 * GDN_WIDTH
GDN_CHUNK = 64
N_GROUPS = 4
EXPERTS_PER_GROUP = 8
N_EXPERTS = 32
EXPERT_FF = 512
RMS_EPS = 1e-6

LANES = 128
SUBLANES = 8
VMEM_LIMIT_BYTES = 56 * 1024 * 1024

COL_RW = 0
COL_QKV = COL_RW + RW_COLS
COL_Z = COL_QKV + GDN_CONV_CH
COL_GATE = COL_Z + GDN_WIDTH
COL_AB = COL_GATE + 2 * D_MODEL
IN_COLS_PACKED = COL_AB + LANES

SCAN_SEQS = 8
SCAN_ROWS = RW_HEAD_DIM // 2
SCAN_ROW_GROUP = 1

GDN_SEQ_BLOCK = 8
GDN_SEQ_GROUP = 8


def _cparams(*sem):
    return pltpu.CompilerParams(dimension_semantics=sem, vmem_limit_bytes=VMEM_LIMIT_BYTES)


def _const_spec(shape):
    zeros = (0,) * len(shape)
    return pl.BlockSpec(shape, lambda *_: zeros)


def _dot(a, b):
    return jnp.dot(a.astype(BF16), b.astype(BF16), preferred_element_type=F32)


def _dot_hi(a, b):
    return jnp.dot(a, b, preferred_element_type=F32, precision=HIGHEST)


def _split_bf16(a):
    hi = a.astype(BF16)
    return hi, (a - hi.astype(F32)).astype(BF16)


def _dot3(a, b):
    d = lambda x, y: jnp.dot(x, y, preferred_element_type=F32)
    return d(a[0], b[0]) + d(a[0], b[1]) + d(a[1], b[0])


def _dot_f32(a, b):
    return _dot3(_split_bf16(a), _split_bf16(b))


def _seg_sum(x, ones):
    hi, lo = _split_bf16(x)
    return jnp.dot(hi, ones, preferred_element_type=F32) + jnp.dot(lo, ones, preferred_element_type=F32)


def _dot_nt(a, b):
    dims = (((1,), (1,)), ((), ()))
    return lax.dot_general(a.astype(BF16), b.astype(BF16), dims, preferred_element_type=F32)


def _sigmoid(x):
    return 1.0 / (1.0 + jnp.exp(-x))


def _softplus(x):
    return jnp.maximum(x, 0.0) + jnp.log1p(jnp.exp(-jnp.abs(x)))


def _rmsnorm(x, w):
    ms = jnp.mean(x * x, axis=-1, keepdims=True)
    return x * lax.rsqrt(ms + RMS_EPS) * w


def _block_ones(width, seg):
    r = jnp.arange(width) // seg
    return (r[:, None] == r[None, :]).astype(BF16)


def _in_proj_kernel(x_ref, nw_ref, w_ref, xn_ref, rw_ref, qkv_ref, z_ref, gate_ref, ab_ref, *, norm):
    x = x_ref[...]
    xn = _rmsnorm(x, nw_ref[...]) if norm else x
    xn_ref[...] = xn
    xb = xn.astype(BF16)
    rw_ref[...] = jnp.dot(xb, w_ref[:, COL_RW:COL_QKV], preferred_element_type=F32)
    qkv_ref[...] = jnp.dot(xb, w_ref[:, COL_QKV:COL_Z], preferred_element_type=F32)
    z_ref[...] = jnp.dot(xb, w_ref[:, COL_Z:COL_GATE], preferred_element_type=F32)
    gate_ref[...] = jnp.dot(xb, w_ref[:, COL_GATE:COL_AB], preferred_element_type=F32)
    ab_ref[...] = jnp.dot(xb, w_ref[:, COL_AB:IN_COLS_PACKED], preferred_element_type=F32)


def _in_proj(x, norm_w, w_packed, tm, norm=True):
    n = x.shape[0]
    widths = (D_MODEL, RW_COLS, GDN_CONV_CH, GDN_WIDTH, 2 * D_MODEL, LANES)
    row = lambda w: pl.BlockSpec((tm, w), lambda i: (i, 0))
    return pl.pallas_call(
        functools.partial(_in_proj_kernel, norm=norm),
        grid=(n // tm,),
        in_specs=[row(D_MODEL), _const_spec((1, D_MODEL)), _const_spec(w_packed.shape)],
        out_specs=[row(w) for w in widths],
        out_shape=[jax.ShapeDtypeStruct((n, w), F32) for w in widths],
        compiler_params=_cparams("arbitrary"),
        name="in_proj",
    )(x, norm_w, w_packed)


def _stage_tile(ext_ref, init_ref, cur, halo, tm):
    t = pl.program_id(1)

    @pl.when(t == 0)
    def _():
        ext_ref[0:halo, :] = init_ref[0]

    @pl.when(t > 0)
    def _():
        ext_ref[0:halo, :] = ext_ref[tm:tm + halo, :]

    ext_ref[halo:halo + tm, :] = cur


def _rwkv_prep_kernel(cur_ref, init_ref, mu_ref, w0_ref, w2_ref, a0_ref, a2_ref, g2_ref, kk_ref, ka_ref,
                      ones_ref, wa_ref, bk_ref, rr_ref, v_ref, g_ref, ext_ref, *, halo, bs, tm):
    cur = cur_ref[...]
    _stage_tile(ext_ref, init_ref, cur, halo, tm)
    prev = ext_ref[halo - bs:halo - bs + tm, :]
    zc = cur + (prev - cur) * mu_ref[...]
    c0, c1, c2 = RW_WIDTH, 2 * RW_WIDTH, 3 * RW_WIDTH
    r = zc[:, 0:c0]
    k = zc[:, c0:c1]
    v = zc[:, c1:c2]
    xw = zc[:, c2:c2 + RW_DECAY_LORA]
    xa = zc[:, c2 + RW_DECAY_LORA:c2 + RW_DECAY_LORA + RW_AAA_LORA]
    xg = zc[:, c2 + RW_DECAY_LORA + RW_AAA_LORA:RW_COLS]
    w_raw = -_softplus(-(w0_ref[...] + _dot_f32(jnp.tanh(xw), w2_ref[...]))) - 0.5
    decay = jnp.exp(-jnp.exp(w_raw))
    a = _sigmoid(a0_ref[...] + _dot_f32(xa, a2_ref[...]))
    g = _dot_f32(_sigmoid(xg), g2_ref[...])
    kkr = k * kk_ref[...]
    kk = kkr * lax.rsqrt(_seg_sum(kkr * kkr, ones_ref[...]) + 1e-6)
    wa_ref[:, 0:RW_WIDTH] = decay
    wa_ref[:, RW_WIDTH:2 * RW_WIDTH] = -kk
    bk_ref[:, 0:RW_WIDTH] = kk * a
    bk_ref[:, RW_WIDTH:2 * RW_WIDTH] = k * (1.0 + (a - 1.0) * ka_ref[...])
    rr_ref[:, 0:RW_WIDTH] = r
    rr_ref[:, RW_WIDTH:2 * RW_WIDTH] = r
    v_ref[...] = v
    g_ref[...] = g


def _rwkv_prep(rw_cur, init, p, nb, tm, bs, wide):
    n = rw_cur.shape[0]
    nt = n // (nb * tm)
    halo = init.shape[1]
    row = lambda w: pl.BlockSpec((tm, w), lambda b, t: (b * nt + t, 0))
    if wide:
        out_row = lambda w: pl.BlockSpec((tm, w), lambda b, t: (t, b))
        out_shape = lambda w: jax.ShapeDtypeStruct((n // nb, nb * w), F32)
    else:
        out_row = row
        out_shape = lambda w: jax.ShapeDtypeStruct((n, w), F32)
    consts = [p["mu_shift"], p["rw_w0"], p["rw_w2"], p["rw_a0"], p["rw_a2"], p["rw_g2"], p["rw_k_k"], p["rw_k_a"],
              p["ones_rw"]]
    return pl.pallas_call(
        functools.partial(_rwkv_prep_kernel, halo=halo, bs=bs, tm=tm),
        grid=(nb, nt),
        in_specs=[row(RW_COLS), pl.BlockSpec((1, halo, RW_COLS), lambda b, t: (b, 0, 0))]
        + [_const_spec(c.shape) for c in consts],
        out_specs=[out_row(2 * RW_WIDTH)] * 3 + [out_row(RW_WIDTH), row(RW_WIDTH)],
        out_shape=[out_shape(2 * RW_WIDTH)] * 3 + [out_shape(RW_WIDTH), jax.ShapeDtypeStruct((n, RW_WIDTH), F32)],
        scratch_shapes=[pltpu.VMEM((halo + tm, RW_COLS), F32)],
        compiler_params=_cparams("arbitrary", "arbitrary"),
        name="rwkv_prep",
    )(rw_cur, init, *consts)


def _sublane_allsum(x):
    x = x + pltpu.roll(x, 4, 0)
    x = x + pltpu.roll(x, 2, 0)
    return x + pltpu.roll(x, 1, 0)


def _rwkv_scan_kernel(wa_ref, bk_ref, rr_ref, v_ref, s0_ref, y_ref, sout_ref, s_ref, vec_ref, *, tc):
    c = pl.program_id(1)
    nj = RW_HEAD_DIM // SUBLANES
    first8 = (lax.broadcasted_iota(jnp.int32, (SUBLANES, LANES), 1) // RW_HEADS) % 2 == 0

    @pl.when(c == 0)
    def _():
        s_ref[...] = s0_ref[0]

    def unpack(t, slot):
        for pair, src_ref in enumerate((wa_ref, bk_ref)):
            for jb in range(nj):
                rows = pl.ds(jb * SUBLANES, SUBLANES)
                x = src_ref[0, t, rows, :]
                vec_ref[slot, 2 * pair, rows, :] = jnp.where(first8, x, pltpu.roll(x, RW_HEADS, 1))
                vec_ref[slot, 2 * pair + 1, rows, :] = jnp.where(first8, pltpu.roll(x, LANES - RW_HEADS, 1), x)

    def step(t, slot):
        unpack(jnp.minimum(t + 1, tc - 1), 1 - slot)

        def vec(which, jb):
            rows = pl.ds(jb * SUBLANES, SUBLANES)
            return rr_ref[0, t, rows, :] if which == 4 else vec_ref[slot, which, rows, :]

        for i0 in range(0, SCAN_ROWS, SCAN_ROW_GROUP):
            rows = range(i0, i0 + SCAN_ROW_GROUP)
            s = {i: [s_ref[i, jb * SUBLANES:(jb + 1) * SUBLANES, :] for jb in range(nj)] for i in rows}
            acc = {}
            for jb in range(nj):
                a = vec(1, jb)
                for i in rows:
                    acc[i] = s[i][jb] * a if jb == 0 else acc[i] + s[i][jb] * a
            sa = {i: _sublane_allsum(acc[i]) for i in rows}
            vb = {i: jnp.broadcast_to(v_ref[0, t, i:i + 1, :], (SUBLANES, LANES)) for i in rows}
            accy = {}
            for jb in range(nj):
                w, b, k, r = vec(0, jb), vec(2, jb), vec(3, jb), vec(4, jb)
                for i in rows:
                    sn = s[i][jb] * w + sa[i] * b + vb[i] * k
                    s_ref[i, jb * SUBLANES:(jb + 1) * SUBLANES, :] = sn
                    accy[i] = sn * r if jb == 0 else accy[i] + sn * r
            for i in rows:
                y_ref[0, t, i:i + 1, :] = _sublane_allsum(accy[i])[0:1, :]

    def two_steps(t2, carry):
        step(2 * t2, 0)
        step(2 * t2 + 1, 1)
        return carry

    unpack(0, 0)
    lax.fori_loop(0, tc // 2, two_steps, 0)

    @pl.when(c == pl.num_programs(1) - 1)
    def _():
        sout_ref[0] = s_ref[...]


def _rwkv_scan(wa, bk, rr, vvec, s0, tc):
    g, t = wa.shape[0], wa.shape[1]
    jspec = pl.BlockSpec((1, tc, RW_HEAD_DIM, LANES), lambda gi, c: (gi, c, 0, 0))
    return pl.pallas_call(
        functools.partial(_rwkv_scan_kernel, tc=tc),
        grid=(g, t // tc),
        in_specs=[jspec, jspec, jspec,
                  pl.BlockSpec((1, tc, SCAN_ROWS, LANES), lambda gi, c: (gi, c, 0, 0)),
                  pl.BlockSpec((1, SCAN_ROWS, RW_HEAD_DIM, LANES), lambda gi, c: (gi, 0, 0, 0))],
        out_specs=[pl.BlockSpec((1, tc, SCAN_ROWS, LANES), lambda gi, c: (gi, c, 0, 0)),
                   pl.BlockSpec((1, SCAN_ROWS, RW_HEAD_DIM, LANES), lambda gi, c: (gi, 0, 0, 0))],
        out_shape=[jax.ShapeDtypeStruct((g, t, SCAN_ROWS, LANES), F32),
                   jax.ShapeDtypeStruct((g, SCAN_ROWS, RW_HEAD_DIM, LANES), F32)],
        scratch_shapes=[pltpu.VMEM((SCAN_ROWS, RW_HEAD_DIM, LANES), F32), pltpu.VMEM((2, 4, RW_HEAD_DIM, LANES), F32)],
        compiler_params=_cparams("arbitrary", "arbitrary"),
        name="rwkv_scan",
    )(wa, bk, rr, vvec, s0)


def _to_scan_pair(x, b):
    t, g = x.shape[0], b // SCAN_SEQS
    x = jnp.swapaxes(x.reshape(t, g, LANES, RW_HEAD_DIM), 2, 3)
    return x.reshape(1, t, RW_HEAD_DIM, LANES) if g == 1 else jnp.swapaxes(x, 0, 1)


def _to_scan_rows(x, b):
    t, g = x.shape[0], b // SCAN_SEQS
    x = x.reshape(t, g, SCAN_SEQS, RW_HEADS, 2, SCAN_ROWS)
    return jnp.transpose(x, (1, 0, 5, 2, 4, 3)).reshape(g, t, SCAN_ROWS, LANES)


def _from_scan_rows(y):
    g, t = y.shape[0], y.shape[1]
    y = y.reshape(g, t, SCAN_ROWS, SCAN_SEQS, 2, RW_HEADS)
    return jnp.transpose(y, (1, 0, 3, 5, 4, 2)).reshape(t, g * SCAN_SEQS * RW_WIDTH)


def _state_to_scan(s):
    b = s.shape[0]
    g = b // SCAN_SEQS
    s = s.reshape(g, SCAN_SEQS, RW_HEADS, 2, SCAN_ROWS, RW_HEAD_DIM)
    return jnp.transpose(s, (0, 4, 5, 1, 3, 2)).reshape(g, SCAN_ROWS, RW_HEAD_DIM, LANES)


def _state_from_scan(s, b):
    g = s.shape[0]
    s = s.reshape(g, SCAN_ROWS, RW_HEAD_DIM, SCAN_SEQS, 2, RW_HEADS)
    return jnp.transpose(s, (0, 3, 5, 4, 1, 2)).reshape(b, RW_HEADS, RW_HEAD_DIM, RW_HEAD_DIM)


def _gdn_prep_kernel(x_ref, init_ref, ab_ref, cw_ref, alog_ref, dt_ref, ones_ref, q_ref, k_ref, v_ref, gb_ref,
                     ext_ref, *, halo, bs, tm):
    cur = x_ref[...]
    _stage_tile(ext_ref, init_ref, cur, halo, tm)
    conv = cur * cw_ref[GDN_CONV - 1:GDN_CONV, :]
    for s in range(1, GDN_CONV):
        off = halo - s * bs
        conv = conv + ext_ref[off:off + tm, :] * cw_ref[GDN_CONV - 1 - s:GDN_CONV - s, :]
    qkv = conv * _sigmoid(conv)
    q = qkv[:, 0:GDN_WIDTH]
    k = qkv[:, GDN_WIDTH:2 * GDN_WIDTH]
    ones = ones_ref[...]
    q_ref[...] = q * lax.rsqrt(_seg_sum(q * q, ones) + 1e-6) * (GDN_HEAD_DIM ** -0.5)
    k_ref[...] = k * lax.rsqrt(_seg_sum(k * k, ones) + 1e-6)
    v_ref[...] = qkv[:, 2 * GDN_WIDTH:3 * GDN_WIDTH]
    ab = ab_ref[...]
    g = -jnp.exp(alog_ref[...]) * _softplus(ab + dt_ref[...])
    lane = lax.broadcasted_iota(jnp.int32, ab.shape, 1)
    gb_ref[...] = jnp.where(lane < GDN_HEADS, g, _sigmoid(ab))


def _gdn_prep(qkv_raw, init, ab, p, nb, tm, bs):
    n = qkv_raw.shape[0]
    nt = n // (nb * tm)
    halo = init.shape[1]
    row = lambda w: pl.BlockSpec((tm, w), lambda b, t: (b * nt + t, 0))
    consts = [p["gdn_conv_w"], p["gdn_alog"], p["gdn_dt"], p["ones_gdn"]]
    return pl.pallas_call(
        functools.partial(_gdn_prep_kernel, halo=halo, bs=bs, tm=tm),
        grid=(nb, nt),
        in_specs=[row(GDN_CONV_CH), pl.BlockSpec((1, halo, GDN_CONV_CH), lambda b, t: (b, 0, 0)), row(LANES)]
        + [_const_spec(c.shape) for c in consts],
        out_specs=[row(GDN_WIDTH)] * 3 + [row(LANES)],
        out_shape=[jax.ShapeDtypeStruct((n, GDN_WIDTH), F32)] * 3 + [jax.ShapeDtypeStruct((n, LANES), F32)],
        scratch_shapes=[pltpu.VMEM((halo + tm, GDN_CONV_CH), F32)],
        compiler_params=_cparams("arbitrary", "arbitrary"),
        name="gdn_prep",
    )(qkv_raw, init, ab, *consts)


def _unit_lower_inverses(a_list, eye, chunk):
    n = [-a for a in a_list]
    inv = [eye + x for x in n]
    for _ in range(int(math.log2(chunk)) - 1):
        ns = [_split_bf16(x) for x in n]
        n = [_dot3(x, x) for x in ns]
        ns = [_split_bf16(x) for x in n]
        inv = [x + _dot3(_split_bf16(x), y) for x, y in zip(inv, ns)]
    return inv


def _gdn_chunk_kernel(q_ref, k_ref, v_ref, gb_ref, s0_ref, o_ref, sout_ref, s_ref, *, chunk, nbb, group):
    ci = pl.program_id(1)

    @pl.when(ci == 0)
    def _():
        s_ref[...] = s0_ref[...]

    rows = lax.broadcasted_iota(jnp.int32, (chunk, chunk), 0)
    cols = lax.broadcasted_iota(jnp.int32, (chunk, chunk), 1)
    causal = rows >= cols
    strict = rows > cols
    eye = (rows == cols).astype(F32)
    tri = causal.astype(F32)

    def one_group(gi, carry):
        units = [(gi * group + j, h) for j in range(group) for h in range(GDN_HEADS)]
        lanes = lambda h: slice(h * GDN_HEAD_DIM, (h + 1) * GDN_HEAD_DIM)
        s = [s_ref[b, h] for b, h in units]
        q = [q_ref[b, :, lanes(h)] for b, h in units]
        k = [k_ref[b, :, lanes(h)] for b, h in units]
        v = [v_ref[b, :, lanes(h)] for b, h in units]
        gb = [gb_ref[gi * group + j] for j in range(group)]
        gcs = [_dot_hi(tri, x) for x in gb]
        gc = [gcs[i // GDN_HEADS][:, h:h + 1] for i, (_, h) in enumerate(units)]
        beta = [gb[i // GDN_HEADS][:, GDN_HEADS + h:GDN_HEADS + h + 1] for i, (_, h) in enumerate(units)]
        decay = [jnp.exp(jnp.where(causal, x - jnp.sum(eye * x, axis=0, keepdims=True), -jnp.inf)) for x in gc]
        kb = [x * y for x, y in zip(k, beta)]
        a = [jnp.where(strict, _dot_nt(x, y) * d, 0.0) for x, y, d in zip(kb, k, decay)]
        inv = _unit_lower_inverses(a, eye, chunk)
        egc = [jnp.exp(x) for x in gc]
        rhs = [jnp.concatenate([x * bt, y * e], axis=1) for x, bt, y, e in zip(v, beta, kb, egc)]
        sol = [_dot3(_split_bf16(x), _split_bf16(y)) for x, y in zip(inv, rhs)]
        wS = [_dot(x[:, GDN_HEAD_DIM:2 * GDN_HEAD_DIM], y) for x, y in zip(sol, s)]
        v_new = [x[:, 0:GDN_HEAD_DIM] - y for x, y in zip(sol, wS)]
        qk = [_dot_nt(x, y) * d for x, y, d in zip(q, k, decay)]
        o_state = [_dot(x * e, y) for x, e, y in zip(q, egc, s)]
        o_chunk = [_dot(x, y) for x, y in zip(qk, v_new)]
        g_last = [x[chunk - 1:chunk, :] for x in gc]
        kd_t = [(x * jnp.exp(gl - g)).T for x, gl, g in zip(k, g_last, gc)]
        s_add = [_dot(x, y) for x, y in zip(kd_t, v_new)]
        for i, (b, h) in enumerate(units):
            o_ref[b, :, lanes(h)] = o_state[i] + o_chunk[i]
            s_ref[b, h] = s[i] * jnp.exp(g_last[i]) + s_add[i]
        return carry

    if nbb == group:
        one_group(0, 0)
    else:
        lax.fori_loop(0, nbb // group, one_group, 0)

    @pl.when(ci == pl.num_programs(1) - 1)
    def _():
        sout_ref[...] = s_ref[...]


def _gdn_chunks(q, k, v, gb, s0, chunk):
    b, t_len, _ = q.shape
    nbb = min(b, GDN_SEQ_BLOCK)
    group = min(nbb, GDN_SEQ_GROUP)
    blk = lambda w: pl.BlockSpec((nbb, chunk, w), lambda bi, ci: (bi, ci, 0))
    st = pl.BlockSpec((nbb, GDN_HEADS, GDN_HEAD_DIM, GDN_HEAD_DIM), lambda bi, ci: (bi, 0, 0, 0))
    return pl.pallas_call(
        functools.partial(_gdn_chunk_kernel, chunk=chunk, nbb=nbb, group=group),
        grid=(b // nbb, t_len // chunk),
        in_specs=[blk(GDN_WIDTH)] * 3 + [blk(LANES), st],
        out_specs=[blk(GDN_WIDTH), st],
        out_shape=[jax.ShapeDtypeStruct(q.shape, F32), jax.ShapeDtypeStruct(s0.shape, F32)],
        scratch_shapes=[pltpu.VMEM((nbb, GDN_HEADS, GDN_HEAD_DIM, GDN_HEAD_DIM), F32)],
        compiler_params=_cparams("arbitrary", "arbitrary"),
        name="gdn_chunks",
    )(q, k, v, gb, s0)


def _route(logits, count_ref):
    lane = lax.broadcasted_iota(jnp.int32, logits.shape, 1)
    neg = -jnp.inf
    is_group = (lane >= N_EXPERTS) & (lane < N_EXPERTS + N_GROUPS)
    gl = jnp.where(is_group, logits, neg)
    gmax = jnp.max(gl, axis=-1, keepdims=True)
    gp = 1.0 / jnp.sum(jnp.exp(gl - gmax), axis=-1, keepdims=True)
    gidx = jnp.min(jnp.where(gl == gmax, lane, LANES), axis=-1, keepdims=True) - N_EXPERTS
    lo = gidx * EXPERTS_PER_GROUP
    el = jnp.where((lane >= lo) & (lane < lo + EXPERTS_PER_GROUP), logits, neg)
    m1 = jnp.max(el, axis=-1, keepdims=True)
    i1 = jnp.min(jnp.where(el == m1, lane, LANES), axis=-1, keepdims=True)
    el2 = jnp.where(lane == i1, neg, el)
    m2 = jnp.max(el2, axis=-1, keepdims=True)
    i2 = jnp.min(jnp.where(el2 == m2, lane, LANES), axis=-1, keepdims=True)
    e2 = jnp.exp(m2 - m1)
    den = 1.0 + e2
    tm = logits.shape[0]
    hit1 = lane == i1
    hit2 = lane == i2
    chosen = jnp.where(hit1 | hit2, 1.0, 0.0)
    before = lax.broadcasted_iota(jnp.int32, (tm, tm), 0) > lax.broadcasted_iota(jnp.int32, (tm, tm), 1)
    seen = jnp.dot(before.astype(BF16), chosen.astype(BF16), preferred_element_type=F32) + count_ref[...]
    rank1 = jnp.sum(jnp.where(hit1, seen, 0.0), axis=-1, keepdims=True)
    rank2 = jnp.sum(jnp.where(hit2, seen, 0.0), axis=-1, keepdims=True)
    count_ref[...] += jnp.sum(chosen, axis=0, keepdims=True)
    fields = (i1.astype(F32), i2.astype(F32), gp / den, gp * e2 / den, rank1, rank2)
    out = jnp.zeros(logits.shape, F32)
    for j, f in enumerate(fields):
        out = jnp.where(lane == j, f, out)
    return out


def _merge_kernel(y_ref, r_ref, k_ref, v_ref, g_ref, o_ref, z_ref, gate_ref, h_ref, lnw_ref, lnb_ref, rk_ref,
                  ones_ref, gnw_ref, woa_ref, wob_ref, wo_ref, n2_ref, rw_ref, rb_ref, *rest, nt_main):
    tail_refs, (h1_ref, xt_ref, mg_ref, cnt_ref) = rest[:-4], rest[-4:]
    i = pl.program_id(0)

    @pl.when(i == 0)
    def _():
        cnt_ref[...] = tail_refs[3][...] if tail_refs else jnp.zeros_like(cnt_ref)

    if tail_refs:
        @pl.when(i >= nt_main)
        def _():
            h1_ref[...] = tail_refs[0][...]
            xt_ref[...] = tail_refs[1][...]
            mg_ref[...] = tail_refs[2][...]

        pl.when(i < nt_main)(functools.partial(
            _merge_tile, y_ref, r_ref, k_ref, v_ref, g_ref, o_ref, z_ref, gate_ref, h_ref, lnw_ref, lnb_ref, rk_ref,
            ones_ref, gnw_ref, woa_ref, wob_ref, wo_ref, n2_ref, rw_ref, rb_ref, h1_ref, xt_ref, mg_ref, cnt_ref))
    else:
        _merge_tile(y_ref, r_ref, k_ref, v_ref, g_ref, o_ref, z_ref, gate_ref, h_ref, lnw_ref, lnb_ref, rk_ref,
                    ones_ref, gnw_ref, woa_ref, wob_ref, wo_ref, n2_ref, rw_ref, rb_ref, h1_ref, xt_ref, mg_ref, cnt_ref)


def _merge_tile(y_ref, r_ref, k_ref, v_ref, g_ref, o_ref, z_ref, gate_ref, h_ref, lnw_ref, lnb_ref, rk_ref,
                ones_ref, gnw_ref, woa_ref, wob_ref, wo_ref, n2_ref, rw_ref, rb_ref, h1_ref, xt_ref, mg_ref, cnt_ref):
    tm = h_ref.shape[0]
    halves = [pl.ds(0, tm // 2), pl.ds(tm // 2, tm // 2)]
    ones = ones_ref[...]
    inv_n = 1.0 / RW_HEAD_DIM
    y = [y_ref[p, :] for p in halves]
    mu = [_seg_sum(a, ones) * inv_n for a in y]
    rkk = [_seg_sum(r_ref[p, :] * k_ref[p, :] * rk_ref[...], ones) for p in halves]
    yc = [a - m for a, m in zip(y, mu)]
    var = [_seg_sum(c * c, ones) * inv_n for c in yc]
    ya = [(c * lax.rsqrt(s + RW_GN_EPS) * lnw_ref[...] + lnb_ref[...] + b * v_ref[p, :]) * g_ref[p, :]
          for c, s, b, p in zip(yc, var, rkk, halves)]

    gnw = gnw_ref[...]
    yb = []
    for p in halves:
        z = z_ref[p, :]
        heads = [_rmsnorm(o_ref[p, h * GDN_HEAD_DIM:(h + 1) * GDN_HEAD_DIM], gnw) for h in range(GDN_HEADS)]
        yb.append(jnp.concatenate(heads, axis=1) * (z * _sigmoid(z)))

    out_a = [_dot(a, woa_ref[...]) for a in ya]
    out_b = [_dot(b, wob_ref[...]) for b in yb]
    merged = [_sigmoid(gate_ref[p, 0:D_MODEL]) * a + _sigmoid(gate_ref[p, D_MODEL:2 * D_MODEL]) * b
              for p, a, b in zip(halves, out_a, out_b)]
    h1 = [h_ref[p, :] + _dot(m, wo_ref[...]) for p, m in zip(halves, merged)]
    xt = [_rmsnorm(a, n2_ref[...]) for a in h1]
    for p, a, b in zip(halves, h1, xt):
        h1_ref[p, :] = a
        xt_ref[p, :] = b
    logits = jnp.concatenate([_dot_f32(a, rw_ref[...]) for a in xt], axis=0) + rb_ref[...]
    mg_ref[...] = _route(logits, cnt_ref)


def _merge(y, rr, bk, v, g, o, z, gates, h, p, tm, wide_nt, tail):
    n = h.shape[0]
    nt_main = n // tm
    n_out = n + (tail[0].shape[0] if tail else 0)
    mi = (lambda i: jnp.minimum(i, nt_main - 1)) if tail else (lambda i: i)
    row = lambda w: pl.BlockSpec((tm, w), lambda i: (mi(i), 0))
    out_row = lambda w: pl.BlockSpec((tm, w), lambda i: (i, 0))
    tail_row = lambda w: pl.BlockSpec((tm, w), lambda i: (jnp.maximum(i - nt_main, 0), 0))
    if wide_nt is None:
        part = lambda per_seq, k: pl.BlockSpec((tm, RW_WIDTH), lambda i: (mi(i), k))
    else:
        part = lambda per_seq, k: pl.BlockSpec(
            (tm, RW_WIDTH), lambda i: (mi(i) % wide_nt, (mi(i) // wide_nt) * per_seq + k))
    consts = [p["rw_lnx_w"], p["rw_lnx_b"], p["rw_r_k"], p["ones_rw"], p["gdn_norm_w"], p["w_oA"], p["w_oB"],
              p["w_o"], p["norm2_w"], p["router_w"], p["router_b"]]
    tail_specs = [tail_row(D_MODEL), tail_row(D_MODEL), tail_row(LANES), _const_spec((1, LANES))] if tail else []
    return pl.pallas_call(
        functools.partial(_merge_kernel, nt_main=nt_main),
        grid=(n_out // tm,),
        in_specs=[part(1, 0), part(2, 0), part(2, 1), part(1, 0)] + [row(RW_WIDTH)] * 3
        + [row(2 * D_MODEL), row(D_MODEL)] + [_const_spec(c.shape) for c in consts] + tail_specs,
        out_specs=[out_row(D_MODEL), out_row(D_MODEL), out_row(LANES), _const_spec((1, LANES))],
        out_shape=[jax.ShapeDtypeStruct((n_out, D_MODEL), F32), jax.ShapeDtypeStruct((n_out, D_MODEL), F32),
                   jax.ShapeDtypeStruct((n_out, LANES), F32), jax.ShapeDtypeStruct((1, LANES), F32)],
        compiler_params=_cparams("arbitrary"),
        name="merge",
    )(y, rr, bk, v, g, o, z, gates, h, *consts, *(tail or ()))


def _row_of(ref3, p):
    return ref3.at[lax.shift_right_logical(p, 3), pl.ds(lax.bitwise_and(p, SUBLANES - 1), 1)]


def _start_row_gather(src_hbm, idx_ref, dst_ref, sem, n, priority):
    def body(g, carry):
        for k in range(SUBLANES):
            src = _row_of(src_hbm, idx_ref[0, 0, g * SUBLANES + k])
            pltpu.make_async_copy(src, dst_ref.at[g, pl.ds(k, 1)], sem).start(priority)
        return carry

    lax.fori_loop(0, n // SUBLANES, body, 0)


def _wait_row_gather(src_hbm, dst_ref, sem, n):
    def body(r, carry):
        pltpu.make_async_copy(src_hbm.at[0, pl.ds(0, 1)], dst_ref.at[0, pl.ds(0, 1)], sem).wait()
        return carry

    lax.fori_loop(0, n, body, 0, unroll=8)


def _moe_dispatch_kernel(p1_ref, p2_ref, x_ref, zeros_hbm, xs_hbm, xbuf, sems, *, tm):
    del zeros_hbm
    i = pl.program_id(0)
    nt = pl.num_programs(0)
    slot = i % 2

    def wait_tile(s):
        def body(r, carry):
            pltpu.make_async_copy(xbuf.at[s, 0, pl.ds(0, 1)], xs_hbm.at[0, pl.ds(0, 1)], sems.at[s]).wait()
            return carry

        lax.fori_loop(0, 2 * tm, body, 0, unroll=8)

    @pl.when(i >= 2)
    def _():
        wait_tile(slot)

    xbuf[slot] = x_ref[...].reshape(tm // SUBLANES, SUBLANES, D_MODEL)

    def issue(g, carry):
        for k in range(SUBLANES):
            src = xbuf.at[slot, g, pl.ds(k, 1)]
            for pos_ref, priority in ((p1_ref, 0), (p2_ref, 1)):
                dst = _row_of(xs_hbm, pos_ref[0, 0, g * SUBLANES + k])
                pltpu.make_async_copy(src, dst, sems.at[slot]).start(priority)
        return carry

    lax.fori_loop(0, tm // SUBLANES, issue, 0)

    @pl.when(i == nt - 1)
    def _():
        wait_tile(slot)

        @pl.when(nt > 1)
        def _():
            wait_tile(1 - slot)


def _moe_dispatch(xt, pos1, pos2, n_sorted, tm):
    n = xt.shape[0]
    idx = pl.BlockSpec((1, 1, tm), lambda i: (i, 0, 0), memory_space=pltpu.SMEM)
    return pl.pallas_call(
        functools.partial(_moe_dispatch_kernel, tm=tm),
        grid=(n // tm,),
        in_specs=[idx, idx, pl.BlockSpec((tm, D_MODEL), lambda i: (i, 0)), pl.BlockSpec(memory_space=pl.ANY)],
        out_specs=pl.BlockSpec(memory_space=pl.ANY),
        out_shape=jax.ShapeDtypeStruct((n_sorted // SUBLANES, SUBLANES, D_MODEL), F32),
        scratch_shapes=[pltpu.VMEM((2, tm // SUBLANES, SUBLANES, D_MODEL), F32), pltpu.SemaphoreType.DMA((2,))],
        input_output_aliases={3: 0},
        compiler_params=_cparams("arbitrary"),
        name="moe_dispatch",
    )(pos1, pos2, xt, jnp.zeros((n_sorted // SUBLANES, SUBLANES, D_MODEL), F32)).reshape(n_sorted, D_MODEL)


def _moe_experts_kernel(te_ref, tv_ref, x_ref, wg_ref, wu_ref, wd_ref, y_ref):
    i = pl.program_id(0)

    @pl.when(tv_ref[i] == 1)
    def _():
        xb = x_ref[...].astype(BF16)
        hg = jnp.dot(xb, wg_ref[0].astype(BF16), preferred_element_type=F32)
        hu = jnp.dot(xb, wu_ref[0].astype(BF16), preferred_element_type=F32)
        hid = hg * _sigmoid(hg) * hu
        y_ref[...] = jnp.dot(hid.astype(BF16), wd_ref[0].astype(BF16), preferred_element_type=F32)

    @pl.when(tv_ref[i] == 0)
    def _():
        y_ref[...] = jnp.zeros_like(y_ref)


def _moe_experts(xs, tile_expert, tile_valid, wg, wu, wd, tm):
    nt = xs.shape[0] // tm
    wspec = lambda shape: pl.BlockSpec((1,) + shape, lambda i, te, tv: (te[i], 0, 0))
    rows = pl.BlockSpec((tm, D_MODEL), lambda i, te, tv: (i, 0))
    return pl.pallas_call(
        _moe_experts_kernel,
        grid_spec=pltpu.PrefetchScalarGridSpec(
            num_scalar_prefetch=2,
            grid=(nt,),
            in_specs=[rows, wspec((D_MODEL, EXPERT_FF)), wspec((D_MODEL, EXPERT_FF)), wspec((EXPERT_FF, D_MODEL))],
            out_specs=rows,
        ),
        out_shape=jax.ShapeDtypeStruct(xs.shape, F32),
        compiler_params=_cparams("arbitrary"),
        name="moe_experts",
    )(tile_expert, tile_valid, xs, wg, wu, wd)


def _moe_combine_kernel(cur1_ref, cur2_ref, nxt1_ref, nxt2_ref, ys_hbm, h1_ref, mg_ref, nf_ref, o_ref, otail_ref,
                        buf1, buf2, sems, *, tm, nt_main):
    i = pl.program_id(0)
    nt = pl.num_programs(0)
    slot = i % 2

    @pl.when(i == 0)
    def _():
        _start_row_gather(ys_hbm, cur1_ref, buf1.at[0], sems.at[0, 0], tm, 0)
        _start_row_gather(ys_hbm, cur2_ref, buf2.at[0], sems.at[1, 0], tm, 1)

    @pl.when(i + 1 < nt)
    def _():
        _start_row_gather(ys_hbm, nxt1_ref, buf1.at[1 - slot], sems.at[0, 1 - slot], tm, 0)
        _start_row_gather(ys_hbm, nxt2_ref, buf2.at[1 - slot], sems.at[1, 1 - slot], tm, 1)

    _wait_row_gather(ys_hbm, buf1.at[slot], sems.at[0, slot], tm)
    _wait_row_gather(ys_hbm, buf2.at[slot], sems.at[1, slot], tm)
    mg = mg_ref[...]
    rows = lambda buf: buf[slot].reshape(tm, D_MODEL)
    moe = mg[:, 2:3] * rows(buf1) + mg[:, 3:4] * rows(buf2)
    out = _rmsnorm(h1_ref[...] + moe, nf_ref[...])

    @pl.when(i < nt_main)
    def _():
        o_ref[...] = out

    @pl.when(i >= nt_main)
    def _():
        otail_ref[...] = out


def _moe_combine(ys, pos1, pos2, h1, mg, nf, tm, n_main):
    n = h1.shape[0]
    nt = n // tm
    nt_main = n_main // tm
    cur = pl.BlockSpec((1, 1, tm), lambda i: (i, 0, 0), memory_space=pltpu.SMEM)
    nxt = pl.BlockSpec((1, 1, tm), lambda i: (jnp.minimum(i + 1, nt - 1), 0, 0), memory_space=pltpu.SMEM)
    row = lambda w: pl.BlockSpec((tm, w), lambda i: (i, 0))
    return pl.pallas_call(
        functools.partial(_moe_combine_kernel, tm=tm, nt_main=nt_main),
        grid=(nt,),
        in_specs=[cur, cur, nxt, nxt, pl.BlockSpec(memory_space=pl.ANY), row(D_MODEL), row(LANES),
                  _const_spec((1, D_MODEL))],
        out_specs=[pl.BlockSpec((tm, D_MODEL), lambda i: (jnp.minimum(i, nt_main - 1), 0)),
                   pl.BlockSpec((tm, D_MODEL), lambda i: (jnp.maximum(i - nt_main, 0), 0))],
        out_shape=[jax.ShapeDtypeStruct((n_main, D_MODEL), F32), jax.ShapeDtypeStruct((n - n_main, D_MODEL), F32)],
        scratch_shapes=[pltpu.VMEM((2, tm // SUBLANES, SUBLANES, D_MODEL), F32)] * 2 + [pltpu.SemaphoreType.DMA((2, 2))],
        compiler_params=_cparams("arbitrary"),
        name="moe_combine",
    )(pos1, pos2, pos1, pos2, ys.reshape(-1, SUBLANES, D_MODEL), h1, mg, nf)


def _moe(xt, h1, mg, counts, wg, wu, wd, nf, tm_expert, tm_token, n_main):
    n = xt.shape[0]
    i32 = jnp.int32
    expert = jnp.arange(N_EXPERTS, dtype=i32)
    cnt = counts[0, :N_EXPERTS].astype(i32)
    padded = (cnt + tm_expert - 1) // tm_expert * tm_expert
    ends = jnp.sum(jnp.where(expert[:, None] >= expert[None, :], padded[None, :], 0), axis=1)
    starts = ends - padded
    start_of = lambda e: jnp.sum(jnp.where(e[:, None] == expert[None, :], starts[None, :], 0), axis=1)
    pos1 = start_of(mg[:, 0].astype(i32)) + mg[:, 4].astype(i32)
    pos2 = start_of(mg[:, 1].astype(i32)) + mg[:, 5].astype(i32)
    nt = (2 * n + N_EXPERTS * (tm_expert - 1)) // tm_expert + 1
    tile_start = jnp.arange(nt, dtype=i32) * tm_expert
    tile_valid = (tile_start < jnp.sum(padded)).astype(i32)
    tile_expert = jnp.sum((tile_start[:, None] >= ends[None, :]).astype(i32), axis=1)
    last_used = jnp.max(jnp.where(padded > 0, expert, 0))
    tile_expert = jnp.where(tile_valid == 1, tile_expert, last_used)
    shape = (n // tm_token, 1, tm_token)
    pos1, pos2 = pos1.reshape(shape), pos2.reshape(shape)
    xs = _moe_dispatch(xt, pos1, pos2, nt * tm_expert, tm_token)
    ys = _moe_experts(xs, tile_expert, tile_valid, wg, wu, wd, tm_expert)
    return _moe_combine(ys, pos1, pos2, h1, mg, nf, tm_token, n_main)


def _pad_rows(x, rows):
    return jnp.pad(x, ((0, rows - x.shape[0]),) + ((0, 0),) * (x.ndim - 1))


def _mixer(x_rows, p, *, batch, t_len, time_major, tm, rw_prev, conv_prev, wkv0, gdn0, chunk, with_merge, tail=None):
    n = batch * t_len
    xn, rw_cur, qkv_raw, z, gates, ab = _in_proj(x_rows, p["norm1_w"], p["w_in"], min(tm, n))

    if time_major:
        nb, bs, tmt = 1, batch, n
        rw_init = rw_prev[None]
        conv_init = jnp.transpose(conv_prev, (1, 0, 2)).reshape(1, (GDN_CONV - 1) * batch, GDN_CONV_CH)
    else:
        nb, bs, tmt = batch, 1, min(tm, t_len)
        rw_init = jnp.pad(rw_prev[:, None, :], ((0, 0), (SUBLANES - 1, 0), (0, 0)))
        conv_init = jnp.pad(conv_prev, ((0, 0), (SUBLANES - (GDN_CONV - 1), 0), (0, 0)))

    wide = not time_major
    wa, bk, rr, v, g = _rwkv_prep(rw_cur, rw_init, p, nb, tmt, bs, wide)
    q, kg, vg, gb = _gdn_prep(qkv_raw, conv_init, ab, p, nb, tmt, bs)

    def seq_major(a):
        if time_major:
            return jnp.transpose(a.reshape(t_len, batch, a.shape[-1]), (1, 0, 2))
        return a.reshape(batch, t_len, a.shape[-1])

    def rows_like_x(a):
        if time_major:
            a = jnp.transpose(a, (1, 0, 2))
        return a.reshape(n, a.shape[-1])

    t_pad = -(-t_len // chunk) * chunk
    def gdn_seq(a):
        a = seq_major(a)
        if t_pad != t_len:
            a = jnp.pad(a, ((0, 0), (0, t_pad - t_len), (0, 0)))
        return a
    o_pad, gdn = _gdn_chunks(gdn_seq(q), gdn_seq(kg), gdn_seq(vg), gdn_seq(gb), gdn0, chunk)

    bpad = -(-batch // SCAN_SEQS) * SCAN_SEQS

    def by_time(a):
        a = a.reshape(t_len, -1)
        return jnp.pad(a, ((0, 0), (0, a.shape[1] // batch * (bpad - batch))))

    scan_in = tuple(_to_scan_pair(by_time(a), bpad) for a in (wa, bk, rr)) + (_to_scan_rows(by_time(v), bpad),)
    scan_in, o_pad = lax.optimization_barrier((scan_in, o_pad))
    tc = min(t_len, 32)
    y_scan, s_scan = _rwkv_scan(*scan_in, _state_to_scan(_pad_rows(wkv0, bpad)), tc)
    y_rw = _from_scan_rows(y_scan)[:, :batch * RW_WIDTH]
    if time_major:
        y_rw = y_rw.reshape(n, RW_WIDTH)
    wkv = _state_from_scan(s_scan, bpad)[:batch]
    o = rows_like_x(o_pad[:, :t_len])

    out = dict(xn=xn, rw_cur=rw_cur, qkv_raw=qkv_raw, wkv=wkv, gdn=gdn)
    if with_merge:
        out["h1"], out["xt"], out["mg"], out["counts"] = _merge(y_rw, rr, bk, v, g, o, z, gates, x_rows, p, min(tm, n),
                                                                 t_len // tmt if wide else None, tail)
    return out


def kernel(x_prompt, x_sample, state_shift, state_wkv, state_conv, state_gdn, meta_tokens, norm1_w, w_in, mu_shift, rw_w0, rw_w2, rw_a0, rw_a2, rw_g2, rw_k_k, rw_k_a, rw_r_k, rw_lnx_w, rw_lnx_b, gdn_conv_w, gdn_A_log, gdn_dt_bias, gdn_norm_w, w_oA, w_oB, w_o, norm2_w, router_g, router_g_b, router_e, router_e_b, moe_w_gate, moe_w_up, moe_w_down, norm_f_w):
    bp, sp, _ = x_prompt.shape
    bsm, ss, _ = x_sample.shape
    row = lambda a: a[0].reshape(1, -1)
    w0 = w_in[0]
    gcol = RW_COLS + 4 * GDN_WIDTH + 2 * GDN_HEADS
    w_packed = jnp.concatenate(
        [w0[:, :RW_COLS + 4 * GDN_WIDTH], w0[:, gcol:], w0[:, RW_COLS + 4 * GDN_WIDTH:gcol],
         jnp.zeros((D_MODEL, LANES - 2 * GDN_HEADS), F32)], axis=1).astype(BF16)
    lane_pad = lambda a: jnp.pad(a, ((0, 0), (0, LANES - a.shape[1])))
    p = dict(
        norm1_w=row(norm1_w), w_in=w_packed, mu_shift=row(mu_shift), rw_w0=row(rw_w0), rw_w2=rw_w2[0],
        rw_a0=row(rw_a0), rw_a2=rw_a2[0], rw_g2=rw_g2[0], rw_k_k=row(rw_k_k), rw_k_a=row(rw_k_a),
        rw_r_k=row(rw_r_k), rw_lnx_w=row(rw_lnx_w), rw_lnx_b=row(rw_lnx_b),
        ones_rw=_block_ones(RW_WIDTH, RW_HEAD_DIM), ones_gdn=_block_ones(GDN_WIDTH, GDN_HEAD_DIM),
        gdn_conv_w=gdn_conv_w[0], gdn_alog=lane_pad(row(gdn_A_log)), gdn_dt=lane_pad(row(gdn_dt_bias)),
        gdn_norm_w=row(gdn_norm_w), w_oA=w_oA[0].astype(BF16), w_oB=w_oB[0].astype(BF16),
        w_o=w_o[0].astype(BF16), norm2_w=row(norm2_w),
        router_w=lane_pad(jnp.concatenate([router_e[0], router_g[0]], axis=1)),
        router_b=lane_pad(jnp.concatenate([row(router_e_b), row(router_g_b)], axis=1)),
    )
    nf = norm_f_w.reshape(1, -1)
    dt = x_prompt.dtype

    meta = _mixer(meta_tokens.astype(dt), p, batch=1, t_len=N_META, time_major=False, tm=N_META,
                  rw_prev=jnp.zeros((1, RW_COLS), dt), conv_prev=jnp.zeros((1, GDN_CONV - 1, GDN_CONV_CH), dt),
                  wkv0=jnp.zeros((1, RW_HEADS, RW_HEAD_DIM, RW_HEAD_DIM), dt),
                  gdn0=jnp.zeros((1, GDN_HEADS, GDN_HEAD_DIM, GDN_HEAD_DIM), dt), chunk=N_META, with_merge=False)
    rep = lambda a: jnp.broadcast_to(a, (bp,) + a.shape[1:])

    xs_rows = jnp.transpose(x_sample, (1, 0, 2)).reshape(ss * bsm, D_MODEL)
    prev_s = _in_proj(state_shift[0].astype(dt), p["norm1_w"], w_packed, bsm, norm=False)[1]
    sample = _mixer(xs_rows, p, batch=bsm, t_len=ss, time_major=True, tm=256,
                    rw_prev=prev_s, conv_prev=state_conv[0].astype(dt), wkv0=state_wkv[0], gdn0=state_gdn[0],
                    chunk=SUBLANES, with_merge=True)

    prompt = _mixer(x_prompt.reshape(bp * sp, D_MODEL), p, batch=bp, t_len=sp, time_major=False, tm=256,
                    rw_prev=rep(meta["rw_cur"][N_META - 1:]), conv_prev=rep(meta["qkv_raw"][None, N_META - 3:]),
                    wkv0=rep(meta["wkv"]), gdn0=rep(meta["gdn"]), chunk=GDN_CHUNK, with_merge=True,
                    tail=(sample["h1"], sample["xt"], sample["mg"], sample["counts"]))

    y_p, y_s = _moe(prompt["xt"], prompt["h1"], prompt["mg"], prompt["counts"], moe_w_gate[0], moe_w_up[0],
                    moe_w_down[0], nf, 256, 256, bp * sp)

    y_prompt = y_p.reshape(bp, sp, D_MODEL)
    y_sample = jnp.transpose(y_s.reshape(ss, bsm, D_MODEL), (1, 0, 2))
    shift_p = prompt["xn"].reshape(bp, sp, D_MODEL)[:, -1][None]
    conv_p = prompt["qkv_raw"].reshape(bp, sp, GDN_CONV_CH)[:, sp - (GDN_CONV - 1):][None]
    shift_s = sample["xn"].reshape(ss, bsm, D_MODEL)[-1][None].astype(state_shift.dtype)
    xpad_s = jnp.concatenate([state_conv[0].astype(dt),
                              jnp.transpose(sample["qkv_raw"].reshape(ss, bsm, GDN_CONV_CH), (1, 0, 2))], axis=1)
    conv_s = xpad_s[:, ss:][None].astype(state_conv.dtype)
    return (y_prompt, y_sample, shift_p, prompt["wkv"][None], conv_p, prompt["gdn"][None],
            shift_s, sample["wkv"][None].astype(state_wkv.dtype), conv_s, sample["gdn"][None].astype(state_gdn.dtype))
```

```python
import functools
import math

import jax
import jax.numpy as jnp
from jax import lax
from jax.experimental import pallas as pl
from jax.experimental.pallas import tpu as pltpu

F32 = jnp.float32
BF16 = jnp.bfloat16
HIGHEST = lax.Precision.HIGHEST

D_MODEL = 1024
N_META = 16
RW_WIDTH = 512
RW_HEAD_DIM = 64
RW_HEADS = 8
RW_DECAY_LORA = 64
RW_AAA_LORA = 64
RW_GATE_LORA = 128
RW_COLS = 3 * RW_WIDTH + RW_DECAY_LORA + RW_AAA_LORA + RW_GATE_LORA
RW_GN_EPS = RW_HEAD_DIM * 1e-5
GDN_WIDTH = 512
GDN_HEAD_DIM = 128
GDN_HEADS = 4
GDN_CONV = 4
GDN_CONV_CH = 3 * GDN_WIDTH
GDN_CHUNK = 64
N_GROUPS = 4
EXPERTS_PER_GROUP = 8
N_EXPERTS = 32
EXPERT_FF = 512
RMS_EPS = 1e-6

LANES = 128
SUBLANES = 8
VMEM_LIMIT_BYTES = 56 * 1024 * 1024

COL_RW = 0
COL_QKV = COL_RW + RW_COLS
COL_Z = COL_QKV + GDN_CONV_CH
COL_GATE = COL_Z + GDN_WIDTH
COL_AB = COL_GATE + 2 * D_MODEL
IN_COLS_PACKED = COL_AB + LANES

SCAN_SEQS = 8
SCAN_ROWS = RW_HEAD_DIM // 2
SCAN_ROW_GROUP = 1

GDN_SEQ_BLOCK = 8
GDN_SEQ_GROUP = 8


def _cparams(*sem):
    return pltpu.CompilerParams(dimension_semantics=sem, vmem_limit_bytes=VMEM_LIMIT_BYTES)


def _const_spec(shape):
    zeros = (0,) * len(shape)
    return pl.BlockSpec(shape, lambda *_: zeros)


def _dot(a, b):
    return jnp.dot(a.astype(BF16), b.astype(BF16), preferred_element_type=F32)


def _dot_hi(a, b):
    return jnp.dot(a, b, preferred_element_type=F32, precision=HIGHEST)


def _split_bf16(a):
    hi = a.astype(BF16)
    return hi, (a - hi.astype(F32)).astype(BF16)


def _dot3(a, b):
    d = lambda x, y: jnp.dot(x, y, preferred_element_type=F32)
    return d(a[0], b[0]) + d(a[0], b[1]) + d(a[1], b[0])


def _dot_f32(a, b):
    return _dot3(_split_bf16(a), _split_bf16(b))


def _seg_sum(x, ones):
    hi, lo = _split_bf16(x)
    return jnp.dot(hi, ones, preferred_element_type=F32) + jnp.dot(lo, ones, preferred_element_type=F32)


def _dot_nt(a, b):
    dims = (((1,), (1,)), ((), ()))
    return lax.dot_general(a.astype(BF16), b.astype(BF16), dims, preferred_element_type=F32)


def _sigmoid(x):
    return 1.0 / (1.0 + jnp.exp(-x))


def _softplus(x):
    return jnp.maximum(x, 0.0) + jnp.log1p(jnp.exp(-jnp.abs(x)))


def _rmsnorm(x, w):
    ms = jnp.mean(x * x, axis=-1, keepdims=True)
    return x * lax.rsqrt(ms + RMS_EPS) * w


def _block_ones(width, seg):
    r = jnp.arange(width) // seg
    return (r[:, None] == r[None, :]).astype(BF16)


def _in_proj_kernel(x_ref, nw_ref, w_ref, xn_ref, rw_ref, qkv_ref, z_ref, gate_ref, ab_ref, *, norm):
    x = x_ref[...]
    xn = _rmsnorm(x, nw_ref[...]) if norm else x
    xn_ref[...] = xn
    xb = xn.astype(BF16)
    rw_ref[...] = jnp.dot(xb, w_ref[:, COL_RW:COL_QKV], preferred_element_type=F32)
    qkv_ref[...] = jnp.dot(xb, w_ref[:, COL_QKV:COL_Z], preferred_element_type=F32)
    z_ref[...] = jnp.dot(xb, w_ref[:, COL_Z:COL_GATE], preferred_element_type=F32)
    gate_ref[...] = jnp.dot(xb, w_ref[:, COL_GATE:COL_AB], preferred_element_type=F32)
    ab_ref[...] = jnp.dot(xb, w_ref[:, COL_AB:IN_COLS_PACKED], preferred_element_type=F32)


def _in_proj(x, norm_w, w_packed, tm, norm=True):
    n = x.shape[0]
    widths = (D_MODEL, RW_COLS, GDN_CONV_CH, GDN_WIDTH, 2 * D_MODEL, LANES)
    row = lambda w: pl.BlockSpec((tm, w), lambda i: (i, 0))
    return pl.pallas_call(
        functools.partial(_in_proj_kernel, norm=norm),
        grid=(n // tm,),
        in_specs=[row(D_MODEL), _const_spec((1, D_MODEL)), _const_spec(w_packed.shape)],
        out_specs=[row(w) for w in widths],
        out_shape=[jax.ShapeDtypeStruct((n, w), F32) for w in widths],
        compiler_params=_cparams("arbitrary"),
        name="in_proj",
    )(x, norm_w, w_packed)


def _stage_tile(ext_ref, init_ref, cur, halo, tm):
    t = pl.program_id(1)

    @pl.when(t == 0)
    def _():
        ext_ref[0:halo, :] = init_ref[0]

    @pl.when(t > 0)
    def _():
        ext_ref[0:halo, :] = ext_ref[tm:tm + halo, :]

    ext_ref[halo:halo + tm, :] = cur


def _rwkv_prep_kernel(cur_ref, init_ref, mu_ref, w0_ref, w2_ref, a0_ref, a2_ref, g2_ref, kk_ref, ka_ref,
                      ones_ref, wa_ref, bk_ref, rr_ref, v_ref, g_ref, ext_ref, *, halo, bs, tm):
    cur = cur_ref[...]
    _stage_tile(ext_ref, init_ref, cur, halo, tm)
    prev = ext_ref[halo - bs:halo - bs + tm, :]
    zc = cur + (prev - cur) * mu_ref[...]
    c0, c1, c2 = RW_WIDTH, 2 * RW_WIDTH, 3 * RW_WIDTH
    r = zc[:, 0:c0]
    k = zc[:, c0:c1]
    v = zc[:, c1:c2]
    xw = zc[:, c2:c2 + RW_DECAY_LORA]
    xa = zc[:, c2 + RW_DECAY_LORA:c2 + RW_DECAY_LORA + RW_AAA_LORA]
    xg = zc[:, c2 + RW_DECAY_LORA + RW_AAA_LORA:RW_COLS]
    w_raw = -_softplus(-(w0_ref[...] + _dot_f32(jnp.tanh(xw), w2_ref[...]))) - 0.5
    decay = jnp.exp(-jnp.exp(w_raw))
    a = _sigmoid(a0_ref[...] + _dot_f32(xa, a2_ref[...]))
    g = _dot_f32(_sigmoid(xg), g2_ref[...])
    kkr = k * kk_ref[...]
    kk = kkr * lax.rsqrt(_seg_sum(kkr * kkr, ones_ref[...]) + 1e-6)
    wa_ref[:, 0:RW_WIDTH] = decay
    wa_ref[:, RW_WIDTH:2 * RW_WIDTH] = -kk
    bk_ref[:, 0:RW_WIDTH] = kk * a
    bk_ref[:, RW_WIDTH:2 * RW_WIDTH] = k * (1.0 + (a - 1.0) * ka_ref[...])
    rr_ref[:, 0:RW_WIDTH] = r
    rr_ref[:, RW_WIDTH:2 * RW_WIDTH] = r
    v_ref[...] = v
    g_ref[...] = g


def _rwkv_prep(rw_cur, init, p, nb, tm, bs, wide):
    n = rw_cur.shape[0]
    nt = n // (nb * tm)
    halo = init.shape[1]
    row = lambda w: pl.BlockSpec((tm, w), lambda b, t: (b * nt + t, 0))
    if wide:
        out_row = lambda w: pl.BlockSpec((tm, w), lambda b, t: (t, b))
        out_shape = lambda w: jax.ShapeDtypeStruct((n // nb, nb * w), F32)
    else:
        out_row = row
        out_shape = lambda w: jax.ShapeDtypeStruct((n, w), F32)
    consts = [p["mu_shift"], p["rw_w0"], p["rw_w2"], p["rw_a0"], p["rw_a2"], p["rw_g2"], p["rw_k_k"], p["rw_k_a"],
              p["ones_rw"]]
    return pl.pallas_call(
        functools.partial(_rwkv_prep_kernel, halo=halo, bs=bs, tm=tm),
        grid=(nb, nt),
        in_specs=[row(RW_COLS), pl.BlockSpec((1, halo, RW_COLS), lambda b, t: (b, 0, 0))]
        + [_const_spec(c.shape) for c in consts],
        out_specs=[out_row(2 * RW_WIDTH)] * 3 + [out_row(RW_WIDTH), row(RW_WIDTH)],
        out_shape=[out_shape(2 * RW_WIDTH)] * 3 + [out_shape(RW_WIDTH), jax.ShapeDtypeStruct((n, RW_WIDTH), F32)],
        scratch_shapes=[pltpu.VMEM((halo + tm, RW_COLS), F32)],
        compiler_params=_cparams("arbitrary", "arbitrary"),
        name="rwkv_prep",
    )(rw_cur, init, *consts)


def _sublane_allsum(x):
    x = x + pltpu.roll(x, 4, 0)
    x = x + pltpu.roll(x, 2, 0)
    return x + pltpu.roll(x, 1, 0)


def _rwkv_scan_kernel(wa_ref, bk_ref, rr_ref, v_ref, s0_ref, y_ref, sout_ref, s_ref, vec_ref, *, tc):
    c = pl.program_id(1)
    nj = RW_HEAD_DIM // SUBLANES
    first8 = (lax.broadcasted_iota(jnp.int32, (SUBLANES, LANES), 1) // RW_HEADS) % 2 == 0

    @pl.when(c == 0)
    def _():
        s_ref[...] = s0_ref[0]

    def unpack(t, slot):
        for pair, src_ref in enumerate((wa_ref, bk_ref)):
            for jb in range(nj):
                rows = pl.ds(jb * SUBLANES, SUBLANES)
                x = src_ref[0, t, rows, :]
                vec_ref[slot, 2 * pair, rows, :] = jnp.where(first8, x, pltpu.roll(x, RW_HEADS, 1))
                vec_ref[slot, 2 * pair + 1, rows, :] = jnp.where(first8, pltpu.roll(x, LANES - RW_HEADS, 1), x)

    def step(t, slot):
        unpack(jnp.minimum(t + 1, tc - 1), 1 - slot)

        def vec(which, jb):
            rows = pl.ds(jb * SUBLANES, SUBLANES)
            return rr_ref[0, t, rows, :] if which == 4 else vec_ref[slot, which, rows, :]

        for i0 in range(0, SCAN_ROWS, SCAN_ROW_GROUP):
            rows = range(i0, i0 + SCAN_ROW_GROUP)
            s = {i: [s_ref[i, jb * SUBLANES:(jb + 1) * SUBLANES, :] for jb in range(nj)] for i in rows}
            acc = {}
            for jb in range(nj):
                a = vec(1, jb)
                for i in rows:
                    acc[i] = s[i][jb] * a if jb == 0 else acc[i] + s[i][jb] * a
            sa = {i: _sublane_allsum(acc[i]) for i in rows}
            vb = {i: jnp.broadcast_to(v_ref[0, t, i:i + 1, :], (SUBLANES, LANES)) for i in rows}
            accy = {}
            for jb in range(nj):
                w, b, k, r = vec(0, jb), vec(2, jb), vec(3, jb), vec(4, jb)
                for i in rows:
                    sn = s[i][jb] * w + sa[i] * b + vb[i] * k
                    s_ref[i, jb * SUBLANES:(jb + 1) * SUBLANES, :] = sn
                    accy[i] = sn * r if jb == 0 else accy[i] + sn * r
            for i in rows:
                y_ref[0, t, i:i + 1, :] = _sublane_allsum(accy[i])[0:1, :]

    def two_steps(t2, carry):
        step(2 * t2, 0)
        step(2 * t2 + 1, 1)
        return carry

    unpack(0, 0)
    lax.fori_loop(0, tc // 2, two_steps, 0)

    @pl.when(c == pl.num_programs(1) - 1)
    def _():
        sout_ref[0] = s_ref[...]


def _rwkv_scan(wa, bk, rr, vvec, s0, tc):
    g, t = wa.shape[0], wa.shape[1]
    jspec = pl.BlockSpec((1, tc, RW_HEAD_DIM, LANES), lambda gi, c: (gi, c, 0, 0))
    return pl.pallas_call(
        functools.partial(_rwkv_scan_kernel, tc=tc),
        grid=(g, t // tc),
        in_specs=[jspec, jspec, jspec,
                  pl.BlockSpec((1, tc, SCAN_ROWS, LANES), lambda gi, c: (gi, c, 0, 0)),
                  pl.BlockSpec((1, SCAN_ROWS, RW_HEAD_DIM, LANES), lambda gi, c: (gi, 0, 0, 0))],
        out_specs=[pl.BlockSpec((1, tc, SCAN_ROWS, LANES), lambda gi, c: (gi, c, 0, 0)),
                   pl.BlockSpec((1, SCAN_ROWS, RW_HEAD_DIM, LANES), lambda gi, c: (gi, 0, 0, 0))],
        out_shape=[jax.ShapeDtypeStruct((g, t, SCAN_ROWS, LANES), F32),
                   jax.ShapeDtypeStruct((g, SCAN_ROWS, RW_HEAD_DIM, LANES), F32)],
        scratch_shapes=[pltpu.VMEM((SCAN_ROWS, RW_HEAD_DIM, LANES), F32), pltpu.VMEM((2, 4, RW_HEAD_DIM, LANES), F32)],
        compiler_params=_cparams("arbitrary", "arbitrary"),
        name="rwkv_scan",
    )(wa, bk, rr, vvec, s0)


def _to_scan_pair(x, b):
    t, g = x.shape[0], b // SCAN_SEQS
    x = jnp.swapaxes(x.reshape(t, g, LANES, RW_HEAD_DIM), 2, 3)
    return x.reshape(1, t, RW_HEAD_DIM, LANES) if g == 1 else jnp.swapaxes(x, 0, 1)


def _to_scan_rows(x, b):
    t, g = x.shape[0], b // SCAN_SEQS
    x = x.reshape(t, g, SCAN_SEQS, RW_HEADS, 2, SCAN_ROWS)
    return jnp.transpose(x, (1, 0, 5, 2, 4, 3)).reshape(g, t, SCAN_ROWS, LANES)


def _from_scan_rows(y):
    g, t = y.shape[0], y.shape[1]
    y = y.reshape(g, t, SCAN_ROWS, SCAN_SEQS, 2, RW_HEADS)
    return jnp.transpose(y, (1, 0, 3, 5, 4, 2)).reshape(t, g * SCAN_SEQS * RW_WIDTH)


def _state_to_scan(s):
    b = s.shape[0]
    g = b // SCAN_SEQS
    s = s.reshape(g, SCAN_SEQS, RW_HEADS, 2, SCAN_ROWS, RW_HEAD_DIM)
    return jnp.transpose(s, (0, 4, 5, 1, 3, 2)).reshape(g, SCAN_ROWS, RW_HEAD_DIM, LANES)


def _state_from_scan(s, b):
    g = s.shape[0]
    s = s.reshape(g, SCAN_ROWS, RW_HEAD_DIM, SCAN_SEQS, 2, RW_HEADS)
    return jnp.transpose(s, (0, 3, 5, 4, 1, 2)).reshape(b, RW_HEADS, RW_HEAD_DIM, RW_HEAD_DIM)


def _gdn_prep_kernel(x_ref, init_ref, ab_ref, cw_ref, alog_ref, dt_ref, ones_ref, q_ref, k_ref, v_ref, gb_ref,
                     ext_ref, *, halo, bs, tm):
    cur = x_ref[...]
    _stage_tile(ext_ref, init_ref, cur, halo, tm)
    conv = cur * cw_ref[GDN_CONV - 1:GDN_CONV, :]
    for s in range(1, GDN_CONV):
        off = halo - s * bs
        conv = conv + ext_ref[off:off + tm, :] * cw_ref[GDN_CONV - 1 - s:GDN_CONV - s, :]
    qkv = conv * _sigmoid(conv)
    q = qkv[:, 0:GDN_WIDTH]
    k = qkv[:, GDN_WIDTH:2 * GDN_WIDTH]
    ones = ones_ref[...]
    q_ref[...] = q * lax.rsqrt(_seg_sum(q * q, ones) + 1e-6) * (GDN_HEAD_DIM ** -0.5)
    k_ref[...] = k * lax.rsqrt(_seg_sum(k * k, ones) + 1e-6)
    v_ref[...] = qkv[:, 2 * GDN_WIDTH:3 * GDN_WIDTH]
    ab = ab_ref[...]
    g = -jnp.exp(alog_ref[...]) * _softplus(ab + dt_ref[...])
    lane = lax.broadcasted_iota(jnp.int32, ab.shape, 1)
    gb_ref[...] = jnp.where(lane < GDN_HEADS, g, _sigmoid(ab))


def _gdn_prep(qkv_raw, init, ab, p, nb, tm, bs):
    n = qkv_raw.shape[0]
    nt = n // (nb * tm)
    halo = init.shape[1]
    row = lambda w: pl.BlockSpec((tm, w), lambda b, t: (b * nt + t, 0))
    consts = [p["gdn_conv_w"], p["gdn_alog"], p["gdn_dt"], p["ones_gdn"]]
    return pl.pallas_call(
        functools.partial(_gdn_prep_kernel, halo=halo, bs=bs, tm=tm),
        grid=(nb, nt),
        in_specs=[row(GDN_CONV_CH), pl.BlockSpec((1, halo, GDN_CONV_CH), lambda b, t: (b, 0, 0)), row(LANES)]
        + [_const_spec(c.shape) for c in consts],
        out_specs=[row(GDN_WIDTH)] * 3 + [row(LANES)],
        out_shape=[jax.ShapeDtypeStruct((n, GDN_WIDTH), F32)] * 3 + [jax.ShapeDtypeStruct((n, LANES), F32)],
        scratch_shapes=[pltpu.VMEM((halo + tm, GDN_CONV_CH), F32)],
        compiler_params=_cparams("arbitrary", "arbitrary"),
        name="gdn_prep",
    )(qkv_raw, init, ab, *consts)


def _unit_lower_inverses(a_list, eye, chunk):
    n = [-a for a in a_list]
    inv = [eye + x for x in n]
    for _ in range(int(math.log2(chunk)) - 1):
        ns = [_split_bf16(x) for x in n]
        n = [_dot3(x, x) for x in ns]
        ns = [_split_bf16(x) for x in n]
        inv = [x + _dot3(_split_bf16(x), y) for x, y in zip(inv, ns)]
    return inv


def _gdn_chunk_kernel(q_ref, k_ref, v_ref, gb_ref, s0_ref, o_ref, sout_ref, s_ref, *, chunk, nbb, group):
    ci = pl.program_id(1)

    @pl.when(ci == 0)
    def _():
        s_ref[...] = s0_ref[...]

    rows = lax.broadcasted_iota(jnp.int32, (chunk, chunk), 0)
    cols = lax.broadcasted_iota(jnp.int32, (chunk, chunk), 1)
    causal = rows >= cols
    strict = rows > cols
    eye = (rows == cols).astype(F32)
    tri = causal.astype(F32)

    def one_group(gi, carry):
        units = [(gi * group + j, h) for j in range(group) for h in range(GDN_HEADS)]
        lanes = lambda h: slice(h * GDN_HEAD_DIM, (h + 1) * GDN_HEAD_DIM)
        s = [s_ref[b, h] for b, h in units]
        q = [q_ref[b, :, lanes(h)] for b, h in units]
        k = [k_ref[b, :, lanes(h)] for b, h in units]
        v = [v_ref[b, :, lanes(h)] for b, h in units]
        gb = [gb_ref[gi * group + j] for j in range(group)]
        gcs = [_dot_hi(tri, x) for x in gb]
        gc = [gcs[i // GDN_HEADS][:, h:h + 1] for i, (_, h) in enumerate(units)]
        beta = [gb[i // GDN_HEADS][:, GDN_HEADS + h:GDN_HEADS + h + 1] for i, (_, h) in enumerate(units)]
        decay = [jnp.exp(jnp.where(causal, x - jnp.sum(eye * x, axis=0, keepdims=True), -jnp.inf)) for x in gc]
        kb = [x * y for x, y in zip(k, beta)]
        a = [jnp.where(strict, _dot_nt(x, y) * d, 0.0) for x, y, d in zip(kb, k, decay)]
        inv = _unit_lower_inverses(a, eye, chunk)
        egc = [jnp.exp(x) for x in gc]
        rhs = [jnp.concatenate([x * bt, y * e], axis=1) for x, bt, y, e in zip(v, beta, kb, egc)]
        sol = [_dot3(_split_bf16(x), _split_bf16(y)) for x, y in zip(inv, rhs)]
        wS = [_dot(x[:, GDN_HEAD_DIM:2 * GDN_HEAD_DIM], y) for x, y in zip(sol, s)]
        v_new = [x[:, 0:GDN_HEAD_DIM] - y for x, y in zip(sol, wS)]
        qk = [_dot_nt(x, y) * d for x, y, d in zip(q, k, decay)]
        o_state = [_dot(x * e, y) for x, e, y in zip(q, egc, s)]
        o_chunk = [_dot(x, y) for x, y in zip(qk, v_new)]
        g_last = [x[chunk - 1:chunk, :] for x in gc]
        kd_t = [(x * jnp.exp(gl - g)).T for x, gl, g in zip(k, g_last, gc)]
        s_add = [_dot(x, y) for x, y in zip(kd_t, v_new)]
        for i, (b, h) in enumerate(units):
            o_ref[b, :, lanes(h)] = o_state[i] + o_chunk[i]
            s_ref[b, h] = s[i] * jnp.exp(g_last[i]) + s_add[i]
        return carry

    if nbb == group:
        one_group(0, 0)
    else:
        lax.fori_loop(0, nbb // group, one_group, 0)

    @pl.when(ci == pl.num_programs(1) - 1)
    def _():
        sout_ref[...] = s_ref[...]


def _gdn_chunks(q, k, v, gb, s0, chunk):
    b, t_len, _ = q.shape
    nbb = min(b, GDN_SEQ_BLOCK)
    group = min(nbb, GDN_SEQ_GROUP)
    blk = lambda w: pl.BlockSpec((nbb, chunk, w), lambda bi, ci: (bi, ci, 0))
    st = pl.BlockSpec((nbb, GDN_HEADS, GDN_HEAD_DIM, GDN_HEAD_DIM), lambda bi, ci: (bi, 0, 0, 0))
    return pl.pallas_call(
        functools.partial(_gdn_chunk_kernel, chunk=chunk, nbb=nbb, group=group),
        grid=(b // nbb, t_len // chunk),
        in_specs=[blk(GDN_WIDTH)] * 3 + [blk(LANES), st],
        out_specs=[blk(GDN_WIDTH), st],
        out_shape=[jax.ShapeDtypeStruct(q.shape, F32), jax.ShapeDtypeStruct(s0.shape, F32)],
        scratch_shapes=[pltpu.VMEM((nbb, GDN_HEADS, GDN_HEAD_DIM, GDN_HEAD_DIM), F32)],
        compiler_params=_cparams("arbitrary", "arbitrary"),
        name="gdn_chunks",
    )(q, k, v, gb, s0)


def _route(logits, count_ref):
    lane = lax.broadcasted_iota(jnp.int32, logits.shape, 1)
    neg = -jnp.inf
    is_group = (lane >= N_EXPERTS) & (lane < N_EXPERTS + N_GROUPS)
    gl = jnp.where(is_group, logits, neg)
    gmax = jnp.max(gl, axis=-1, keepdims=True)
    gp = 1.0 / jnp.sum(jnp.exp(gl - gmax), axis=-1, keepdims=True)
    gidx = jnp.min(jnp.where(gl == gmax, lane, LANES), axis=-1, keepdims=True) - N_EXPERTS
    lo = gidx * EXPERTS_PER_GROUP
    el = jnp.where((lane >= lo) & (lane < lo + EXPERTS_PER_GROUP), logits, neg)
    m1 = jnp.max(el, axis=-1, keepdims=True)
    i1 = jnp.min(jnp.where(el == m1, lane, LANES), axis=-1, keepdims=True)
    el2 = jnp.where(lane == i1, neg, el)
    m2 = jnp.max(el2, axis=-1, keepdims=True)
    i2 = jnp.min(jnp.where(el2 == m2, lane, LANES), axis=-1, keepdims=True)
    e2 = jnp.exp(m2 - m1)
    den = 1.0 + e2
    tm = logits.shape[0]
    hit1 = lane == i1
    hit2 = lane == i2
    chosen = jnp.where(hit1 | hit2, 1.0, 0.0)
    before = lax.broadcasted_iota(jnp.int32, (tm, tm), 0) > lax.broadcasted_iota(jnp.int32, (tm, tm), 1)
    seen = jnp.dot(before.astype(BF16), chosen.astype(BF16), preferred_element_type=F32) + count_ref[...]
    rank1 = jnp.sum(jnp.where(hit1, seen, 0.0), axis=-1, keepdims=True)
    rank2 = jnp.sum(jnp.where(hit2, seen, 0.0), axis=-1, keepdims=True)
    count_ref[...] += jnp.sum(chosen, axis=0, keepdims=True)
    fields = (i1.astype(F32), i2.astype(F32), gp / den, gp * e2 / den, rank1, rank2)
    out = jnp.zeros(logits.shape, F32)
    for j, f in enumerate(fields):
        out = jnp.where(lane == j, f, out)
    return out


def _merge_kernel(y_ref, r_ref, k_ref, v_ref, g_ref, o_ref, z_ref, gate_ref, h_ref, lnw_ref, lnb_ref, rk_ref,
                  ones_ref, gnw_ref, woa_ref, wob_ref, wo_ref, n2_ref, rw_ref, rb_ref, *rest, nt_main):
    tail_refs, (h1_ref, xt_ref, mg_ref, cnt_ref) = rest[:-4], rest[-4:]
    i = pl.program_id(0)

    @pl.when(i == 0)
    def _():
        cnt_ref[...] = tail_refs[3][...] if tail_refs else jnp.zeros_like(cnt_ref)

    if tail_refs:
        @pl.when(i >= nt_main)
        def _():
            h1_ref[...] = tail_refs[0][...]
            xt_ref[...] = tail_refs[1][...]
            mg_ref[...] = tail_refs[2][...]

        pl.when(i < nt_main)(functools.partial(
            _merge_tile, y_ref, r_ref, k_ref, v_ref, g_ref, o_ref, z_ref, gate_ref, h_ref, lnw_ref, lnb_ref, rk_ref,
            ones_ref, gnw_ref, woa_ref, wob_ref, wo_ref, n2_ref, rw_ref, rb_ref, h1_ref, xt_ref, mg_ref, cnt_ref))
    else:
        _merge_tile(y_ref, r_ref, k_ref, v_ref, g_ref, o_ref, z_ref, gate_ref, h_ref, lnw_ref, lnb_ref, rk_ref,
                    ones_ref, gnw_ref, woa_ref, wob_ref, wo_ref, n2_ref, rw_ref, rb_ref, h1_ref, xt_ref, mg_ref, cnt_ref)


def _merge_tile(y_ref, r_ref, k_ref, v_ref, g_ref, o_ref, z_ref, gate_ref, h_ref, lnw_ref, lnb_ref, rk_ref,
                ones_ref, gnw_ref, woa_ref, wob_ref, wo_ref, n2_ref, rw_ref, rb_ref, h1_ref, xt_ref, mg_ref, cnt_ref):
    tm = h_ref.shape[0]
    halves = [pl.ds(0, tm // 2), pl.ds(tm // 2, tm // 2)]
    ones = ones_ref[...]
    inv_n = 1.0 / RW_HEAD_DIM
    y = [y_ref[p, :] for p in halves]
    mu = [_seg_sum(a, ones) * inv_n for a in y]
    rkk = [_seg_sum(r_ref[p, :] * k_ref[p, :] * rk_ref[...], ones) for p in halves]
    yc = [a - m for a, m in zip(y, mu)]
    var = [_seg_sum(c * c, ones) * inv_n for c in yc]
    ya = [(c * lax.rsqrt(s + RW_GN_EPS) * lnw_ref[...] + lnb_ref[...] + b * v_ref[p, :]) * g_ref[p, :]
          for c, s, b, p in zip(yc, var, rkk, halves)]

    gnw = gnw_ref[...]
    yb = []
    for p in halves:
        z = z_ref[p, :]
        heads = [_rmsnorm(o_ref[p, h * GDN_HEAD_DIM:(h + 1) * GDN_HEAD_DIM], gnw) for h in range(GDN_HEADS)]
        yb.append(jnp.concatenate(heads, axis=1) * (z * _sigmoid(z)))

    out_a = [_dot(a, woa_ref[...]) for a in ya]
    out_b = [_dot(b, wob_ref[...]) for b in yb]
    merged = [_sigmoid(gate_ref[p, 0:D_MODEL]) * a + _sigmoid(gate_ref[p, D_MODEL:2 * D_MODEL]) * b
              for p, a, b in zip(halves, out_a, out_b)]
    h1 = [h_ref[p, :] + _dot(m, wo_ref[...]) for p, m in zip(halves, merged)]
    xt = [_rmsnorm(a, n2_ref[...]) for a in h1]
    for p, a, b in zip(halves, h1, xt):
        h1_ref[p, :] = a
        xt_ref[p, :] = b
    logits = jnp.concatenate([_dot_f32(a, rw_ref[...]) for a in xt], axis=0) + rb_ref[...]
    mg_ref[...] = _route(logits, cnt_ref)


def _merge(y, rr, bk, v, g, o, z, gates, h, p, tm, wide_nt, tail):
    n = h.shape[0]
    nt_main = n // tm
    n_out = n + (tail[0].shape[0] if tail else 0)
    mi = (lambda i: jnp.minimum(i, nt_main - 1)) if tail else (lambda i: i)
    row = lambda w: pl.BlockSpec((tm, w), lambda i: (mi(i), 0))
    out_row = lambda w: pl.BlockSpec((tm, w), lambda i: (i, 0))
    tail_row = lambda w: pl.BlockSpec((tm, w), lambda i: (jnp.maximum(i - nt_main, 0), 0))
    if wide_nt is None:
        part = lambda per_seq, k: pl.BlockSpec((tm, RW_WIDTH), lambda i: (mi(i), k))
    else:
        part = lambda per_seq, k: pl.BlockSpec(
            (tm, RW_WIDTH), lambda i: (mi(i) % wide_nt, (mi(i) // wide_nt) * per_seq + k))
    consts = [p["rw_lnx_w"], p["rw_lnx_b"], p["rw_r_k"], p["ones_rw"], p["gdn_norm_w"], p["w_oA"], p["w_oB"],
              p["w_o"], p["norm2_w"], p["router_w"], p["router_b"]]
    tail_specs = [tail_row(D_MODEL), tail_row(D_MODEL), tail_row(LANES), _const_spec((1, LANES))] if tail else []
    return pl.pallas_call(
        functools.partial(_merge_kernel, nt_main=nt_main),
        grid=(n_out // tm,),
        in_specs=[part(1, 0), part(2, 0), part(2, 1), part(1, 0)] + [row(RW_WIDTH)] * 3
        + [row(2 * D_MODEL), row(D_MODEL)] + [_const_spec(c.shape) for c in consts] + tail_specs,
        out_specs=[out_row(D_MODEL), out_row(D_MODEL), out_row(LANES), _const_spec((1, LANES))],
        out_shape=[jax.ShapeDtypeStruct((n_out, D_MODEL), F32), jax.ShapeDtypeStruct((n_out, D_MODEL), F32),
                   jax.ShapeDtypeStruct((n_out, LANES), F32), jax.ShapeDtypeStruct((1, LANES), F32)],
        compiler_params=_cparams("arbitrary"),
        name="merge",
    )(y, rr, bk, v, g, o, z, gates, h, *consts, *(tail or ()))


def _row_of(ref3, p):
    return ref3.at[lax.shift_right_logical(p, 3), pl.ds(lax.bitwise_and(p, SUBLANES - 1), 1)]


def _start_row_gather(src_hbm, idx_ref, dst_ref, sem, n, priority):
    def body(g, carry):
        for k in range(SUBLANES):
            src = _row_of(src_hbm, idx_ref[0, 0, g * SUBLANES + k])
            pltpu.make_async_copy(src, dst_ref.at[g, pl.ds(k, 1)], sem).start(priority)
        return carry

    lax.fori_loop(0, n // SUBLANES, body, 0)


def _wait_row_gather(src_hbm, dst_ref, sem, n):
    def body(r, carry):
        pltpu.make_async_copy(src_hbm.at[0, pl.ds(0, 1)], dst_ref.at[0, pl.ds(0, 1)], sem).wait()
        return carry

    lax.fori_loop(0, n, body, 0, unroll=8)


def _moe_dispatch_kernel(p1_ref, p2_ref, x_ref, zeros_hbm, xs_hbm, xbuf, sems, *, tm):
    del zeros_hbm
    i = pl.program_id(0)
    nt = pl.num_programs(0)
    slot = i % 2

    def wait_tile(s):
        def body(r, carry):
            pltpu.make_async_copy(xbuf.at[s, 0, pl.ds(0, 1)], xs_hbm.at[0, pl.ds(0, 1)], sems.at[s]).wait()
            return carry

        lax.fori_loop(0, 2 * tm, body, 0, unroll=8)

    @pl.when(i >= 2)
    def _():
        wait_tile(slot)

    xbuf[slot] = x_ref[...].reshape(tm // SUBLANES, SUBLANES, D_MODEL)

    def issue(g, carry):
        for k in range(SUBLANES):
            src = xbuf.at[slot, g, pl.ds(k, 1)]
            for pos_ref, priority in ((p1_ref, 0), (p2_ref, 1)):
                dst = _row_of(xs_hbm, pos_ref[0, 0, g * SUBLANES + k])
                pltpu.make_async_copy(src, dst, sems.at[slot]).start(priority)
        return carry

    lax.fori_loop(0, tm // SUBLANES, issue, 0)

    @pl.when(i == nt - 1)
    def _():
        wait_tile(slot)

        @pl.when(nt > 1)
        def _():
            wait_tile(1 - slot)


def _moe_dispatch(xt, pos1, pos2, n_sorted, tm):
    n = xt.shape[0]
    idx = pl.BlockSpec((1, 1, tm), lambda i: (i, 0, 0), memory_space=pltpu.SMEM)
    return pl.pallas_call(
        functools.partial(_moe_dispatch_kernel, tm=tm),
        grid=(n // tm,),
        in_specs=[idx, idx, pl.BlockSpec((tm, D_MODEL), lambda i: (i, 0)), pl.BlockSpec(memory_space=pl.ANY)],
        out_specs=pl.BlockSpec(memory_space=pl.ANY),
        out_shape=jax.ShapeDtypeStruct((n_sorted // SUBLANES, SUBLANES, D_MODEL), F32),
        scratch_shapes=[pltpu.VMEM((2, tm // SUBLANES, SUBLANES, D_MODEL), F32), pltpu.SemaphoreType.DMA((2,))],
        input_output_aliases={3: 0},
        compiler_params=_cparams("arbitrary"),
        name="moe_dispatch",
    )(pos1, pos2, xt, jnp.zeros((n_sorted // SUBLANES, SUBLANES, D_MODEL), F32)).reshape(n_sorted, D_MODEL)


def _moe_experts_kernel(te_ref, tv_ref, x_ref, wg_ref, wu_ref, wd_ref, y_ref, wg_bf, wu_bf, wd_bf):
    i = pl.program_id(0)

    @pl.when(jnp.logical_or(i == 0, te_ref[i] != te_ref[jnp.maximum(i - 1, 0)]))
    def _():
        wg_bf[...] = wg_ref[0].astype(BF16)
        wu_bf[...] = wu_ref[0].astype(BF16)
        wd_bf[...] = wd_ref[0].astype(BF16)

    @pl.when(tv_ref[i] == 1)
    def _():
        tm = x_ref.shape[0]
        halves = [pl.ds(0, tm // 2), pl.ds(tm // 2, tm // 2)]
        xb = [x_ref[p, :].astype(BF16) for p in halves]
        hg = [jnp.dot(a, wg_bf[...], preferred_element_type=F32) for a in xb]
        hu = [jnp.dot(a, wu_bf[...], preferred_element_type=F32) for a in xb]
        hid = [(a * _sigmoid(a) * b).astype(BF16) for a, b in zip(hg, hu)]
        out = [jnp.dot(a, wd_bf[...], preferred_element_type=F32) for a in hid]
        for p, a in zip(halves, out):
            y_ref[p, :] = a

    @pl.when(tv_ref[i] == 0)
    def _():
        y_ref[...] = jnp.zeros_like(y_ref)


def _moe_experts(xs, tile_expert, tile_valid, wg, wu, wd, tm):
    nt = xs.shape[0] // tm
    wspec = lambda shape: pl.BlockSpec((1,) + shape, lambda i, te, tv: (te[i], 0, 0))
    rows = pl.BlockSpec((tm, D_MODEL), lambda i, te, tv: (i, 0))
    return pl.pallas_call(
        _moe_experts_kernel,
        grid_spec=pltpu.PrefetchScalarGridSpec(
            num_scalar_prefetch=2,
            grid=(nt,),
            in_specs=[rows, wspec((D_MODEL, EXPERT_FF)), wspec((D_MODEL, EXPERT_FF)), wspec((EXPERT_FF, D_MODEL))],
            out_specs=rows,
            scratch_shapes=[pltpu.VMEM((D_MODEL, EXPERT_FF), BF16), pltpu.VMEM((D_MODEL, EXPERT_FF), BF16),
                            pltpu.VMEM((EXPERT_FF, D_MODEL), BF16)],
        ),
        out_shape=jax.ShapeDtypeStruct(xs.shape, F32),
        compiler_params=_cparams("arbitrary"),
        name="moe_experts",
    )(tile_expert, tile_valid, xs, wg, wu, wd)


def _moe_combine_kernel(cur1_ref, cur2_ref, nxt1_ref, nxt2_ref, ys_hbm, h1_ref, mg_ref, nf_ref, o_ref, otail_ref,
                        buf1, buf2, sems, *, tm, nt_main):
    i = pl.program_id(0)
    nt = pl.num_programs(0)
    slot = i % 2

    @pl.when(i == 0)
    def _():
        _start_row_gather(ys_hbm, cur1_ref, buf1.at[0], sems.at[0, 0], tm, 0)
        _start_row_gather(ys_hbm, cur2_ref, buf2.at[0], sems.at[1, 0], tm, 1)

    @pl.when(i + 1 < nt)
    def _():
        _start_row_gather(ys_hbm, nxt1_ref, buf1.at[1 - slot], sems.at[0, 1 - slot], tm, 0)
        _start_row_gather(ys_hbm, nxt2_ref, buf2.at[1 - slot], sems.at[1, 1 - slot], tm, 1)

    _wait_row_gather(ys_hbm, buf1.at[slot], sems.at[0, slot], tm)
    _wait_row_gather(ys_hbm, buf2.at[slot], sems.at[1, slot], tm)
    mg = mg_ref[...]
    rows = lambda buf: buf[slot].reshape(tm, D_MODEL)
    moe = mg[:, 2:3] * rows(buf1) + mg[:, 3:4] * rows(buf2)
    out = _rmsnorm(h1_ref[...] + moe, nf_ref[...])

    @pl.when(i < nt_main)
    def _():
        o_ref[...] = out

    @pl.when(i >= nt_main)
    def _():
        otail_ref[...] = out


def _moe_combine(ys, pos1, pos2, h1, mg, nf, tm, n_main):
    n = h1.shape[0]
    nt = n // tm
    nt_main = n_main // tm
    cur = pl.BlockSpec((1, 1, tm), lambda i: (i, 0, 0), memory_space=pltpu.SMEM)
    nxt = pl.BlockSpec((1, 1, tm), lambda i: (jnp.minimum(i + 1, nt - 1), 0, 0), memory_space=pltpu.SMEM)
    row = lambda w: pl.BlockSpec((tm, w), lambda i: (i, 0))
    return pl.pallas_call(
        functools.partial(_moe_combine_kernel, tm=tm, nt_main=nt_main),
        grid=(nt,),
        in_specs=[cur, cur, nxt, nxt, pl.BlockSpec(memory_space=pl.ANY), row(D_MODEL), row(LANES),
                  _const_spec((1, D_MODEL))],
        out_specs=[pl.BlockSpec((tm, D_MODEL), lambda i: (jnp.minimum(i, nt_main - 1), 0)),
                   pl.BlockSpec((tm, D_MODEL), lambda i: (jnp.maximum(i - nt_main, 0), 0))],
        out_shape=[jax.ShapeDtypeStruct((n_main, D_MODEL), F32), jax.ShapeDtypeStruct((n - n_main, D_MODEL), F32)],
        scratch_shapes=[pltpu.VMEM((2, tm // SUBLANES, SUBLANES, D_MODEL), F32)] * 2 + [pltpu.SemaphoreType.DMA((2, 2))],
        compiler_params=_cparams("arbitrary"),
        name="moe_combine",
    )(pos1, pos2, pos1, pos2, ys.reshape(-1, SUBLANES, D_MODEL), h1, mg, nf)


def _moe(xt, h1, mg, counts, wg, wu, wd, nf, tm_expert, tm_token, n_main):
    n = xt.shape[0]
    i32 = jnp.int32
    expert = jnp.arange(N_EXPERTS, dtype=i32)
    cnt = counts[0, :N_EXPERTS].astype(i32)
    padded = (cnt + tm_expert - 1) // tm_expert * tm_expert
    ends = jnp.sum(jnp.where(expert[:, None] >= expert[None, :], padded[None, :], 0), axis=1)
    starts = ends - padded
    start_of = lambda e: jnp.sum(jnp.where(e[:, None] == expert[None, :], starts[None, :], 0), axis=1)
    pos1 = start_of(mg[:, 0].astype(i32)) + mg[:, 4].astype(i32)
    pos2 = start_of(mg[:, 1].astype(i32)) + mg[:, 5].astype(i32)
    nt = (2 * n + N_EXPERTS * (tm_expert - 1)) // tm_expert + 1
    tile_start = jnp.arange(nt, dtype=i32) * tm_expert
    tile_valid = (tile_start < jnp.sum(padded)).astype(i32)
    tile_expert = jnp.sum((tile_start[:, None] >= ends[None, :]).astype(i32), axis=1)
    last_used = jnp.max(jnp.where(padded > 0, expert, 0))
    tile_expert = jnp.where(tile_valid == 1, tile_expert, last_used)
    shape = (n // tm_token, 1, tm_token)
    pos1, pos2 = pos1.reshape(shape), pos2.reshape(shape)
    xs = _moe_dispatch(xt, pos1, pos2, nt * tm_expert, tm_token)
    ys = _moe_experts(xs, tile_expert, tile_valid, wg, wu, wd, tm_expert)
    return _moe_combine(ys, pos1, pos2, h1, mg, nf, tm_token, n_main)


def _pad_rows(x, rows):
    return jnp.pad(x, ((0, rows - x.shape[0]),) + ((0, 0),) * (x.ndim - 1))


def _mixer(x_rows, p, *, batch, t_len, time_major, tm, rw_prev, conv_prev, wkv0, gdn0, chunk, with_merge, tail=None):
    n = batch * t_len
    xn, rw_cur, qkv_raw, z, gates, ab = _in_proj(x_rows, p["norm1_w"], p["w_in"], min(tm, n))

    if time_major:
        nb, bs, tmt = 1, batch, n
        rw_init = rw_prev[None]
        conv_init = jnp.transpose(conv_prev, (1, 0, 2)).reshape(1, (GDN_CONV - 1) * batch, GDN_CONV_CH)
    else:
        nb, bs, tmt = batch, 1, min(tm, t_len)
        rw_init = jnp.pad(rw_prev[:, None, :], ((0, 0), (SUBLANES - 1, 0), (0, 0)))
        conv_init = jnp.pad(conv_prev, ((0, 0), (SUBLANES - (GDN_CONV - 1), 0), (0, 0)))

    wide = not time_major
    wa, bk, rr, v, g = _rwkv_prep(rw_cur, rw_init, p, nb, tmt, bs, wide)
    q, kg, vg, gb = _gdn_prep(qkv_raw, conv_init, ab, p, nb, tmt, bs)

    def seq_major(a):
        if time_major:
            return jnp.transpose(a.reshape(t_len, batch, a.shape[-1]), (1, 0, 2))
        return a.reshape(batch, t_len, a.shape[-1])

    def rows_like_x(a):
        if time_major:
            a = jnp.transpose(a, (1, 0, 2))
        return a.reshape(n, a.shape[-1])

    t_pad = -(-t_len // chunk) * chunk
    def gdn_seq(a):
        a = seq_major(a)
        if t_pad != t_len:
            a = jnp.pad(a, ((0, 0), (0, t_pad - t_len), (0, 0)))
        return a
    o_pad, gdn = _gdn_chunks(gdn_seq(q), gdn_seq(kg), gdn_seq(vg), gdn_seq(gb), gdn0, chunk)

    bpad = -(-batch // SCAN_SEQS) * SCAN_SEQS

    def by_time(a):
        a = a.reshape(t_len, -1)
        return jnp.pad(a, ((0, 0), (0, a.shape[1] // batch * (bpad - batch))))

    scan_in = tuple(_to_scan_pair(by_time(a), bpad) for a in (wa, bk, rr)) + (_to_scan_rows(by_time(v), bpad),)
    scan_in, o_pad = lax.optimization_barrier((scan_in, o_pad))
    tc = min(t_len, 32)
    y_scan, s_scan = _rwkv_scan(*scan_in, _state_to_scan(_pad_rows(wkv0, bpad)), tc)
    y_rw = _from_scan_rows(y_scan)[:, :batch * RW_WIDTH]
    if time_major:
        y_rw = y_rw.reshape(n, RW_WIDTH)
    wkv = _state_from_scan(s_scan, bpad)[:batch]
    o = rows_like_x(o_pad[:, :t_len])

    out = dict(xn=xn, rw_cur=rw_cur, qkv_raw=qkv_raw, wkv=wkv, gdn=gdn)
    if with_merge:
        out["h1"], out["xt"], out["mg"], out["counts"] = _merge(y_rw, rr, bk, v, g, o, z, gates, x_rows, p, min(tm, n),
                                                                 t_len // tmt if wide else None, tail)
    return out


def kernel(x_prompt, x_sample, state_shift, state_wkv, state_conv, state_gdn, meta_tokens, norm1_w, w_in, mu_shift, rw_w0, rw_w2, rw_a0, rw_a2, rw_g2, rw_k_k, rw_k_a, rw_r_k, rw_lnx_w, rw_lnx_b, gdn_conv_w, gdn_A_log, gdn_dt_bias, gdn_norm_w, w_oA, w_oB, w_o, norm2_w, router_g, router_g_b, router_e, router_e_b, moe_w_gate, moe_w_up, moe_w_down, norm_f_w):
    bp, sp, _ = x_prompt.shape
    bsm, ss, _ = x_sample.shape
    row = lambda a: a[0].reshape(1, -1)
    w0 = w_in[0]
    gcol = RW_COLS + 4 * GDN_WIDTH + 2 * GDN_HEADS
    w_packed = jnp.concatenate(
        [w0[:, :RW_COLS + 4 * GDN_WIDTH], w0[:, gcol:], w0[:, RW_COLS + 4 * GDN_WIDTH:gcol],
         jnp.zeros((D_MODEL, LANES - 2 * GDN_HEADS), F32)], axis=1).astype(BF16)
    lane_pad = lambda a: jnp.pad(a, ((0, 0), (0, LANES - a.shape[1])))
    p = dict(
        norm1_w=row(norm1_w), w_in=w_packed, mu_shift=row(mu_shift), rw_w0=row(rw_w0), rw_w2=rw_w2[0],
        rw_a0=row(rw_a0), rw_a2=rw_a2[0], rw_g2=rw_g2[0], rw_k_k=row(rw_k_k), rw_k_a=row(rw_k_a),
        rw_r_k=row(rw_r_k), rw_lnx_w=row(rw_lnx_w), rw_lnx_b=row(rw_lnx_b),
        ones_rw=_block_ones(RW_WIDTH, RW_HEAD_DIM), ones_gdn=_block_ones(GDN_WIDTH, GDN_HEAD_DIM),
        gdn_conv_w=gdn_conv_w[0], gdn_alog=lane_pad(row(gdn_A_log)), gdn_dt=lane_pad(row(gdn_dt_bias)),
        gdn_norm_w=row(gdn_norm_w), w_oA=w_oA[0].astype(BF16), w_oB=w_oB[0].astype(BF16),
        w_o=w_o[0].astype(BF16), norm2_w=row(norm2_w),
        router_w=lane_pad(jnp.concatenate([router_e[0], router_g[0]], axis=1)),
        router_b=lane_pad(jnp.concatenate([row(router_e_b), row(router_g_b)], axis=1)),
    )
    nf = norm_f_w.reshape(1, -1)
    dt = x_prompt.dtype

    meta = _mixer(meta_tokens.astype(dt), p, batch=1, t_len=N_META, time_major=False, tm=N_META,
                  rw_prev=jnp.zeros((1, RW_COLS), dt), conv_prev=jnp.zeros((1, GDN_CONV - 1, GDN_CONV_CH), dt),
                  wkv0=jnp.zeros((1, RW_HEADS, RW_HEAD_DIM, RW_HEAD_DIM), dt),
                  gdn0=jnp.zeros((1, GDN_HEADS, GDN_HEAD_DIM, GDN_HEAD_DIM), dt), chunk=N_META, with_merge=False)
    rep = lambda a: jnp.broadcast_to(a, (bp,) + a.shape[1:])

    xs_rows = jnp.transpose(x_sample, (1, 0, 2)).reshape(ss * bsm, D_MODEL)
    prev_s = _in_proj(state_shift[0].astype(dt), p["norm1_w"], w_packed, bsm, norm=False)[1]
    sample = _mixer(xs_rows, p, batch=bsm, t_len=ss, time_major=True, tm=256,
                    rw_prev=prev_s, conv_prev=state_conv[0].astype(dt), wkv0=state_wkv[0], gdn0=state_gdn[0],
                    chunk=SUBLANES, with_merge=True)

    prompt = _mixer(x_prompt.reshape(bp * sp, D_MODEL), p, batch=bp, t_len=sp, time_major=False, tm=256,
                    rw_prev=rep(meta["rw_cur"][N_META - 1:]), conv_prev=rep(meta["qkv_raw"][None, N_META - 3:]),
                    wkv0=rep(meta["wkv"]), gdn0=rep(meta["gdn"]), chunk=GDN_CHUNK, with_merge=True,
                    tail=(sample["h1"], sample["xt"], sample["mg"], sample["counts"]))

    y_p, y_s = _moe(prompt["xt"], prompt["h1"], prompt["mg"], prompt["counts"], moe_w_gate[0], moe_w_up[0],
                    moe_w_down[0], nf, 256, 256, bp * sp)

    y_prompt = y_p.reshape(bp, sp, D_MODEL)
    y_sample = jnp.transpose(y_s.reshape(ss, bsm, D_MODEL), (1, 0, 2))
    shift_p = prompt["xn"].reshape(bp, sp, D_MODEL)[:, -1][None]
    conv_p = prompt["qkv_raw"].reshape(bp, sp, GDN_CONV_CH)[:, sp - (GDN_CONV - 1):][None]
    shift_s = sample["xn"].reshape(ss, bsm, D_MODEL)[-1][None].astype(state_shift.dtype)
    xpad_s = jnp.concatenate([state_conv[0].astype(dt),
                              jnp.transpose(sample["qkv_raw"].reshape(ss, bsm, GDN_CONV_CH), (1, 0, 2))], axis=1)
    conv_s = xpad_s[:, ss:][None].astype(state_conv.dtype)
    return (y_prompt, y_sample, shift_p, prompt["wkv"][None], conv_p, prompt["gdn"][None],
            shift_s, sample["wkv"][None].astype(state_wkv.dtype), conv_s, sample["gdn"][None].astype(state_gdn.dtype))
```

```python
import functools
import math

import jax
import jax.numpy as jnp
from jax import lax
from jax.experimental import pallas as pl
from jax.experimental.pallas import tpu as pltpu

F32 = jnp.float32
BF16 = jnp.bfloat16
HIGHEST = lax.Precision.HIGHEST

D_MODEL = 1024
N_META = 16
RW_WIDTH = 512
RW_HEAD_DIM = 64
RW_HEADS = 8
RW_DECAY_LORA = 64
RW_AAA_LORA = 64
RW_GATE_LORA = 128
RW_COLS = 3 * RW_WIDTH + RW_DECAY_LORA + RW_AAA_LORA + RW_GATE_LORA
RW_GN_EPS = RW_HEAD_DIM * 1e-5
GDN_WIDTH = 512
GDN_HEAD_DIM = 128
GDN_HEADS = 4
GDN_CONV = 4
GDN_CONV_CH = 3 * GDN_WIDTH
GDN_CHUNK = 64
N_GROUPS = 4
EXPERTS_PER_GROUP = 8
N_EXPERTS = 32
EXPERT_FF = 512
RMS_EPS = 1e-6

LANES = 128
SUBLANES = 8
VMEM_LIMIT_BYTES = 56 * 1024 * 1024

COL_RW = 0
COL_QKV = COL_RW + RW_COLS
COL_Z = COL_QKV + GDN_CONV_CH
COL_AB = COL_Z + GDN_WIDTH
COL_GATE = COL_AB + 2 * GDN_HEADS

SCAN_SEQS = 8
SCAN_ROWS = RW_HEAD_DIM // 2
SCAN_ROW_GROUP = 1

GDN_SEQ_BLOCK = 8
GDN_SEQ_GROUP = 8


def _cparams(*sem):
    return pltpu.CompilerParams(dimension_semantics=sem, vmem_limit_bytes=VMEM_LIMIT_BYTES)


def _const_spec(shape):
    zeros = (0,) * len(shape)
    return pl.BlockSpec(shape, lambda *_: zeros)


def _dot(a, b):
    return jnp.dot(a.astype(BF16), b.astype(BF16), preferred_element_type=F32)


def _dot_hi(a, b):
    return jnp.dot(a, b, preferred_element_type=F32, precision=HIGHEST)


def _split_bf16(a):
    hi = a.astype(BF16)
    return hi, (a - hi.astype(F32)).astype(BF16)


def _dot3(a, b):
    d = lambda x, y: jnp.dot(x, y, preferred_element_type=F32)
    return d(a[0], b[0]) + d(a[0], b[1]) + d(a[1], b[0])


def _dot_f32(a, b):
    return _dot3(_split_bf16(a), _split_bf16(b))


def _seg_sum(x, ones):
    hi, lo = _split_bf16(x)
    return jnp.dot(hi, ones, preferred_element_type=F32) + jnp.dot(lo, ones, preferred_element_type=F32)


def _dot_nt(a, b):
    dims = (((1,), (1,)), ((), ()))
    return lax.dot_general(a.astype(BF16), b.astype(BF16), dims, preferred_element_type=F32)


def _sigmoid(x):
    return 1.0 / (1.0 + jnp.exp(-x))


def _softplus(x):
    return jnp.maximum(x, 0.0) + jnp.log1p(jnp.exp(-jnp.abs(x)))


def _rmsnorm(x, w):
    ms = jnp.mean(x * x, axis=-1, keepdims=True)
    return x * lax.rsqrt(ms + RMS_EPS) * w


def _block_ones(width, seg):
    r = jnp.arange(width) // seg
    return (r[:, None] == r[None, :]).astype(BF16)


def _in_proj_kernel(x_ref, nw_ref, w_ref, wgate_ref, wab_ref, xn_ref, rw_ref, qkv_ref, z_ref, gate_ref, ab_ref, *,
                    norm):
    x = x_ref[...]
    xn = _rmsnorm(x, nw_ref[...]) if norm else x
    xn_ref[...] = xn
    xb = xn.astype(BF16)
    rw_ref[...] = jnp.dot(xb, w_ref[:, COL_RW:COL_QKV], preferred_element_type=F32)
    qkv_ref[...] = jnp.dot(xb, w_ref[:, COL_QKV:COL_Z], preferred_element_type=F32)
    z_ref[...] = jnp.dot(xb, w_ref[:, COL_Z:COL_AB], preferred_element_type=F32)
    gate_ref[...] = jnp.dot(xb, wgate_ref[...], preferred_element_type=F32)
    ab_ref[...] = jnp.dot(xb, wab_ref[...], preferred_element_type=F32)


def _in_proj(x, norm_w, weights, tm, norm=True):
    n = x.shape[0]
    widths = (D_MODEL, RW_COLS, GDN_CONV_CH, GDN_WIDTH, 2 * D_MODEL, LANES)
    row = lambda w: pl.BlockSpec((tm, w), lambda i: (i, 0))
    return pl.pallas_call(
        functools.partial(_in_proj_kernel, norm=norm),
        grid=(n // tm,),
        in_specs=[row(D_MODEL), _const_spec((1, D_MODEL))] + [_const_spec(w.shape) for w in weights],
        out_specs=[row(w) for w in widths],
        out_shape=[jax.ShapeDtypeStruct((n, w), F32) for w in widths],
        compiler_params=_cparams("arbitrary"),
        name="in_proj",
    )(x, norm_w, *weights)


def _stage_tile(ext_ref, init_ref, cur, halo, tm):
    t = pl.program_id(1)

    @pl.when(t == 0)
    def _():
        ext_ref[0:halo, :] = init_ref[0]

    @pl.when(t > 0)
    def _():
        ext_ref[0:halo, :] = ext_ref[tm:tm + halo, :]

    ext_ref[halo:halo + tm, :] = cur


def _rwkv_prep_kernel(cur_ref, init_ref, mu_ref, w0_ref, w2_ref, a0_ref, a2_ref, g2_ref, kk_ref, ka_ref,
                      ones_ref, wa_ref, bk_ref, rr_ref, v_ref, g_ref, ext_ref, *, halo, bs, tm):
    cur = cur_ref[...]
    _stage_tile(ext_ref, init_ref, cur, halo, tm)
    prev = ext_ref[halo - bs:halo - bs + tm, :]
    zc = cur + (prev - cur) * mu_ref[...]
    c0, c1, c2 = RW_WIDTH, 2 * RW_WIDTH, 3 * RW_WIDTH
    r = zc[:, 0:c0]
    k = zc[:, c0:c1]
    v = zc[:, c1:c2]
    xw = zc[:, c2:c2 + RW_DECAY_LORA]
    xa = zc[:, c2 + RW_DECAY_LORA:c2 + RW_DECAY_LORA + RW_AAA_LORA]
    xg = zc[:, c2 + RW_DECAY_LORA + RW_AAA_LORA:RW_COLS]
    w_raw = -_softplus(-(w0_ref[...] + _dot_f32(jnp.tanh(xw), w2_ref[...]))) - 0.5
    decay = jnp.exp(-jnp.exp(w_raw))
    a = _sigmoid(a0_ref[...] + _dot_f32(xa, a2_ref[...]))
    g = _dot_f32(_sigmoid(xg), g2_ref[...])
    kkr = k * kk_ref[...]
    kk = kkr * lax.rsqrt(_seg_sum(kkr * kkr, ones_ref[...]) + 1e-6)
    wa_ref[:, 0:RW_WIDTH] = decay
    wa_ref[:, RW_WIDTH:2 * RW_WIDTH] = -kk
    bk_ref[:, 0:RW_WIDTH] = kk * a
    bk_ref[:, RW_WIDTH:2 * RW_WIDTH] = k * (1.0 + (a - 1.0) * ka_ref[...])
    rr_ref[:, 0:RW_WIDTH] = r
    rr_ref[:, RW_WIDTH:2 * RW_WIDTH] = r
    v_ref[...] = v
    g_ref[...] = g


def _rwkv_prep(rw_cur, init, p, nb, tm, bs, wide):
    n = rw_cur.shape[0]
    nt = n // (nb * tm)
    halo = init.shape[1]
    row = lambda w: pl.BlockSpec((tm, w), lambda b, t: (b * nt + t, 0))
    if wide:
        out_row = lambda w: pl.BlockSpec((tm, w), lambda b, t: (t, b))
        out_shape = lambda w: jax.ShapeDtypeStruct((n // nb, nb * w), F32)
    else:
        out_row = row
        out_shape = lambda w: jax.ShapeDtypeStruct((n, w), F32)
    consts = [p["mu_shift"], p["rw_w0"], p["rw_w2"], p["rw_a0"], p["rw_a2"], p["rw_g2"], p["rw_k_k"], p["rw_k_a"],
              p["ones_rw"]]
    return pl.pallas_call(
        functools.partial(_rwkv_prep_kernel, halo=halo, bs=bs, tm=tm),
        grid=(nb, nt),
        in_specs=[row(RW_COLS), pl.BlockSpec((1, halo, RW_COLS), lambda b, t: (b, 0, 0))]
        + [_const_spec(c.shape) for c in consts],
        out_specs=[out_row(2 * RW_WIDTH)] * 3 + [out_row(RW_WIDTH), row(RW_WIDTH)],
        out_shape=[out_shape(2 * RW_WIDTH)] * 3 + [out_shape(RW_WIDTH), jax.ShapeDtypeStruct((n, RW_WIDTH), F32)],
        scratch_shapes=[pltpu.VMEM((halo + tm, RW_COLS), F32)],
        compiler_params=_cparams("arbitrary", "arbitrary"),
        name="rwkv_prep",
    )(rw_cur, init, *consts)


def _sublane_allsum(x):
    x = x + pltpu.roll(x, 4, 0)
    x = x + pltpu.roll(x, 2, 0)
    return x + pltpu.roll(x, 1, 0)


def _rwkv_scan_kernel(wa_ref, bk_ref, rr_ref, v_ref, s0_ref, y_ref, sout_ref, s_ref, vec_ref, *, tc):
    c = pl.program_id(1)
    nj = RW_HEAD_DIM // SUBLANES
    first8 = (lax.broadcasted_iota(jnp.int32, (SUBLANES, LANES), 1) // RW_HEADS) % 2 == 0

    @pl.when(c == 0)
    def _():
        s_ref[...] = s0_ref[0]

    def unpack(t, slot):
        for pair, src_ref in enumerate((wa_ref, bk_ref)):
            for jb in range(nj):
                rows = pl.ds(jb * SUBLANES, SUBLANES)
                x = src_ref[0, t, rows, :]
                vec_ref[slot, 2 * pair, rows, :] = jnp.where(first8, x, pltpu.roll(x, RW_HEADS, 1))
                vec_ref[slot, 2 * pair + 1, rows, :] = jnp.where(first8, pltpu.roll(x, LANES - RW_HEADS, 1), x)

    def step(t, slot):
        unpack(jnp.minimum(t + 1, tc - 1), 1 - slot)

        def vec(which, jb):
            rows = pl.ds(jb * SUBLANES, SUBLANES)
            return rr_ref[0, t, rows, :] if which == 4 else vec_ref[slot, which, rows, :]

        for i0 in range(0, SCAN_ROWS, SCAN_ROW_GROUP):
            rows = range(i0, i0 + SCAN_ROW_GROUP)
            s = {i: [s_ref[i, jb * SUBLANES:(jb + 1) * SUBLANES, :] for jb in range(nj)] for i in rows}
            acc = {}
            for jb in range(nj):
                a = vec(1, jb)
                for i in rows:
                    acc[i] = s[i][jb] * a if jb == 0 else acc[i] + s[i][jb] * a
            sa = {i: _sublane_allsum(acc[i]) for i in rows}
            vb = {i: jnp.broadcast_to(v_ref[0, t, i:i + 1, :], (SUBLANES, LANES)) for i in rows}
            accy = {}
            for jb in range(nj):
                w, b, k, r = vec(0, jb), vec(2, jb), vec(3, jb), vec(4, jb)
                for i in rows:
                    sn = s[i][jb] * w + sa[i] * b + vb[i] * k
                    s_ref[i, jb * SUBLANES:(jb + 1) * SUBLANES, :] = sn
                    accy[i] = sn * r if jb == 0 else accy[i] + sn * r
            for i in rows:
                y_ref[0, t, i:i + 1, :] = _sublane_allsum(accy[i])[0:1, :]

    def two_steps(t2, carry):
        step(2 * t2, 0)
        step(2 * t2 + 1, 1)
        return carry

    unpack(0, 0)
    lax.fori_loop(0, tc // 2, two_steps, 0)

    @pl.when(c == pl.num_programs(1) - 1)
    def _():
        sout_ref[0] = s_ref[...]


def _rwkv_scan(wa, bk, rr, vvec, s0, tc):
    g, t = wa.shape[0], wa.shape[1]
    jspec = pl.BlockSpec((1, tc, RW_HEAD_DIM, LANES), lambda gi, c: (gi, c, 0, 0))
    return pl.pallas_call(
        functools.partial(_rwkv_scan_kernel, tc=tc),
        grid=(g, t // tc),
        in_specs=[jspec, jspec, jspec,
                  pl.BlockSpec((1, tc, SCAN_ROWS, LANES), lambda gi, c: (gi, c, 0, 0)),
                  pl.BlockSpec((1, SCAN_ROWS, RW_HEAD_DIM, LANES), lambda gi, c: (gi, 0, 0, 0))],
        out_specs=[pl.BlockSpec((1, tc, SCAN_ROWS, LANES), lambda gi, c: (gi, c, 0, 0)),
                   pl.BlockSpec((1, SCAN_ROWS, RW_HEAD_DIM, LANES), lambda gi, c: (gi, 0, 0, 0))],
        out_shape=[jax.ShapeDtypeStruct((g, t, SCAN_ROWS, LANES), F32),
                   jax.ShapeDtypeStruct((g, SCAN_ROWS, RW_HEAD_DIM, LANES), F32)],
        scratch_shapes=[pltpu.VMEM((SCAN_ROWS, RW_HEAD_DIM, LANES), F32), pltpu.VMEM((2, 4, RW_HEAD_DIM, LANES), F32)],
        compiler_params=_cparams("arbitrary", "arbitrary"),
        name="rwkv_scan",
    )(wa, bk, rr, vvec, s0)


def _to_scan_pair(x, b):
    t, g = x.shape[0], b // SCAN_SEQS
    x = jnp.swapaxes(x.reshape(t, g, LANES, RW_HEAD_DIM), 2, 3)
    return x.reshape(1, t, RW_HEAD_DIM, LANES) if g == 1 else jnp.swapaxes(x, 0, 1)


def _to_scan_rows(x, b):
    t, g = x.shape[0], b // SCAN_SEQS
    x = x.reshape(t, g, SCAN_SEQS, RW_HEADS, 2, SCAN_ROWS)
    return jnp.transpose(x, (1, 0, 5, 2, 4, 3)).reshape(g, t, SCAN_ROWS, LANES)


def _from_scan_rows(y):
    g, t = y.shape[0], y.shape[1]
    y = y.reshape(g, t, SCAN_ROWS, SCAN_SEQS, 2, RW_HEADS)
    return jnp.transpose(y, (1, 0, 3, 5, 4, 2)).reshape(t, g * SCAN_SEQS * RW_WIDTH)


def _state_to_scan(s):
    b = s.shape[0]
    g = b // SCAN_SEQS
    s = s.reshape(g, SCAN_SEQS, RW_HEADS, 2, SCAN_ROWS, RW_HEAD_DIM)
    return jnp.transpose(s, (0, 4, 5, 1, 3, 2)).reshape(g, SCAN_ROWS, RW_HEAD_DIM, LANES)


def _state_from_scan(s, b):
    g = s.shape[0]
    s = s.reshape(g, SCAN_ROWS, RW_HEAD_DIM, SCAN_SEQS, 2, RW_HEADS)
    return jnp.transpose(s, (0, 3, 5, 4, 1, 2)).reshape(b, RW_HEADS, RW_HEAD_DIM, RW_HEAD_DIM)


def _gdn_prep_kernel(x_ref, init_ref, ab_ref, cw_ref, alog_ref, dt_ref, ones_ref, q_ref, k_ref, v_ref, gb_ref,
                     ext_ref, *, halo, bs, tm):
    cur = x_ref[...]
    _stage_tile(ext_ref, init_ref, cur, halo, tm)
    conv = cur * cw_ref[GDN_CONV - 1:GDN_CONV, :]
    for s in range(1, GDN_CONV):
        off = halo - s * bs
        conv = conv + ext_ref[off:off + tm, :] * cw_ref[GDN_CONV - 1 - s:GDN_CONV - s, :]
    qkv = conv * _sigmoid(conv)
    q = qkv[:, 0:GDN_WIDTH]
    k = qkv[:, GDN_WIDTH:2 * GDN_WIDTH]
    ones = ones_ref[...]
    q_ref[...] = q * lax.rsqrt(_seg_sum(q * q, ones) + 1e-6) * (GDN_HEAD_DIM ** -0.5)
    k_ref[...] = k * lax.rsqrt(_seg_sum(k * k, ones) + 1e-6)
    v_ref[...] = qkv[:, 2 * GDN_WIDTH:3 * GDN_WIDTH]
    ab = ab_ref[...]
    g = -jnp.exp(alog_ref[...]) * _softplus(ab + dt_ref[...])
    lane = lax.broadcasted_iota(jnp.int32, ab.shape, 1)
    gb_ref[...] = jnp.where(lane < GDN_HEADS, g, _sigmoid(ab))


def _gdn_prep(qkv_raw, init, ab, p, nb, tm, bs):
    n = qkv_raw.shape[0]
    nt = n // (nb * tm)
    halo = init.shape[1]
    row = lambda w: pl.BlockSpec((tm, w), lambda b, t: (b * nt + t, 0))
    consts = [p["gdn_conv_w"], p["gdn_alog"], p["gdn_dt"], p["ones_gdn"]]
    return pl.pallas_call(
        functools.partial(_gdn_prep_kernel, halo=halo, bs=bs, tm=tm),
        grid=(nb, nt),
        in_specs=[row(GDN_CONV_CH), pl.BlockSpec((1, halo, GDN_CONV_CH), lambda b, t: (b, 0, 0)), row(LANES)]
        + [_const_spec(c.shape) for c in consts],
        out_specs=[row(GDN_WIDTH)] * 3 + [row(LANES)],
        out_shape=[jax.ShapeDtypeStruct((n, GDN_WIDTH), F32)] * 3 + [jax.ShapeDtypeStruct((n, LANES), F32)],
        scratch_shapes=[pltpu.VMEM((halo + tm, GDN_CONV_CH), F32)],
        compiler_params=_cparams("arbitrary", "arbitrary"),
        name="gdn_prep",
    )(qkv_raw, init, ab, *consts)


def _unit_lower_inverses(a_list, eye, chunk):
    n = [-a for a in a_list]
    inv = [eye + x for x in n]
    for _ in range(int(math.log2(chunk)) - 1):
        ns = [_split_bf16(x) for x in n]
        n = [_dot3(x, x) for x in ns]
        ns = [_split_bf16(x) for x in n]
        inv = [x + _dot3(_split_bf16(x), y) for x, y in zip(inv, ns)]
    return inv


def _gdn_chunk_kernel(q_ref, k_ref, v_ref, gb_ref, s0_ref, o_ref, sout_ref, s_ref, *, chunk, nbb, group):
    ci = pl.program_id(1)

    @pl.when(ci == 0)
    def _():
        s_ref[...] = s0_ref[...]

    rows = lax.broadcasted_iota(jnp.int32, (chunk, chunk), 0)
    cols = lax.broadcasted_iota(jnp.int32, (chunk, chunk), 1)
    causal = rows >= cols
    strict = rows > cols
    eye = (rows == cols).astype(F32)
    tri = causal.astype(F32)

    def one_group(gi, carry):
        units = [(gi * group + j, h) for j in range(group) for h in range(GDN_HEADS)]
        lanes = lambda h: slice(h * GDN_HEAD_DIM, (h + 1) * GDN_HEAD_DIM)
        s = [s_ref[b, h] for b, h in units]
        q = [q_ref[b, :, lanes(h)] for b, h in units]
        k = [k_ref[b, :, lanes(h)] for b, h in units]
        v = [v_ref[b, :, lanes(h)] for b, h in units]
        gb = [gb_ref[gi * group + j] for j in range(group)]
        gcs = [_dot_hi(tri, x) for x in gb]
        gc = [gcs[i // GDN_HEADS][:, h:h + 1] for i, (_, h) in enumerate(units)]
        beta = [gb[i // GDN_HEADS][:, GDN_HEADS + h:GDN_HEADS + h + 1] for i, (_, h) in enumerate(units)]
        decay = [jnp.exp(jnp.where(causal, x - jnp.sum(eye * x, axis=0, keepdims=True), -jnp.inf)) for x in gc]
        kb = [x * y for x, y in zip(k, beta)]
        a = [jnp.where(strict, _dot_nt(x, y) * d, 0.0) for x, y, d in zip(kb, k, decay)]
        inv = _unit_lower_inverses(a, eye, chunk)
        egc = [jnp.exp(x) for x in gc]
        rhs = [jnp.concatenate([x * bt, y * e], axis=1) for x, bt, y, e in zip(v, beta, kb, egc)]
        sol = [_dot3(_split_bf16(x), _split_bf16(y)) for x, y in zip(inv, rhs)]
        wS = [_dot(x[:, GDN_HEAD_DIM:2 * GDN_HEAD_DIM], y) for x, y in zip(sol, s)]
        v_new = [x[:, 0:GDN_HEAD_DIM] - y for x, y in zip(sol, wS)]
        qk = [_dot_nt(x, y) * d for x, y, d in zip(q, k, decay)]
        o_state = [_dot(x * e, y) for x, e, y in zip(q, egc, s)]
        o_chunk = [_dot(x, y) for x, y in zip(qk, v_new)]
        g_last = [x[chunk - 1:chunk, :] for x in gc]
        kd_t = [(x * jnp.exp(gl - g)).T for x, gl, g in zip(k, g_last, gc)]
        s_add = [_dot(x, y) for x, y in zip(kd_t, v_new)]
        for i, (b, h) in enumerate(units):
            o_ref[b, :, lanes(h)] = o_state[i] + o_chunk[i]
            s_ref[b, h] = s[i] * jnp.exp(g_last[i]) + s_add[i]
        return carry

    if nbb == group:
        one_group(0, 0)
    else:
        lax.fori_loop(0, nbb // group, one_group, 0)

    @pl.when(ci == pl.num_programs(1) - 1)
    def _():
        sout_ref[...] = s_ref[...]


def _gdn_chunks(q, k, v, gb, s0, chunk):
    b, t_len, _ = q.shape
    nbb = min(b, GDN_SEQ_BLOCK)
    group = min(nbb, GDN_SEQ_GROUP)
    blk = lambda w: pl.BlockSpec((nbb, chunk, w), lambda bi, ci: (bi, ci, 0))
    st = pl.BlockSpec((nbb, GDN_HEADS, GDN_HEAD_DIM, GDN_HEAD_DIM), lambda bi, ci: (bi, 0, 0, 0))
    return pl.pallas_call(
        functools.partial(_gdn_chunk_kernel, chunk=chunk, nbb=nbb, group=group),
        grid=(b // nbb, t_len // chunk),
        in_specs=[blk(GDN_WIDTH)] * 3 + [blk(LANES), st],
        out_specs=[blk(GDN_WIDTH), st],
        out_shape=[jax.ShapeDtypeStruct(q.shape, F32), jax.ShapeDtypeStruct(s0.shape, F32)],
        scratch_shapes=[pltpu.VMEM((nbb, GDN_HEADS, GDN_HEAD_DIM, GDN_HEAD_DIM), F32)],
        compiler_params=_cparams("arbitrary", "arbitrary"),
        name="gdn_chunks",
    )(q, k, v, gb, s0)


def _route(logits, count_ref):
    lane = lax.broadcasted_iota(jnp.int32, logits.shape, 1)
    neg = -jnp.inf
    is_group = (lane >= N_EXPERTS) & (lane < N_EXPERTS + N_GROUPS)
    gl = jnp.where(is_group, logits, neg)
    gmax = jnp.max(gl, axis=-1, keepdims=True)
    gp = 1.0 / jnp.sum(jnp.exp(gl - gmax), axis=-1, keepdims=True)
    gidx = jnp.min(jnp.where(gl == gmax, lane, LANES), axis=-1, keepdims=True) - N_EXPERTS
    lo = gidx * EXPERTS_PER_GROUP
    el = jnp.where((lane >= lo) & (lane < lo + EXPERTS_PER_GROUP), logits, neg)
    m1 = jnp.max(el, axis=-1, keepdims=True)
    i1 = jnp.min(jnp.where(el == m1, lane, LANES), axis=-1, keepdims=True)
    el2 = jnp.where(lane == i1, neg, el)
    m2 = jnp.max(el2, axis=-1, keepdims=True)
    i2 = jnp.min(jnp.where(el2 == m2, lane, LANES), axis=-1, keepdims=True)
    e2 = jnp.exp(m2 - m1)
    den = 1.0 + e2
    tm = logits.shape[0]
    hit1 = lane == i1
    hit2 = lane == i2
    chosen = jnp.where(hit1 | hit2, 1.0, 0.0)
    before = lax.broadcasted_iota(jnp.int32, (tm, tm), 0) > lax.broadcasted_iota(jnp.int32, (tm, tm), 1)
    seen = jnp.dot(before.astype(BF16), chosen.astype(BF16), preferred_element_type=F32) + count_ref[...]
    rank1 = jnp.sum(jnp.where(hit1, seen, 0.0), axis=-1, keepdims=True)
    rank2 = jnp.sum(jnp.where(hit2, seen, 0.0), axis=-1, keepdims=True)
    count_ref[...] += jnp.sum(chosen, axis=0, keepdims=True)
    fields = (i1.astype(F32), i2.astype(F32), gp / den, gp * e2 / den, rank1, rank2)
    out = jnp.zeros(logits.shape, F32)
    for j, f in enumerate(fields):
        out = jnp.where(lane == j, f, out)
    return out


def _merge_kernel(y_ref, r_ref, k_ref, v_ref, g_ref, o_ref, z_ref, gate_ref, h_ref, lnw_ref, lnb_ref, rk_ref,
                  ones_ref, gnw_ref, woa_ref, wob_ref, wo_ref, n2_ref, rw_ref, rb_ref, *rest, nt_main):
    tail_refs, (h1_ref, xt_ref, mg_ref, cnt_ref) = rest[:-4], rest[-4:]
    i = pl.program_id(0)

    @pl.when(i == 0)
    def _():
        cnt_ref[...] = tail_refs[3][...] if tail_refs else jnp.zeros_like(cnt_ref)

    if tail_refs:
        @pl.when(i >= nt_main)
        def _():
            h1_ref[...] = tail_refs[0][...]
            xt_ref[...] = tail_refs[1][...]
            mg_ref[...] = tail_refs[2][...]

        pl.when(i < nt_main)(functools.partial(
            _merge_tile, y_ref, r_ref, k_ref, v_ref, g_ref, o_ref, z_ref, gate_ref, h_ref, lnw_ref, lnb_ref, rk_ref,
            ones_ref, gnw_ref, woa_ref, wob_ref, wo_ref, n2_ref, rw_ref, rb_ref, h1_ref, xt_ref, mg_ref, cnt_ref))
    else:
        _merge_tile(y_ref, r_ref, k_ref, v_ref, g_ref, o_ref, z_ref, gate_ref, h_ref, lnw_ref, lnb_ref, rk_ref,
                    ones_ref, gnw_ref, woa_ref, wob_ref, wo_ref, n2_ref, rw_ref, rb_ref, h1_ref, xt_ref, mg_ref, cnt_ref)


def _merge_tile(y_ref, r_ref, k_ref, v_ref, g_ref, o_ref, z_ref, gate_ref, h_ref, lnw_ref, lnb_ref, rk_ref,
                ones_ref, gnw_ref, woa_ref, wob_ref, wo_ref, n2_ref, rw_ref, rb_ref, h1_ref, xt_ref, mg_ref, cnt_ref):
    tm = h_ref.shape[0]
    halves = [pl.ds(0, tm // 2), pl.ds(tm // 2, tm // 2)]
    ones = ones_ref[...]
    inv_n = 1.0 / RW_HEAD_DIM
    y = [y_ref[p, :] for p in halves]
    mu = [_seg_sum(a, ones) * inv_n for a in y]
    rkk = [_seg_sum(r_ref[p, :] * k_ref[p, :] * rk_ref[...], ones) for p in halves]
    yc = [a - m for a, m in zip(y, mu)]
    var = [_seg_sum(c * c, ones) * inv_n for c in yc]
    ya = [(c * lax.rsqrt(s + RW_GN_EPS) * lnw_ref[...] + lnb_ref[...] + b * v_ref[p, :]) * g_ref[p, :]
          for c, s, b, p in zip(yc, var, rkk, halves)]

    gnw = gnw_ref[...]
    yb = []
    for p in halves:
        z = z_ref[p, :]
        heads = [_rmsnorm(o_ref[p, h * GDN_HEAD_DIM:(h + 1) * GDN_HEAD_DIM], gnw) for h in range(GDN_HEADS)]
        yb.append(jnp.concatenate(heads, axis=1) * (z * _sigmoid(z)))

    out_a = [_dot(a, woa_ref[...]) for a in ya]
    out_b = [_dot(b, wob_ref[...]) for b in yb]
    merged = [_sigmoid(gate_ref[p, 0:D_MODEL]) * a + _sigmoid(gate_ref[p, D_MODEL:2 * D_MODEL]) * b
              for p, a, b in zip(halves, out_a, out_b)]
    h1 = [h_ref[p, :] + _dot(m, wo_ref[...]) for p, m in zip(halves, merged)]
    xt = [_rmsnorm(a, n2_ref[...]) for a in h1]
    for p, a, b in zip(halves, h1, xt):
        h1_ref[p, :] = a
        xt_ref[p, :] = b
    logits = jnp.concatenate([_dot_f32(a, rw_ref[...]) for a in xt], axis=0) + rb_ref[...]
    mg_ref[...] = _route(logits, cnt_ref)


def _merge(y, rr, bk, v, g, o, z, gates, h, p, tm, wide_nt, tail):
    n = h.shape[0]
    nt_main = n // tm
    n_out = n + (tail[0].shape[0] if tail else 0)
    mi = (lambda i: jnp.minimum(i, nt_main - 1)) if tail else (lambda i: i)
    row = lambda w: pl.BlockSpec((tm, w), lambda i: (mi(i), 0))
    out_row = lambda w: pl.BlockSpec((tm, w), lambda i: (i, 0))
    tail_row = lambda w: pl.BlockSpec((tm, w), lambda i: (jnp.maximum(i - nt_main, 0), 0))
    if wide_nt is None:
        part = lambda per_seq, k: pl.BlockSpec((tm, RW_WIDTH), lambda i: (mi(i), k))
    else:
        part = lambda per_seq, k: pl.BlockSpec(
            (tm, RW_WIDTH), lambda i: (mi(i) % wide_nt, (mi(i) // wide_nt) * per_seq + k))
    consts = [p["rw_lnx_w"], p["rw_lnx_b"], p["rw_r_k"], p["ones_rw"], p["gdn_norm_w"], p["w_oA"], p["w_oB"],
              p["w_o"], p["norm2_w"], p["router_w"], p["router_b"]]
    tail_specs = [tail_row(D_MODEL), tail_row(D_MODEL), tail_row(LANES), _const_spec((1, LANES))] if tail else []
    return pl.pallas_call(
        functools.partial(_merge_kernel, nt_main=nt_main),
        grid=(n_out // tm,),
        in_specs=[part(1, 0), part(2, 0), part(2, 1), part(1, 0)] + [row(RW_WIDTH)] * 3
        + [row(2 * D_MODEL), row(D_MODEL)] + [_const_spec(c.shape) for c in consts] + tail_specs,
        out_specs=[out_row(D_MODEL), out_row(D_MODEL), out_row(LANES), _const_spec((1, LANES))],
        out_shape=[jax.ShapeDtypeStruct((n_out, D_MODEL), F32), jax.ShapeDtypeStruct((n_out, D_MODEL), F32),
                   jax.ShapeDtypeStruct((n_out, LANES), F32), jax.ShapeDtypeStruct((1, LANES), F32)],
        compiler_params=_cparams("arbitrary"),
        name="merge",
    )(y, rr, bk, v, g, o, z, gates, h, *consts, *(tail or ()))


def _row_of(ref3, p):
    return ref3.at[lax.shift_right_logical(p, 3), pl.ds(lax.bitwise_and(p, SUBLANES - 1), 1)]


def _start_row_gather(src_hbm, idx_ref, dst_ref, sem, n, priority):
    def body(g, carry):
        for k in range(SUBLANES):
            src = _row_of(src_hbm, idx_ref[0, 0, g * SUBLANES + k])
            pltpu.make_async_copy(src, dst_ref.at[g, pl.ds(k, 1)], sem).start(priority)
        return carry

    lax.fori_loop(0, n // SUBLANES, body, 0)


def _wait_row_gather(src_hbm, dst_ref, sem, n):
    def body(r, carry):
        pltpu.make_async_copy(src_hbm.at[0, pl.ds(0, 1)], dst_ref.at[0, pl.ds(0, 1)], sem).wait()
        return carry

    lax.fori_loop(0, n, body, 0, unroll=8)


def _moe_dispatch_kernel(p1_ref, p2_ref, x_ref, zeros_hbm, xs_hbm, xbuf, sems, *, tm):
    del zeros_hbm
    i = pl.program_id(0)
    nt = pl.num_programs(0)
    slot = i % 2

    def wait_tile(s):
        def body(r, carry):
            pltpu.make_async_copy(xbuf.at[s, 0, pl.ds(0, 1)], xs_hbm.at[0, pl.ds(0, 1)], sems.at[s]).wait()
            return carry

        lax.fori_loop(0, 2 * tm, body, 0, unroll=8)

    @pl.when(i >= 2)
    def _():
        wait_tile(slot)

    xbuf[slot] = x_ref[...].reshape(tm // SUBLANES, SUBLANES, D_MODEL)

    def issue(g, carry):
        for k in range(SUBLANES):
            src = xbuf.at[slot, g, pl.ds(k, 1)]
            for pos_ref, priority in ((p1_ref, 0), (p2_ref, 1)):
                dst = _row_of(xs_hbm, pos_ref[0, 0, g * SUBLANES + k])
                pltpu.make_async_copy(src, dst, sems.at[slot]).start(priority)
        return carry

    lax.fori_loop(0, tm // SUBLANES, issue, 0)

    @pl.when(i == nt - 1)
    def _():
        wait_tile(slot)

        @pl.when(nt > 1)
        def _():
            wait_tile(1 - slot)


def _moe_dispatch(xt, pos1, pos2, n_sorted, tm):
    n = xt.shape[0]
    idx = pl.BlockSpec((1, 1, tm), lambda i: (i, 0, 0), memory_space=pltpu.SMEM)
    return pl.pallas_call(
        functools.partial(_moe_dispatch_kernel, tm=tm),
        grid=(n // tm,),
        in_specs=[idx, idx, pl.BlockSpec((tm, D_MODEL), lambda i: (i, 0)), pl.BlockSpec(memory_space=pl.ANY)],
        out_specs=pl.BlockSpec(memory_space=pl.ANY),
        out_shape=jax.ShapeDtypeStruct((n_sorted // SUBLANES, SUBLANES, D_MODEL), F32),
        scratch_shapes=[pltpu.VMEM((2, tm // SUBLANES, SUBLANES, D_MODEL), F32), pltpu.SemaphoreType.DMA((2,))],
        input_output_aliases={3: 0},
        compiler_params=_cparams("arbitrary"),
        name="moe_dispatch",
    )(pos1, pos2, xt, jnp.zeros((n_sorted // SUBLANES, SUBLANES, D_MODEL), F32)).reshape(n_sorted, D_MODEL)


def _moe_experts_kernel(te_ref, tv_ref, x_ref, wg_ref, wu_ref, wd_ref, y_ref):
    i = pl.program_id(0)

    @pl.when(tv_ref[i] == 1)
    def _():
        xb = x_ref[...].astype(BF16)
        hg = jnp.dot(xb, wg_ref[0].astype(BF16), preferred_element_type=F32)
        hu = jnp.dot(xb, wu_ref[0].astype(BF16), preferred_element_type=F32)
        hid = hg * _sigmoid(hg) * hu
        y_ref[...] = jnp.dot(hid.astype(BF16), wd_ref[0].astype(BF16), preferred_element_type=F32)

    @pl.when(tv_ref[i] == 0)
    def _():
        y_ref[...] = jnp.zeros_like(y_ref)


def _moe_experts(xs, tile_expert, tile_valid, wg, wu, wd, tm):
    nt = xs.shape[0] // tm
    wspec = lambda shape: pl.BlockSpec((1,) + shape, lambda i, te, tv: (te[i], 0, 0))
    rows = pl.BlockSpec((tm, D_MODEL), lambda i, te, tv: (i, 0))
    return pl.pallas_call(
        _moe_experts_kernel,
        grid_spec=pltpu.PrefetchScalarGridSpec(
            num_scalar_prefetch=2,
            grid=(nt,),
            in_specs=[rows, wspec((D_MODEL, EXPERT_FF)), wspec((D_MODEL, EXPERT_FF)), wspec((EXPERT_FF, D_MODEL))],
            out_specs=rows,
        ),
        out_shape=jax.ShapeDtypeStruct(xs.shape, F32),
        compiler_params=_cparams("arbitrary"),
        name="moe_experts",
    )(tile_expert, tile_valid, xs, wg, wu, wd)


def _moe_combine_kernel(cur1_ref, cur2_ref, nxt1_ref, nxt2_ref, ys_hbm, h1_ref, mg_ref, nf_ref, o_ref, otail_ref,
                        buf1, buf2, sems, *, tm, nt_main):
    i = pl.program_id(0)
    nt = pl.num_programs(0)
    slot = i % 2

    @pl.when(i == 0)
    def _():
        _start_row_gather(ys_hbm, cur1_ref, buf1.at[0], sems.at[0, 0], tm, 0)
        _start_row_gather(ys_hbm, cur2_ref, buf2.at[0], sems.at[1, 0], tm, 1)

    @pl.when(i + 1 < nt)
    def _():
        _start_row_gather(ys_hbm, nxt1_ref, buf1.at[1 - slot], sems.at[0, 1 - slot], tm, 0)
        _start_row_gather(ys_hbm, nxt2_ref, buf2.at[1 - slot], sems.at[1, 1 - slot], tm, 1)

    _wait_row_gather(ys_hbm, buf1.at[slot], sems.at[0, slot], tm)
    _wait_row_gather(ys_hbm, buf2.at[slot], sems.at[1, slot], tm)
    mg = mg_ref[...]
    rows = lambda buf: buf[slot].reshape(tm, D_MODEL)
    moe = mg[:, 2:3] * rows(buf1) + mg[:, 3:4] * rows(buf2)
    out = _rmsnorm(h1_ref[...] + moe, nf_ref[...])

    @pl.when(i < nt_main)
    def _():
        o_ref[...] = out

    @pl.when(i >= nt_main)
    def _():
        otail_ref[...] = out


def _moe_combine(ys, pos1, pos2, h1, mg, nf, tm, n_main):
    n = h1.shape[0]
    nt = n // tm
    nt_main = n_main // tm
    cur = pl.BlockSpec((1, 1, tm), lambda i: (i, 0, 0), memory_space=pltpu.SMEM)
    nxt = pl.BlockSpec((1, 1, tm), lambda i: (jnp.minimum(i + 1, nt - 1), 0, 0), memory_space=pltpu.SMEM)
    row = lambda w: pl.BlockSpec((tm, w), lambda i: (i, 0))
    return pl.pallas_call(
        functools.partial(_moe_combine_kernel, tm=tm, nt_main=nt_main),
        grid=(nt,),
        in_specs=[cur, cur, nxt, nxt, pl.BlockSpec(memory_space=pl.ANY), row(D_MODEL), row(LANES),
                  _const_spec((1, D_MODEL))],
        out_specs=[pl.BlockSpec((tm, D_MODEL), lambda i: (jnp.minimum(i, nt_main - 1), 0)),
                   pl.BlockSpec((tm, D_MODEL), lambda i: (jnp.maximum(i - nt_main, 0), 0))],
        out_shape=[jax.ShapeDtypeStruct((n_main, D_MODEL), F32), jax.ShapeDtypeStruct((n - n_main, D_MODEL), F32)],
        scratch_shapes=[pltpu.VMEM((2, tm // SUBLANES, SUBLANES, D_MODEL), F32)] * 2 + [pltpu.SemaphoreType.DMA((2, 2))],
        compiler_params=_cparams("arbitrary"),
        name="moe_combine",
    )(pos1, pos2, pos1, pos2, ys.reshape(-1, SUBLANES, D_MODEL), h1, mg, nf)


def _moe(xt, h1, mg, counts, wg, wu, wd, nf, tm_expert, tm_token, n_main):
    n = xt.shape[0]
    i32 = jnp.int32
    expert = jnp.arange(N_EXPERTS, dtype=i32)
    cnt = counts[0, :N_EXPERTS].astype(i32)
    padded = (cnt + tm_expert - 1) // tm_expert * tm_expert
    ends = jnp.sum(jnp.where(expert[:, None] >= expert[None, :], padded[None, :], 0), axis=1)
    starts = ends - padded
    start_of = lambda e: jnp.sum(jnp.where(e[:, None] == expert[None, :], starts[None, :], 0), axis=1)
    pos1 = start_of(mg[:, 0].astype(i32)) + mg[:, 4].astype(i32)
    pos2 = start_of(mg[:, 1].astype(i32)) + mg[:, 5].astype(i32)
    nt = (2 * n + N_EXPERTS * (tm_expert - 1)) // tm_expert + 1
    tile_start = jnp.arange(nt, dtype=i32) * tm_expert
    tile_valid = (tile_start < jnp.sum(padded)).astype(i32)
    tile_expert = jnp.sum((tile_start[:, None] >= ends[None, :]).astype(i32), axis=1)
    last_used = jnp.max(jnp.where(padded > 0, expert, 0))
    tile_expert = jnp.where(tile_valid == 1, tile_expert, last_used)
    shape = (n // tm_token, 1, tm_token)
    pos1, pos2 = pos1.reshape(shape), pos2.reshape(shape)
    xs = _moe_dispatch(xt, pos1, pos2, nt * tm_expert, tm_token)
    ys = _moe_experts(xs, tile_expert, tile_valid, wg, wu, wd, tm_expert)
    return _moe_combine(ys, pos1, pos2, h1, mg, nf, tm_token, n_main)


def _pad_rows(x, rows):
    return jnp.pad(x, ((0, rows - x.shape[0]),) + ((0, 0),) * (x.ndim - 1))


def _mixer(x_rows, p, *, batch, t_len, time_major, tm, rw_prev, conv_prev, wkv0, gdn0, chunk, with_merge, tail=None):
    n = batch * t_len
    xn, rw_cur, qkv_raw, z, gates, ab = _in_proj(x_rows, p["norm1_w"], p["w_in"], min(tm, n))

    if time_major:
        nb, bs, tmt = 1, batch, n
        rw_init = rw_prev[None]
        conv_init = jnp.transpose(conv_prev, (1, 0, 2)).reshape(1, (GDN_CONV - 1) * batch, GDN_CONV_CH)
    else:
        nb, bs, tmt = batch, 1, min(tm, t_len)
        rw_init = jnp.pad(rw_prev[:, None, :], ((0, 0), (SUBLANES - 1, 0), (0, 0)))
        conv_init = jnp.pad(conv_prev, ((0, 0), (SUBLANES - (GDN_CONV - 1), 0), (0, 0)))

    wide = not time_major
    wa, bk, rr, v, g = _rwkv_prep(rw_cur, rw_init, p, nb, tmt, bs, wide)
    q, kg, vg, gb = _gdn_prep(qkv_raw, conv_init, ab, p, nb, tmt, bs)

    def seq_major(a):
        if time_major:
            return jnp.transpose(a.reshape(t_len, batch, a.shape[-1]), (1, 0, 2))
        return a.reshape(batch, t_len, a.shape[-1])

    def rows_like_x(a):
        if time_major:
            a = jnp.transpose(a, (1, 0, 2))
        return a.reshape(n, a.shape[-1])

    t_pad = -(-t_len // chunk) * chunk
    def gdn_seq(a):
        a = seq_major(a)
        if t_pad != t_len:
            a = jnp.pad(a, ((0, 0), (0, t_pad - t_len), (0, 0)))
        return a
    o_pad, gdn = _gdn_chunks(gdn_seq(q), gdn_seq(kg), gdn_seq(vg), gdn_seq(gb), gdn0, chunk)

    bpad = -(-batch // SCAN_SEQS) * SCAN_SEQS

    def by_time(a):
        a = a.reshape(t_len, -1)
        return jnp.pad(a, ((0, 0), (0, a.shape[1] // batch * (bpad - batch))))

    scan_in = tuple(_to_scan_pair(by_time(a), bpad) for a in (wa, bk, rr)) + (_to_scan_rows(by_time(v), bpad),)
    scan_in, o_pad = lax.optimization_barrier((scan_in, o_pad))
    tc = min(t_len, 32)
    y_scan, s_scan = _rwkv_scan(*scan_in, _state_to_scan(_pad_rows(wkv0, bpad)), tc)
    y_rw = _from_scan_rows(y_scan)[:, :batch * RW_WIDTH]
    if time_major:
        y_rw = y_rw.reshape(n, RW_WIDTH)
    wkv = _state_from_scan(s_scan, bpad)[:batch]
    o = rows_like_x(o_pad[:, :t_len])

    out = dict(xn=xn, rw_cur=rw_cur, qkv_raw=qkv_raw, wkv=wkv, gdn=gdn)
    if with_merge:
        out["h1"], out["xt"], out["mg"], out["counts"] = _merge(y_rw, rr, bk, v, g, o, z, gates, x_rows, p, min(tm, n),
                                                                 t_len // tmt if wide else None, tail)
    return out


def kernel(x_prompt, x_sample, state_shift, state_wkv, state_conv, state_gdn, meta_tokens, norm1_w, w_in, mu_shift, rw_w0, rw_w2, rw_a0, rw_a2, rw_g2, rw_k_k, rw_k_a, rw_r_k, rw_lnx_w, rw_lnx_b, gdn_conv_w, gdn_A_log, gdn_dt_bias, gdn_norm_w, w_oA, w_oB, w_o, norm2_w, router_g, router_g_b, router_e, router_e_b, moe_w_gate, moe_w_up, moe_w_down, norm_f_w):
    bp, sp, _ = x_prompt.shape
    bsm, ss, _ = x_sample.shape
    row = lambda a: a[0].reshape(1, -1)
    w0 = w_in[0]
    lane_pad = lambda a: jnp.pad(a, ((0, 0), (0, LANES - a.shape[1])))
    w_packed = (w0[:, :COL_AB].astype(BF16), w0[:, COL_GATE:].astype(BF16),
                lane_pad(w0[:, COL_AB:COL_GATE]).astype(BF16))
    p = dict(
        norm1_w=row(norm1_w), w_in=w_packed, mu_shift=row(mu_shift), rw_w0=row(rw_w0), rw_w2=rw_w2[0],
        rw_a0=row(rw_a0), rw_a2=rw_a2[0], rw_g2=rw_g2[0], rw_k_k=row(rw_k_k), rw_k_a=row(rw_k_a),
        rw_r_k=row(rw_r_k), rw_lnx_w=row(rw_lnx_w), rw_lnx_b=row(rw_lnx_b),
        ones_rw=_block_ones(RW_WIDTH, RW_HEAD_DIM), ones_gdn=_block_ones(GDN_WIDTH, GDN_HEAD_DIM),
        gdn_conv_w=gdn_conv_w[0], gdn_alog=lane_pad(row(gdn_A_log)), gdn_dt=lane_pad(row(gdn_dt_bias)),
        gdn_norm_w=row(gdn_norm_w), w_oA=w_oA[0].astype(BF16), w_oB=w_oB[0].astype(BF16),
        w_o=w_o[0].astype(BF16), norm2_w=row(norm2_w),
        router_w=lane_pad(jnp.concatenate([router_e[0], router_g[0]], axis=1)),
        router_b=lane_pad(jnp.concatenate([row(router_e_b), row(router_g_b)], axis=1)),
    )
    nf = norm_f_w.reshape(1, -1)
    dt = x_prompt.dtype

    meta = _mixer(meta_tokens.astype(dt), p, batch=1, t_len=N_META, time_major=False, tm=N_META,
                  rw_prev=jnp.zeros((1, RW_COLS), dt), conv_prev=jnp.zeros((1, GDN_CONV - 1, GDN_CONV_CH), dt),
                  wkv0=jnp.zeros((1, RW_HEADS, RW_HEAD_DIM, RW_HEAD_DIM), dt),
                  gdn0=jnp.zeros((1, GDN_HEADS, GDN_HEAD_DIM, GDN_HEAD_DIM), dt), chunk=N_META, with_merge=False)
    rep = lambda a: jnp.broadcast_to(a, (bp,) + a.shape[1:])

    xs_rows = jnp.transpose(x_sample, (1, 0, 2)).reshape(ss * bsm, D_MODEL)
    prev_s = _in_proj(state_shift[0].astype(dt), p["norm1_w"], w_packed, bsm, norm=False)[1]
    sample = _mixer(xs_rows, p, batch=bsm, t_len=ss, time_major=True, tm=256,
                    rw_prev=prev_s, conv_prev=state_conv[0].astype(dt), wkv0=state_wkv[0], gdn0=state_gdn[0],
                    chunk=SUBLANES, with_merge=True)

    prompt = _mixer(x_prompt.reshape(bp * sp, D_MODEL), p, batch=bp, t_len=sp, time_major=False, tm=256,
                    rw_prev=rep(meta["rw_cur"][N_META - 1:]), conv_prev=rep(meta["qkv_raw"][None, N_META - 3:]),
                    wkv0=rep(meta["wkv"]), gdn0=rep(meta["gdn"]), chunk=GDN_CHUNK, with_merge=True,
                    tail=(sample["h1"], sample["xt"], sample["mg"], sample["counts"]))

    y_p, y_s = _moe(prompt["xt"], prompt["h1"], prompt["mg"], prompt["counts"], moe_w_gate[0], moe_w_up[0],
                    moe_w_down[0], nf, 256, 256, bp * sp)

    y_prompt = y_p.reshape(bp, sp, D_MODEL)
    y_sample = jnp.transpose(y_s.reshape(ss, bsm, D_MODEL), (1, 0, 2))
    shift_p = prompt["xn"].reshape(bp, sp, D_MODEL)[:, -1][None]
    conv_p = prompt["qkv_raw"].reshape(bp, sp, GDN_CONV_CH)[:, sp - (GDN_CONV - 1):][None]
    shift_s = sample["xn"].reshape(ss, bsm, D_MODEL)[-1][None].astype(state_shift.dtype)
    xpad_s = jnp.concatenate([state_conv[0].astype(dt),
                              jnp.transpose(sample["qkv_raw"].reshape(ss, bsm, GDN_CONV_CH), (1, 0, 2))], axis=1)
    conv_s = xpad_s[:, ss:][None].astype(state_conv.dtype)
    return (y_prompt, y_sample, shift_p, prompt["wkv"][None], conv_p, prompt["gdn"][None],
            shift_s, sample["wkv"][None].astype(state_wkv.dtype), conv_s, sample["gdn"][None].astype(state_gdn.dtype))
```
